```python
import jax
import jax.numpy as jnp
from jax import lax
import numpy as np

D_MODEL = 2048
BATCH = 8
SEQ = 2048
DEPTH = 2

MIX_WIDTH = D_MODEL // 2
N_BRANCH = 3
GDN_DK = 128
GDN_DV = 128
GDN_HEADS = MIX_WIDTH // GDN_DV
GDN_CHUNK = 64
LRU_WIDTH = MIX_WIDTH
LRU_BLOCKS = 8
LRU_C = 8.0
GLA_HEADS = 4
GLA_DV = MIX_WIDTH // GLA_HEADS
GLA_DK = GLA_DV // 2
GLA_GATE_RANK = 16
GLA_GATE_NORM = 16.0
GLA_CHUNK = 16
CONV_K = 4
D_FF = 4 * D_MODEL
N_MOD = 6
EPS = 1e-6

GDN_QK_W = GDN_HEADS * GDN_DK
GDN_V_W = GDN_HEADS * GDN_DV
GLA_QK_W = GLA_HEADS * GLA_DK
GLA_V_W = GLA_HEADS * GLA_DV
IN_SPLITS = (GDN_QK_W, GDN_QK_W, GDN_V_W, GDN_V_W, GDN_HEADS, GDN_HEADS,
             LRU_WIDTH, LRU_WIDTH,
             GLA_QK_W, GLA_QK_W, GLA_V_W, GLA_V_W, GLA_GATE_RANK,
             N_BRANCH * D_MODEL)
D_IN = sum(IN_SPLITS)

kernel_name = 'hybrid_gdn_rglru_gla_adaln_block'


def rmsnorm(x, g):
    xf = x.astype(jnp.float32)
    y = xf * lax.rsqrt(jnp.mean(xf * xf, axis=-1, keepdims=True) + EPS)
    return (y * g.astype(jnp.float32)).astype(x.dtype)


def l2norm(x):
    return x * lax.rsqrt(jnp.sum(x * x, axis=-1, keepdims=True) + EPS)


def causal_dwconv(x, w):
    K, C = w.shape
    return lax.conv_general_dilated(x, w[:, None, :].astype(x.dtype), (1,), [(K - 1, 0)],
                                    dimension_numbers=('NWC', 'WIO', 'NWC'),
                                    feature_group_count=C)


def split_heads(x, n_heads):
    B_, T, _ = x.shape
    return x.reshape(B_, T, n_heads, -1).transpose(0, 2, 1, 3)


def to_chunks(x, c):
    return x.reshape(x.shape[:2] + (x.shape[2] // c, c) + x.shape[3:])


def head_norm_gate(o, z, w):
    B_, H, T, d = o.shape
    o = o.transpose(0, 2, 1, 3)
    o = o * lax.rsqrt(jnp.mean(o * o, axis=-1, keepdims=True) + EPS) * w.astype(jnp.float32)
    return (o * jax.nn.silu(z.astype(jnp.float32).reshape(B_, T, H, d))).reshape(B_, T, H * d)


def gated_delta_rule(q, k, v, g, beta):
    C = GDN_CHUNK
    B_, H, T, dk = q.shape
    dv = v.shape[-1]
    q, k, v = (to_chunks(t, C) for t in (q, k, v))
    g, beta = to_chunks(g, C), to_chunks(beta, C)
    gc = jnp.cumsum(g, axis=-1)
    causal = jnp.tril(jnp.ones((C, C), bool))
    strict = jnp.tril(jnp.ones((C, C), bool), -1)
    diff = gc[..., :, None] - gc[..., None, :]
    gamma = jnp.where(causal, jnp.exp(jnp.where(causal, diff, 0.0)), 0.0)
    kb = k * beta[..., None]
    a_kk = jnp.where(strict, jnp.einsum('bhnid,bhnjd->bhnij', kb, k) * gamma, 0.0)
    rhs = jnp.concatenate([v * beta[..., None], kb * jnp.exp(gc)[..., None]], axis=-1)
    sol = lax.linalg.triangular_solve(a_kk + jnp.eye(C, dtype=a_kk.dtype), rhs,
                                      left_side=True, lower=True, unit_diagonal=True)
    u, w = sol[..., :dv], sol[..., dv:]
    a_qk = jnp.where(causal, jnp.einsum('bhnid,bhnjd->bhnij', q, k) * gamma, 0.0)
    q_dec = q * jnp.exp(gc)[..., None]
    k_tail = k * jnp.exp(gc[..., -1:] - gc)[..., None]
    c_dec = jnp.exp(gc[..., -1])

    def step(S, xs):
        u_c, w_c, q_c, k_c, a_c, d_c = xs
        v_new = u_c - jnp.einsum('bhcd,bhde->bhce', w_c, S)
        o = jnp.einsum('bhcd,bhde->bhce', q_c, S) + jnp.einsum('bhij,bhje->bhie', a_c, v_new)
        S = S * d_c[..., None, None] + jnp.einsum('bhcd,bhce->bhde', k_c, v_new)
        return S, o

    xs = tuple(jnp.moveaxis(t, 2, 0) for t in (u, w, q_dec, k_tail, a_qk, c_dec))
    S0 = jnp.zeros((B_, H, dk, dv), q.dtype)
    _, o = lax.scan(step, S0, xs)
    return jnp.moveaxis(o, 0, 2).reshape(B_, H, T, dv)


def gla_chunked(q, k, v, log_a):
    C = GLA_CHUNK
    B_, H, T, dk = q.shape
    dv = v.shape[-1]
    q, k, v, log_a = (to_chunks(t, C) for t in (q, k, v, log_a))
    b = jnp.cumsum(log_a, axis=-2)
    q_in = q * jnp.exp(b)
    k_in = k * jnp.exp(-b)
    k_out = k * jnp.exp(b[..., -1:, :] - b)
    causal = jnp.tril(jnp.ones((C, C), bool))
    a_qk = jnp.where(causal, jnp.einsum('bhnid,bhnjd->bhnij', q_in, k_in), 0.0)
    o_intra = jnp.einsum('bhnij,bhnje->bhnie', a_qk, v)
    c_dec = jnp.exp(b[..., -1, :])

    def step(S, xs):
        q_c, k_c, v_c, d_c = xs
        o = jnp.einsum('bhcd,bhde->bhce', q_c, S)
        S = S * d_c[..., None] + jnp.einsum('bhcd,bhce->bhde', k_c, v_c)
        return S, o

    xs = tuple(jnp.moveaxis(t, 2, 0) for t in (q_in, k_out, v, c_dec))
    S0 = jnp.zeros((B_, H, dk, dv), q.dtype)
    _, o_inter = lax.scan(step, S0, xs)
    return (o_intra + jnp.moveaxis(o_inter, 0, 2)).reshape(B_, H, T, dv)


def _linear_combine(left, right):
    a_l, b_l = left
    a_r, b_r = right
    return a_l * a_r, a_r * b_l + b_r


def rg_lru(x, w_a, b_a, w_i, b_i, lam):
    f32 = jnp.float32
    B_, T, W = x.shape
    xb = x.reshape(B_, T, LRU_BLOCKS, W // LRU_BLOCKS)
    r = jax.nn.sigmoid(jnp.einsum('btki,kij->btkj', xb, w_a.astype(f32)).reshape(B_, T, W) + b_a.astype(f32))
    i = jax.nn.sigmoid(jnp.einsum('btki,kij->btkj', xb, w_i.astype(f32)).reshape(B_, T, W) + b_i.astype(f32))
    log_a = -LRU_C * r * jax.nn.softplus(-lam.astype(f32))
    first = (jnp.arange(T) == 0)[None, :, None]
    mult = jnp.where(first, 1.0, jnp.sqrt(-jnp.expm1(2.0 * log_a)))
    _, hs = lax.associative_scan(_linear_combine, (jnp.exp(log_a), mult * i * x), axis=1)
    return hs


def token_mixer(h, w_in, conv_gdn, gdn_a_log, gdn_dt_bias, gdn_norm, conv_lru, conv_lru_b,
                lru_w_a, lru_b_a, lru_w_i, lru_b_i, lru_lambda, gla_w_gate, gla_b_gate,
                gla_norm, w_branch, w_out):
    f32 = jnp.float32
    split_at = np.cumsum(IN_SPLITS)[:-1].tolist()
    (qa, ka, va, za, beta_a, alpha_a, xb, yb,
     qc, kc, vc, zc, gk_c, gates) = jnp.split(h @ w_in, split_at, axis=-1)

    qkv = jax.nn.silu(causal_dwconv(jnp.concatenate([qa, ka, va], axis=-1), conv_gdn)).astype(f32)
    qa, ka, va = jnp.split(qkv, [GDN_QK_W, 2 * GDN_QK_W], axis=-1)
    qa = l2norm(split_heads(qa, GDN_HEADS)) * (GDN_DK ** -0.5)
    ka = l2norm(split_heads(ka, GDN_HEADS))
    va = split_heads(va, GDN_HEADS)
    beta = jax.nn.sigmoid(beta_a.astype(f32)).transpose(0, 2, 1)
    g = -(jnp.exp(gdn_a_log.astype(f32))
          * jax.nn.softplus(alpha_a.astype(f32) + gdn_dt_bias.astype(f32))).transpose(0, 2, 1)
    oa = head_norm_gate(gated_delta_rule(qa, ka, va, g, beta), za, gdn_norm)

    xl = causal_dwconv(xb, conv_lru) + conv_lru_b
    hl = rg_lru(xl.astype(f32), lru_w_a, lru_b_a, lru_w_i, lru_b_i, lru_lambda)
    ob = jax.nn.gelu(yb.astype(f32)) * hl

    qc = split_heads(qc.astype(f32), GLA_HEADS) * (GLA_DK ** -0.5)
    kc = split_heads(kc.astype(f32), GLA_HEADS)
    vc = split_heads(vc.astype(f32), GLA_HEADS)
    log_a = jax.nn.log_sigmoid((gk_c @ gla_w_gate + gla_b_gate).astype(f32)) / GLA_GATE_NORM
    oc = head_norm_gate(gla_chunked(qc, kc, vc, split_heads(log_a, GLA_HEADS)), zc, gla_norm)

    gates = jax.nn.sigmoid(gates.astype(f32)).reshape(h.shape[:2] + (N_BRANCH, D_MODEL))
    merged = 0.0
    for n, o in enumerate((oa, ob, oc)):
        merged = merged + gates[:, :, n] * (o.astype(h.dtype) @ w_branch[n])
    return merged.astype(h.dtype) @ w_out


def sq_relu_mlp(h, w1, w2):
    return jnp.square(jax.nn.relu(h @ w1)) @ w2


def setup_inputs(seed: int = 0) -> dict:
    key = jax.random.key(seed)
    ks = iter(jax.random.split(key, 40))

    def nrm(shape, std):
        return std * jax.random.normal(next(ks), shape, jnp.float32)

    def unif(shape, lo, hi):
        return jax.random.uniform(next(ks), shape, jnp.float32, lo, hi)

    L = DEPTH
    x = nrm((BATCH, SEQ, D_MODEL), 1.0)
    c = nrm((BATCH, D_MODEL), 1.0)
    w_ada = nrm((L, D_MODEL, N_MOD * D_MODEL), 0.5 * D_MODEL ** -0.5)
    b_ada = nrm((L, N_MOD * D_MODEL), 0.01)
    g_pre_mix = 1.0 + nrm((L, D_MODEL), 0.05)
    g_post_mix = 1.0 + nrm((L, D_MODEL), 0.05)
    g_pre_mlp = 1.0 + nrm((L, D_MODEL), 0.05)
    g_post_mlp = 1.0 + nrm((L, D_MODEL), 0.05)
    w_in = nrm((L, D_MODEL, D_IN), D_MODEL ** -0.5)
    conv_gdn = nrm((L, CONV_K, 2 * GDN_QK_W + GDN_V_W), CONV_K ** -0.5)
    gdn_a_log = jnp.log(unif((L, GDN_HEADS), 1.0, 16.0))
    dt = jnp.exp(unif((L, GDN_HEADS), float(np.log(1e-3)), float(np.log(1e-1))))
    gdn_dt_bias = dt + jnp.log(-jnp.expm1(-dt))
    gdn_norm = 1.0 + nrm((L, GDN_DV), 0.05)
    conv_lru = nrm((L, CONV_K, LRU_WIDTH), CONV_K ** -0.5)
    conv_lru_b = nrm((L, LRU_WIDTH), 0.01)
    blk = LRU_WIDTH // LRU_BLOCKS
    lru_w_a = nrm((L, LRU_BLOCKS, blk, blk), blk ** -0.5)
    lru_b_a = nrm((L, LRU_WIDTH), 0.01)
    lru_w_i = nrm((L, LRU_BLOCKS, blk, blk), blk ** -0.5)
    lru_b_i = nrm((L, LRU_WIDTH), 0.01)
    rho = unif((L, LRU_WIDTH), 0.9, 0.999)
    s = rho ** (1.0 / LRU_C)
    lru_lambda = jnp.log(s) - jnp.log1p(-s)
    gla_w_gate = nrm((L, GLA_GATE_RANK, GLA_QK_W), GLA_GATE_RANK ** -0.5)
    gla_b_gate = nrm((L, GLA_QK_W), 0.01)
    gla_norm = 1.0 + nrm((L, GLA_DV), 0.05)
    w_branch = nrm((L, N_BRANCH, MIX_WIDTH, D_MODEL), MIX_WIDTH ** -0.5)
    w_out = nrm((L, D_MODEL, D_MODEL), D_MODEL ** -0.5)
    w_mlp1 = nrm((L, D_MODEL, D_FF), D_MODEL ** -0.5)
    w_mlp2 = nrm((L, D_FF, D_MODEL), D_FF ** -0.5)
    return {'x': x, 'c': c, 'w_ada': w_ada, 'b_ada': b_ada,
            'g_pre_mix': g_pre_mix, 'g_post_mix': g_post_mix,
            'g_pre_mlp': g_pre_mlp, 'g_post_mlp': g_post_mlp,
            'w_in': w_in, 'conv_gdn': conv_gdn, 'gdn_a_log': gdn_a_log,
            'gdn_dt_bias': gdn_dt_bias, 'gdn_norm': gdn_norm,
            'conv_lru': conv_lru, 'conv_lru_b': conv_lru_b,
            'lru_w_a': lru_w_a, 'lru_b_a': lru_b_a, 'lru_w_i': lru_w_i, 'lru_b_i': lru_b_i,
            'lru_lambda': lru_lambda, 'gla_w_gate': gla_w_gate, 'gla_b_gate': gla_b_gate,
            'gla_norm': gla_norm, 'w_branch': w_branch, 'w_out': w_out,
            'w_mlp1': w_mlp1, 'w_mlp2': w_mlp2}


def reference(x, c, w_ada, b_ada, g_pre_mix, g_post_mix, g_pre_mlp, g_post_mlp,
              w_in, conv_gdn, gdn_a_log, gdn_dt_bias, gdn_norm, conv_lru, conv_lru_b,
              lru_w_a, lru_b_a, lru_w_i, lru_b_i, lru_lambda, gla_w_gate, gla_b_gate,
              gla_norm, w_branch, w_out, w_mlp1, w_mlp2):
    for l in range(DEPTH):
        mod = jax.nn.silu(c) @ w_ada[l] + b_ada[l]
        sh1, sc1, gt1, sh2, sc2, gt2 = jnp.split(mod[:, None, :], N_MOD, axis=-1)
        h = rmsnorm(x, g_pre_mix[l]) * (1.0 + sc1) + sh1
        y = token_mixer(h, w_in[l], conv_gdn[l], gdn_a_log[l], gdn_dt_bias[l], gdn_norm[l],
                        conv_lru[l], conv_lru_b[l], lru_w_a[l], lru_b_a[l], lru_w_i[l], lru_b_i[l],
                        lru_lambda[l], gla_w_gate[l], gla_b_gate[l], gla_norm[l],
                        w_branch[l], w_out[l])
        x = x + gt1 * rmsnorm(y, g_post_mix[l])
        h = rmsnorm(x, g_pre_mlp[l]) * (1.0 + sc2) + sh2
        x = x + gt2 * rmsnorm(sq_relu_mlp(h, w_mlp1[l], w_mlp2[l]), g_post_mlp[l])
    return x
```

```python
import functools

import jax
import jax.numpy as jnp
from jax import lax
from jax.experimental import pallas as pl
from jax.experimental.pallas import tpu as pltpu

F32 = jnp.float32
BF16 = jnp.bfloat16
HIGHEST = lax.Precision.HIGHEST

D_MODEL = 2048
MIX_WIDTH = D_MODEL // 2
N_BRANCH = 3
GDN_DK = 128
GDN_DV = 128
GDN_HEADS = MIX_WIDTH // GDN_DV
GDN_CHUNK = 64
LRU_WIDTH = MIX_WIDTH
LRU_BLOCKS = 8
LRU_BLOCK_W = LRU_WIDTH // LRU_BLOCKS
LRU_C = 8.0
GLA_HEADS = 4
GLA_DV = MIX_WIDTH // GLA_HEADS
GLA_DK = GLA_DV // 2
GLA_GATE_RANK = 16
GLA_GATE_NORM = 16.0
GLA_BASE_CHUNK = 16
CONV_K = 4
D_FF = 4 * D_MODEL
N_MOD = 6
EPS = 1e-6

GDN_QK_W = GDN_HEADS * GDN_DK
GDN_V_W = GDN_HEADS * GDN_DV
GLA_QK_W = GLA_HEADS * GLA_DK
GLA_V_W = GLA_HEADS * GLA_DV
IN_SPLITS = (GDN_QK_W, GDN_QK_W, GDN_V_W, GDN_V_W, GDN_HEADS, GDN_HEADS,
             LRU_WIDTH, LRU_WIDTH,
             GLA_QK_W, GLA_QK_W, GLA_V_W, GLA_V_W, GLA_GATE_RANK,
             N_BRANCH * D_MODEL)

LANES = 128
SUBLANES = 8
VMEM_LIMIT_BYTES = 56 * 1024 * 1024

BIG_GDN = 0
BIG_LRU = BIG_GDN + 4 * MIX_WIDTH
BIG_GLA = BIG_LRU + 2 * LRU_WIDTH
BIG_GATES = BIG_GLA + 2 * GLA_QK_W + 2 * GLA_V_W
BIG_W = BIG_GATES + N_BRANCH * D_MODEL
SMALL_BETA = 0
SMALL_ALPHA = GDN_HEADS
SMALL_GK = 2 * GDN_HEADS
SMALL_W = LANES

MOD_SH1, MOD_SC1, MOD_GT1, MOD_SH2, MOD_SC2, MOD_GT2 = range(N_MOD)

TIME_TILE = 128
LRU_TIME_TILE = 256
HALO = SUBLANES


def _sigmoid(x):
    return 1.0 / (1.0 + jnp.exp(-x))


def _silu(x):
    return x * _sigmoid(x)


def _softplus(x):
    return jnp.maximum(x, 0.0) + jnp.log(1.0 + jnp.exp(-jnp.abs(x)))


def _log_sigmoid(x):
    return -_softplus(-x)


def _gelu_tanh(x):
    c = 0.7978845608028654
    return 0.5 * x * (1.0 + jnp.tanh(c * (x + 0.044715 * (x * x * x))))


def _rms(x):
    return x * lax.rsqrt(jnp.mean(x * x, axis=-1, keepdims=True) + EPS)


def _dot(a, b, precision=None):
    return jnp.dot(a, b, preferred_element_type=F32, precision=precision)


def _dot_nt(a, b):
    return lax.dot_general(a, b, (((1,), (1,)), ((), ())), preferred_element_type=F32)


def _bdot(a, b):
    return _dot(a.astype(BF16), b.astype(BF16))


def _bdot_nt(a, b):
    return _dot_nt(a.astype(BF16), b.astype(BF16))


def _params(*sem):
    return pltpu.CompilerParams(dimension_semantics=sem, vmem_limit_bytes=VMEM_LIMIT_BYTES)


def _ada_kernel(c_ref, w_ref, b_ref, o_ref):
    sc = _silu(c_ref[...]).astype(BF16)
    o_ref[...] = _dot(sc, w_ref[...].astype(BF16)) + b_ref[...]


def _ada_mod(c, w_ada, b_ada):
    n_layers, _, n_out = w_ada.shape
    bsz = c.shape[0]
    tn = 1024
    return pl.pallas_call(
        _ada_kernel,
        out_shape=jax.ShapeDtypeStruct((n_layers, bsz, n_out), F32),
        grid=(n_layers, n_out // tn),
        in_specs=[
            pl.BlockSpec((bsz, D_MODEL), lambda l, j: (0, 0)),
            pl.BlockSpec((None, D_MODEL, tn), lambda l, j: (l, 0, j)),
            pl.BlockSpec((None, 1, tn), lambda l, j: (l, 0, j)),
        ],
        out_specs=pl.BlockSpec((None, bsz, tn), lambda l, j: (l, 0, j)),
        compiler_params=_params("parallel", "parallel"),
        name="ada_mod",
    )(c, w_ada, b_ada.reshape(n_layers, 1, n_out))


def _in_proj_kernel(x_ref, mod_ref, g_ref, w_ref, ws_ref, wst_ref,
                    big_ref, small_ref, smallt_ref, h_scr):
    @pl.when(pl.program_id(1) == 0)
    def _():
        h = _rms(x_ref[...]) * g_ref[...]
        h = h * (1.0 + mod_ref[MOD_SC1:MOD_SC1 + 1, :]) + mod_ref[MOD_SH1:MOD_SH1 + 1, :]
        hb = h.astype(BF16)
        h_scr[...] = hb
        small_ref[...] = _dot(hb, ws_ref[...])
        smallt_ref[...] = _dot_nt(wst_ref[...], hb)

    big_ref[...] = _dot(h_scr[...], w_ref[...])


def _in_proj(x2, mod, g, w_big, w_small, w_small_t, seq):
    m = x2.shape[0]
    tm = min(1024, seq)
    tn = 1024
    tiles_per_seq = seq // tm
    return pl.pallas_call(
        _in_proj_kernel,
        out_shape=(jax.ShapeDtypeStruct((m, BIG_W), F32),
                   jax.ShapeDtypeStruct((m, SMALL_W), F32),
                   jax.ShapeDtypeStruct((SMALL_W, m), F32)),
        grid=(m // tm, BIG_W // tn),
        in_specs=[
            pl.BlockSpec((tm, D_MODEL), lambda i, j: (i, 0)),
            pl.BlockSpec((None, N_MOD, D_MODEL), lambda i, j: (i // tiles_per_seq, 0, 0)),
            pl.BlockSpec((1, D_MODEL), lambda i, j: (0, 0)),
            pl.BlockSpec((D_MODEL, tn), lambda i, j: (0, j)),
            pl.BlockSpec((D_MODEL, SMALL_W), lambda i, j: (0, 0)),
            pl.BlockSpec((SMALL_W, D_MODEL), lambda i, j: (0, 0)),
        ],
        out_specs=(pl.BlockSpec((tm, tn), lambda i, j: (i, j)),
                   pl.BlockSpec((tm, SMALL_W), lambda i, j: (i, 0)),
                   pl.BlockSpec((SMALL_W, tm), lambda i, j: (0, i))),
        scratch_shapes=[pltpu.VMEM((tm, D_MODEL), BF16)],
        compiler_params=_params("parallel", "arbitrary"),
        name="in_proj",
    )(x2, mod, g, w_big, w_small, w_small_t)


def _causal_conv(xbuf, cw_ref, rows):
    acc = cw_ref[CONV_K - 1:CONV_K, :] * xbuf[HALO:HALO + rows, :]
    for j in range(CONV_K - 1):
        off = HALO - (CONV_K - 1) + j
        acc = acc + cw_ref[j:j + 1, :] * xbuf[off:off + rows, :]
    return acc


def _gdn_kernel(q_ref, k_ref, v_ref, z_ref, sm_ref, smt_ref, cw_ref,
                alr_ref, alc_ref, dtr_ref, dtc_ref, nw_ref, o_ref, xbuf, s_scr):
    rows = TIME_TILE

    @pl.when(pl.program_id(1) == 0)
    def _():
        xbuf[0:HALO, :] = jnp.zeros((HALO, xbuf.shape[1]), F32)
        s_scr[...] = jnp.zeros(s_scr.shape, F32)

    xbuf[HALO:HALO + rows, 0:GDN_QK_W] = q_ref[...]
    xbuf[HALO:HALO + rows, GDN_QK_W:2 * GDN_QK_W] = k_ref[...]
    xbuf[HALO:HALO + rows, 2 * GDN_QK_W:2 * GDN_QK_W + GDN_V_W] = v_ref[...]
    qkv = _silu(_causal_conv(xbuf, cw_ref, rows))
    xbuf[0:HALO, :] = xbuf[rows:rows + HALO, :]

    sm = sm_ref[...]
    smt = smt_ref[...]
    beta_c = _sigmoid(sm)
    beta_r = _sigmoid(smt)
    g_c = -jnp.exp(alr_ref[...]) * _softplus(sm + dtr_ref[...])
    g_r = -jnp.exp(alc_ref[...]) * _softplus(smt + dtc_ref[...])

    ri = lax.broadcasted_iota(jnp.int32, (rows, rows), 0)
    ci = lax.broadcasted_iota(jnp.int32, (rows, rows), 1)
    same = (ri // GDN_CHUNK) == (ci // GDN_CHUNK)
    causal = same & (ci <= ri)
    strict = same & (ci < ri)
    gc_c = _dot(causal.astype(F32), g_c, precision=HIGHEST)
    gc_r = _dot(g_r, (same & (ri <= ci)).astype(F32), precision=HIGHEST)
    eye = (ri == ci).astype(F32)
    zeros_half = jnp.zeros((GDN_CHUNK, GDN_DV), F32)
    col_row = lax.broadcasted_iota(jnp.int32, (1, rows), 1)

    for h in range(GDN_HEADS):
        qh = qkv[:, h * GDN_DK:(h + 1) * GDN_DK]
        kh = qkv[:, GDN_QK_W + h * GDN_DK:GDN_QK_W + (h + 1) * GDN_DK]
        vh = qkv[:, 2 * GDN_QK_W + h * GDN_DV:2 * GDN_QK_W + (h + 1) * GDN_DV]
        qh = qh * lax.rsqrt(jnp.sum(qh * qh, axis=-1, keepdims=True) + EPS) * (GDN_DK ** -0.5)
        kh = kh * lax.rsqrt(jnp.sum(kh * kh, axis=-1, keepdims=True) + EPS)
        bc = beta_c[:, SMALL_BETA + h:SMALL_BETA + h + 1]
        gcc = gc_c[:, SMALL_ALPHA + h:SMALL_ALPHA + h + 1]
        gcr = gc_r[SMALL_ALPHA + h:SMALL_ALPHA + h + 1, :]
        e = jnp.exp(jnp.where(causal, gcc - gcr, 0.0))
        kb = kh * bc
        kt = kh.T
        qk_kk = _bdot(jnp.concatenate([qh, kb], axis=0), kt)
        a_qk = jnp.where(causal, qk_kk[:rows] * e, 0.0)
        a_kk = jnp.where(strict, qk_kk[rows:] * e, 0.0)
        p = -a_kk
        tinv = eye + p
        for _ in range(5):
            p = _bdot(p, p)
            tinv = tinv + _bdot(tinv, p)
        eg = jnp.exp(gcc)
        uw = _bdot(tinv, jnp.concatenate([vh * bc, kb * eg], axis=1))
        u = uw[:, :GDN_DV]
        w = uw[:, GDN_DV:]
        q_dec = qh * eg

        s = s_scr[h]
        outs = []
        for c in range(rows // GDN_CHUNK):
            lo = c * GDN_CHUNK
            hi = lo + GDN_CHUNK
            g_last = gcc[hi - 1:hi, :]
            r = _bdot(jnp.concatenate([w[lo:hi], q_dec[lo:hi]], axis=0), s)
            v_new = u[lo:hi] - r[:GDN_CHUNK]
            pads = [zeros_half] * (rows // GDN_CHUNK)
            pads[c] = v_new
            v_pad = jnp.concatenate(pads, axis=0)
            outs.append(r[GDN_CHUNK:] + _bdot(a_qk[lo:hi, :], v_pad))
            in_chunk = (col_row >= lo) & (col_row < hi)
            tail = jnp.where(in_chunk, jnp.exp(jnp.where(in_chunk, g_last - gcr, 0.0)), 0.0)
            s = s * jnp.exp(g_last) + _bdot(kt * tail, v_pad)
        s_scr[h] = s

        o = _rms(jnp.concatenate(outs, axis=0)) * nw_ref[...]
        zg = _silu(z_ref[:, h * GDN_DV:(h + 1) * GDN_DV])
        o_ref[:, h * GDN_DV:(h + 1) * GDN_DV] = (o * zg).astype(BF16)


def _gdn(big, small, small_t, conv_w, a_log, dt_bias, norm_w, bsz, seq):
    m = bsz * seq
    nt = seq // TIME_TILE
    qkv_w = 2 * GDN_QK_W + GDN_V_W

    def pad_row(v):
        return jnp.zeros((1, SMALL_W), F32).at[0, SMALL_ALPHA:SMALL_ALPHA + GDN_HEADS].set(v)

    alr = pad_row(a_log)
    dtr = pad_row(dt_bias)
    col = lambda cb: pl.BlockSpec((TIME_TILE, MIX_WIDTH), lambda b, t: (b * nt + t, cb))
    const = lambda shape: pl.BlockSpec(shape, lambda b, t: (0, 0))
    return pl.pallas_call(
        _gdn_kernel,
        out_shape=jax.ShapeDtypeStruct((m, MIX_WIDTH), BF16),
        grid=(bsz, nt),
        in_specs=[
            col(0), col(1), col(2), col(3),
            pl.BlockSpec((TIME_TILE, SMALL_W), lambda b, t: (b * nt + t, 0)),
            pl.BlockSpec((SMALL_W, TIME_TILE), lambda b, t: (0, b * nt + t)),
            const((CONV_K, qkv_w)),
            const((1, SMALL_W)), const((SMALL_W, 1)),
            const((1, SMALL_W)), const((SMALL_W, 1)),
            const((1, GDN_DV)),
        ],
        out_specs=pl.BlockSpec((TIME_TILE, MIX_WIDTH), lambda b, t: (b * nt + t, 0)),
        scratch_shapes=[pltpu.VMEM((HALO + TIME_TILE, qkv_w), F32),
                        pltpu.VMEM((GDN_HEADS, GDN_DK, GDN_DV), F32)],
        compiler_params=_params("parallel", "arbitrary"),
        name="gdn",
    )(big, big, big, big, small, small_t, conv_w, alr, alr.T, dtr, dtr.T,
      norm_w.reshape(1, GDN_DV))


def _lru_kernel(x_ref, y_ref, cw_ref, cb_ref, wa_ref, ba_ref, wi_ref, bi_ref, lam_ref,
                o_ref, xbuf, h_scr):
    rows = LRU_TIME_TILE
    first = pl.program_id(1) == 0

    @pl.when(first)
    def _():
        xbuf[0:HALO, :] = jnp.zeros((HALO, LRU_WIDTH), F32)
        h_scr[...] = jnp.zeros(h_scr.shape, F32)

    xbuf[HALO:HALO + rows, :] = x_ref[...]
    xl = _causal_conv(xbuf, cw_ref, rows) + cb_ref[...]
    xbuf[0:HALO, :] = xbuf[rows:rows + HALO, :]

    xlb = xl.astype(BF16)
    r_parts = []
    i_parts = []
    for k in range(LRU_BLOCKS):
        blk = xlb[:, k * LRU_BLOCK_W:(k + 1) * LRU_BLOCK_W]
        r_parts.append(_dot(blk, wa_ref[k]))
        i_parts.append(_dot(blk, wi_ref[k]))
    r = _sigmoid(jnp.concatenate(r_parts, axis=1) + ba_ref[...])
    ig = _sigmoid(jnp.concatenate(i_parts, axis=1) + bi_ref[...])
    log_a = -LRU_C * r * _softplus(-lam_ref[...])
    a = jnp.exp(log_a)
    mult = jnp.sqrt(-jnp.tanh(log_a) * (a * a + 1.0))
    row = lax.broadcasted_iota(jnp.int32, (rows, 1), 0)
    mult = jnp.where(first & (row == 0), 1.0, mult)
    b = mult * ig * xl

    shift = 1
    while shift < rows:
        keep = row >= shift
        a_prev = jnp.where(keep, pltpu.roll(a, shift, 0), 1.0)
        b_prev = jnp.where(keep, pltpu.roll(b, shift, 0), 0.0)
        b = a * b_prev + b
        a = a * a_prev
        shift *= 2
    hs = a * h_scr[0:1, :] + b
    h_scr[0:1, :] = hs[rows - 1:rows, :]
    o_ref[...] = (_gelu_tanh(y_ref[...]) * hs).astype(BF16)


def _lru(big, conv_w, conv_b, w_a, b_a, w_i, b_i, lam, bsz, seq):
    m = bsz * seq
    nt = seq // LRU_TIME_TILE
    base = BIG_LRU // LRU_WIDTH
    row1 = lambda v: v.reshape(1, LRU_WIDTH)
    const2 = lambda shape: pl.BlockSpec(shape, lambda b, t: (0, 0))
    const3 = lambda shape: pl.BlockSpec(shape, lambda b, t: (0, 0, 0))
    return pl.pallas_call(
        _lru_kernel,
        out_shape=jax.ShapeDtypeStruct((m, LRU_WIDTH), BF16),
        grid=(bsz, nt),
        in_specs=[
            pl.BlockSpec((LRU_TIME_TILE, LRU_WIDTH), lambda b, t: (b * nt + t, base)),
            pl.BlockSpec((LRU_TIME_TILE, LRU_WIDTH), lambda b, t: (b * nt + t, base + 1)),
            const2((CONV_K, LRU_WIDTH)), const2((1, LRU_WIDTH)),
            const3((LRU_BLOCKS, LRU_BLOCK_W, LRU_BLOCK_W)), const2((1, LRU_WIDTH)),
            const3((LRU_BLOCKS, LRU_BLOCK_W, LRU_BLOCK_W)), const2((1, LRU_WIDTH)),
            const2((1, LRU_WIDTH)),
        ],
        out_specs=pl.BlockSpec((LRU_TIME_TILE, LRU_WIDTH), lambda b, t: (b * nt + t, 0)),
        scratch_shapes=[pltpu.VMEM((HALO + LRU_TIME_TILE, LRU_WIDTH), F32),
                        pltpu.VMEM((SUBLANES, LRU_WIDTH), F32)],
        compiler_params=_params("parallel", "arbitrary"),
        name="rg_lru",
    )(big, big, conv_w, row1(conv_b), w_a.astype(BF16), row1(b_a), w_i.astype(BF16),
      row1(b_i), row1(lam))


def _gla_kernel(q_ref, k_ref, v_ref, z_ref, sm_ref, wg_ref, bg_ref, nw_ref, o_ref, st_scr):
    rows = TIME_TILE

    @pl.when(pl.program_id(1) == 0)
    def _():
        st_scr[...] = jnp.zeros(st_scr.shape, F32)

    log_a = _log_sigmoid(_dot(sm_ref[...].astype(BF16), wg_ref[...]) + bg_ref[...]) / GLA_GATE_NORM
    ri = lax.broadcasted_iota(jnp.int32, (rows, rows), 0)
    ci = lax.broadcasted_iota(jnp.int32, (rows, rows), 1)
    bcum = _dot((ci <= ri).astype(F32), log_a, precision=HIGHEST)
    row = lax.broadcasted_iota(jnp.int32, (rows, 1), 0)

    def boundary_rows(bh, width, offset, first_zero):
        out = jnp.zeros_like(bh) if first_zero else None
        for g in range(rows // width):
            src = g * width + offset
            if src < 0:
                continue
            val = jnp.broadcast_to(bh[src:src + 1, :], bh.shape)
            out = val if out is None else jnp.where(row // width == g, val, out)
        return out

    for h in range(GLA_HEADS):
        bh = bcum[:, h * GLA_DK:(h + 1) * GLA_DK]
        q = q_ref[:, h * GLA_DK:(h + 1) * GLA_DK] * (GLA_DK ** -0.5)
        k = k_ref[:, h * GLA_DK:(h + 1) * GLA_DK]
        v = v_ref[:, h * GLA_DV:(h + 1) * GLA_DV]
        b_loc = bh - boundary_rows(bh, GLA_BASE_CHUNK, -1, True)
        att = _bdot_nt(q * jnp.exp(b_loc), k * jnp.exp(-b_loc))
        att = jnp.where((ri // GLA_BASE_CHUNK == ci // GLA_BASE_CHUNK) & (ci <= ri), att, 0.0)
        s = GLA_BASE_CHUNK
        while s < rows:
            ref = boundary_rows(bh, 2 * s, s - 1, False)
            qs = q * jnp.exp(jnp.minimum(bh - ref, 0.0))
            ks = k * jnp.exp(jnp.minimum(ref - bh, 0.0))
            mask = ((ri // (2 * s) == ci // (2 * s)) & (ri % (2 * s) >= s) & (ci % (2 * s) < s))
            att = att + jnp.where(mask, _bdot_nt(qs, ks), 0.0)
            s *= 2
        st = st_scr[h]
        o = _bdot(att, v) + _bdot_nt(q * jnp.exp(bh), st)
        b_end = bh[rows - 1:rows, :]
        k_end = k * jnp.exp(b_end - bh)
        st_scr[h] = st * jnp.exp(b_end) + _bdot(v.T, k_end)
        o = _rms(o) * nw_ref[...]
        zg = _silu(z_ref[:, h * GLA_DV:(h + 1) * GLA_DV])
        o_ref[:, h * GLA_DV:(h + 1) * GLA_DV] = (o * zg).astype(BF16)


def _gla(big, small, w_gate, b_gate, norm_w, bsz, seq):
    m = bsz * seq
    nt = seq // TIME_TILE
    wg = jnp.zeros((SMALL_W, GLA_QK_W), F32).at[SMALL_GK:SMALL_GK + GLA_GATE_RANK].set(w_gate)
    qk_base = BIG_GLA // GLA_QK_W
    v_base = (BIG_GLA + 2 * GLA_QK_W) // GLA_V_W
    const = lambda shape: pl.BlockSpec(shape, lambda b, t: (0, 0))
    return pl.pallas_call(
        _gla_kernel,
        out_shape=jax.ShapeDtypeStruct((m, MIX_WIDTH), BF16),
        grid=(bsz, nt),
        in_specs=[
            pl.BlockSpec((TIME_TILE, GLA_QK_W), lambda b, t: (b * nt + t, qk_base)),
            pl.BlockSpec((TIME_TILE, GLA_QK_W), lambda b, t: (b * nt + t, qk_base + 1)),
            pl.BlockSpec((TIME_TILE, GLA_V_W), lambda b, t: (b * nt + t, v_base)),
            pl.BlockSpec((TIME_TILE, GLA_V_W), lambda b, t: (b * nt + t, v_base + 1)),
            pl.BlockSpec((TIME_TILE, SMALL_W), lambda b, t: (b * nt + t, 0)),
            const((SMALL_W, GLA_QK_W)), const((1, GLA_QK_W)), const((1, GLA_DV)),
        ],
        out_specs=pl.BlockSpec((TIME_TILE, MIX_WIDTH), lambda b, t: (b * nt + t, 0)),
        scratch_shapes=[pltpu.VMEM((GLA_HEADS, GLA_DV, GLA_DK), F32)],
        compiler_params=_params("parallel", "arbitrary"),
        name="gla",
    )(big, big, big, big, small, wg.astype(BF16), b_gate.reshape(1, GLA_QK_W),
      norm_w.reshape(1, GLA_DV))


def _merge_kernel(ga_ref, gb_ref, gc_ref, oa_ref, ob_ref, oc_ref, wb_ref, o_ref):
    acc = _sigmoid(ga_ref[...]) * _dot(oa_ref[...], wb_ref[0])
    acc = acc + _sigmoid(gb_ref[...]) * _dot(ob_ref[...], wb_ref[1])
    acc = acc + _sigmoid(gc_ref[...]) * _dot(oc_ref[...], wb_ref[2])
    o_ref[...] = acc.astype(BF16)


def _merge(big, oa, ob, oc, w_branch, seq):
    m = oa.shape[0]
    tm = min(1024, seq)
    tn = 512
    gate = lambda n: pl.BlockSpec(
        (tm, tn), lambda i, j: (i, (BIG_GATES + n * D_MODEL) // tn + j))
    branch = pl.BlockSpec((tm, MIX_WIDTH), lambda i, j: (i, 0))
    return pl.pallas_call(
        _merge_kernel,
        out_shape=jax.ShapeDtypeStruct((m, D_MODEL), BF16),
        grid=(m // tm, D_MODEL // tn),
        in_specs=[gate(0), gate(1), gate(2), branch, branch, branch,
                  pl.BlockSpec((N_BRANCH, MIX_WIDTH, tn), lambda i, j: (0, 0, j))],
        out_specs=pl.BlockSpec((tm, tn), lambda i, j: (i, j)),
        compiler_params=_params("parallel", "arbitrary"),
        name="merge",
    )(big, big, big, oa, ob, oc, w_branch)


def _out_proj_kernel(m_ref, x_ref, mod_ref, g_ref, w_ref, o_ref):
    y = _dot(m_ref[...], w_ref[...])
    o_ref[...] = x_ref[...] + mod_ref[MOD_GT1:MOD_GT1 + 1, :] * (_rms(y) * g_ref[...])


def _out_proj(merged, x2, mod, g, w_out, seq):
    m = x2.shape[0]
    tm = min(512, seq)
    tiles_per_seq = seq // tm
    return pl.pallas_call(
        _out_proj_kernel,
        out_shape=jax.ShapeDtypeStruct((m, D_MODEL), F32),
        grid=(m // tm,),
        in_specs=[
            pl.BlockSpec((tm, D_MODEL), lambda i: (i, 0)),
            pl.BlockSpec((tm, D_MODEL), lambda i: (i, 0)),
            pl.BlockSpec((None, N_MOD, D_MODEL), lambda i: (i // tiles_per_seq, 0, 0)),
            pl.BlockSpec((1, D_MODEL), lambda i: (0, 0)),
            pl.BlockSpec((D_MODEL, D_MODEL), lambda i: (0, 0)),
        ],
        out_specs=pl.BlockSpec((tm, D_MODEL), lambda i: (i, 0)),
        compiler_params=_params("parallel"),
        name="out_proj",
    )(merged, x2, mod, g, w_out)


def _mlp_kernel(x_ref, mod_ref, gpre_ref, gpost_ref, w1_ref, w2_ref, o_ref, h_scr, acc_scr):
    f = pl.program_id(1)

    @pl.when(f == 0)
    def _():
        h = _rms(x_ref[...]) * gpre_ref[...]
        h = h * (1.0 + mod_ref[MOD_SC2:MOD_SC2 + 1, :]) + mod_ref[MOD_SH2:MOD_SH2 + 1, :]
        h_scr[...] = h.astype(BF16)
        acc_scr[...] = jnp.zeros(acc_scr.shape, F32)

    a = jnp.maximum(_dot(h_scr[...], w1_ref[...]), 0.0)
    acc_scr[...] += _dot((a * a).astype(BF16), w2_ref[...])

    @pl.when(f == pl.num_programs(1) - 1)
    def _():
        y = _rms(acc_scr[...]) * gpost_ref[...]
        o_ref[...] = x_ref[...] + mod_ref[MOD_GT2:MOD_GT2 + 1, :] * y


def _mlp(x2, mod, g_pre, g_post, w1, w2, seq):
    m = x2.shape[0]
    tm = min(512, seq)
    tf = 1024
    tiles_per_seq = seq // tm
    return pl.pallas_call(
        _mlp_kernel,
        out_shape=jax.ShapeDtypeStruct((m, D_MODEL), F32),
        grid=(m // tm, D_FF // tf),
        in_specs=[
            pl.BlockSpec((tm, D_MODEL), lambda i, f: (i, 0)),
            pl.BlockSpec((None, N_MOD, D_MODEL), lambda i, f: (i // tiles_per_seq, 0, 0)),
            pl.BlockSpec((1, D_MODEL), lambda i, f: (0, 0)),
            pl.BlockSpec((1, D_MODEL), lambda i, f: (0, 0)),
            pl.BlockSpec((D_MODEL, tf), lambda i, f: (0, f)),
            pl.BlockSpec((tf, D_MODEL), lambda i, f: (f, 0)),
        ],
        out_specs=pl.BlockSpec((tm, D_MODEL), lambda i, f: (i, 0)),
        scratch_shapes=[pltpu.VMEM((tm, D_MODEL), BF16), pltpu.VMEM((tm, D_MODEL), F32)],
        compiler_params=_params("parallel", "arbitrary"),
        name="mlp",
    )(x2, mod, g_pre, g_post, w1, w2)


def _split_w_in(w_in_l):
    offs = [0]
    for s in IN_SPLITS:
        offs.append(offs[-1] + s)
    part = lambda i: w_in_l[:, offs[i]:offs[i + 1]]
    (qa, ka, va, za, beta_a, alpha_a, xb, yb, qc, kc, vc, zc, gk, gates) = (
        part(i) for i in range(len(IN_SPLITS)))
    w_big = jnp.concatenate([qa, ka, va, za, xb, yb, qc, kc, vc, zc, gates], axis=1)
    pad = jnp.zeros((D_MODEL, SMALL_W - 2 * GDN_HEADS - GLA_GATE_RANK), w_in_l.dtype)
    w_small = jnp.concatenate([beta_a, alpha_a, gk, pad], axis=1)
    return w_big.astype(BF16), w_small.astype(BF16), w_small.T.astype(BF16)


def kernel(x, c, w_ada, b_ada, g_pre_mix, g_post_mix, g_pre_mlp, g_post_mlp, w_in, conv_gdn, gdn_a_log, gdn_dt_bias, gdn_norm, conv_lru, conv_lru_b, lru_w_a, lru_b_a, lru_w_i, lru_b_i, lru_lambda, gla_w_gate, gla_b_gate, gla_norm, w_branch, w_out, w_mlp1, w_mlp2):
    bsz, seq, _ = x.shape
    n_layers = w_in.shape[0]
    assert seq % LRU_TIME_TILE == 0 and x.shape[2] == D_MODEL
    row = lambda v: v.reshape(1, D_MODEL)

    mod_all = _ada_mod(c, w_ada, b_ada).reshape(n_layers, bsz, N_MOD, D_MODEL)
    x2 = x.reshape(bsz * seq, D_MODEL)
    for l in range(n_layers):
        mod = mod_all[l]
        w_big, w_small, w_small_t = _split_w_in(w_in[l])
        big, small, small_t = _in_proj(x2, mod, row(g_pre_mix[l]), w_big, w_small, w_small_t, seq)
        oa = _gdn(big, small, small_t, conv_gdn[l], gdn_a_log[l], gdn_dt_bias[l], gdn_norm[l],
                  bsz, seq)
        ob = _lru(big, conv_lru[l], conv_lru_b[l], lru_w_a[l], lru_b_a[l], lru_w_i[l],
                  lru_b_i[l], lru_lambda[l], bsz, seq)
        oc = _gla(big, small, gla_w_gate[l], gla_b_gate[l], gla_norm[l], bsz, seq)
        merged = _merge(big, oa, ob, oc, w_branch[l].astype(BF16), seq)
        x2 = _out_proj(merged, x2, mod, row(g_post_mix[l]), w_out[l].astype(BF16), seq)
        x2 = _mlp(x2, mod, row(g_pre_mlp[l]), row(g_post_mlp[l]),
                  w_mlp1[l].astype(BF16), w_mlp2[l].astype(BF16), seq)
    return x2.reshape(bsz, seq, D_MODEL)
```

```python
import functools

import jax
import jax.numpy as jnp
from jax import lax
from jax.experimental import pallas as pl
from jax.experimental.pallas import tpu as pltpu

F32 = jnp.float32
BF16 = jnp.bfloat16
HIGHEST = lax.Precision.HIGHEST

D_MODEL = 2048
MIX_WIDTH = D_MODEL // 2
N_BRANCH = 3
GDN_DK = 128
GDN_DV = 128
GDN_HEADS = MIX_WIDTH // GDN_DV
GDN_CHUNK = 64
LRU_WIDTH = MIX_WIDTH
LRU_BLOCKS = 8
LRU_BLOCK_W = LRU_WIDTH // LRU_BLOCKS
LRU_C = 8.0
GLA_HEADS = 4
GLA_DV = MIX_WIDTH // GLA_HEADS
GLA_DK = GLA_DV // 2
GLA_GATE_RANK = 16
GLA_GATE_NORM = 16.0
GLA_BASE_CHUNK = 16
CONV_K = 4
D_FF = 4 * D_MODEL
N_MOD = 6
EPS = 1e-6

GDN_QK_W = GDN_HEADS * GDN_DK
GDN_V_W = GDN_HEADS * GDN_DV
GLA_QK_W = GLA_HEADS * GLA_DK
GLA_V_W = GLA_HEADS * GLA_DV
IN_SPLITS = (GDN_QK_W, GDN_QK_W, GDN_V_W, GDN_V_W, GDN_HEADS, GDN_HEADS,
             LRU_WIDTH, LRU_WIDTH,
             GLA_QK_W, GLA_QK_W, GLA_V_W, GLA_V_W, GLA_GATE_RANK,
             N_BRANCH * D_MODEL)

LANES = 128
SUBLANES = 8
VMEM_LIMIT_BYTES = 56 * 1024 * 1024

BIG_GDN = 0
BIG_LRU = BIG_GDN + 4 * MIX_WIDTH
BIG_GLA = BIG_LRU + 2 * LRU_WIDTH
BIG_GATES = BIG_GLA + 2 * GLA_QK_W + 2 * GLA_V_W
BIG_W = BIG_GATES + N_BRANCH * D_MODEL
SMALL_BETA = 0
SMALL_ALPHA = GDN_HEADS
SMALL_GK = 2 * GDN_HEADS
SMALL_W = LANES

MOD_SH1, MOD_SC1, MOD_GT1, MOD_SH2, MOD_SC2, MOD_GT2 = range(N_MOD)

TIME_TILE = 128
LRU_TIME_TILE = 256
HALO = SUBLANES


def _sigmoid(x):
    return 1.0 / (1.0 + jnp.exp(-x))


def _silu(x):
    return x * _sigmoid(x)


def _softplus(x):
    return jnp.maximum(x, 0.0) + jnp.log(1.0 + jnp.exp(-jnp.abs(x)))


def _log_sigmoid(x):
    return -_softplus(-x)


def _gelu_tanh(x):
    c = 0.7978845608028654
    return 0.5 * x * (1.0 + jnp.tanh(c * (x + 0.044715 * (x * x * x))))


def _rms(x):
    return x * lax.rsqrt(jnp.mean(x * x, axis=-1, keepdims=True) + EPS)


def _dot(a, b, precision=None):
    return jnp.dot(a, b, preferred_element_type=F32, precision=precision)


def _dot_nt(a, b):
    return lax.dot_general(a, b, (((1,), (1,)), ((), ())), preferred_element_type=F32)


def _bdot(a, b):
    return _dot(a.astype(BF16), b.astype(BF16))


def _bdot_nt(a, b):
    return _dot_nt(a.astype(BF16), b.astype(BF16))


def _params(*sem):
    return pltpu.CompilerParams(dimension_semantics=sem, vmem_limit_bytes=VMEM_LIMIT_BYTES)


def _ada_kernel(c_ref, w_ref, b_ref, o_ref):
    sc = _silu(c_ref[...]).astype(BF16)
    o_ref[...] = _dot(sc, w_ref[...].astype(BF16)) + b_ref[...]


def _ada_mod(c, w_ada, b_ada):
    n_layers, _, n_out = w_ada.shape
    bsz = c.shape[0]
    tn = 1024
    return pl.pallas_call(
        _ada_kernel,
        out_shape=jax.ShapeDtypeStruct((n_layers, bsz, n_out), F32),
        grid=(n_layers, n_out // tn),
        in_specs=[
            pl.BlockSpec((bsz, D_MODEL), lambda l, j: (0, 0)),
            pl.BlockSpec((None, D_MODEL, tn), lambda l, j: (l, 0, j)),
            pl.BlockSpec((None, 1, tn), lambda l, j: (l, 0, j)),
        ],
        out_specs=pl.BlockSpec((None, bsz, tn), lambda l, j: (l, 0, j)),
        compiler_params=_params("parallel", "parallel"),
        name="ada_mod",
    )(c, w_ada, b_ada.reshape(n_layers, 1, n_out))


def _in_proj_kernel(x_ref, mod_ref, g_ref, w_ref, ws_ref, wst_ref,
                    big_ref, small_ref, smallt_ref, h_scr):
    @pl.when(pl.program_id(1) == 0)
    def _():
        h = _rms(x_ref[...]) * g_ref[...]
        h = h * (1.0 + mod_ref[MOD_SC1:MOD_SC1 + 1, :]) + mod_ref[MOD_SH1:MOD_SH1 + 1, :]
        hb = h.astype(BF16)
        h_scr[...] = hb
        small_ref[...] = _dot(hb, ws_ref[...])
        smallt_ref[...] = _dot_nt(wst_ref[...], hb)

    big_ref[...] = _dot(h_scr[...], w_ref[...])


def _in_proj(x2, mod, g, w_big, w_small, w_small_t, layer, seq):
    m = x2.shape[0]
    tm = min(1024, seq)
    tn = 1024
    tiles_per_seq = seq // tm
    return pl.pallas_call(
        _in_proj_kernel,
        out_shape=(jax.ShapeDtypeStruct((m, BIG_W), F32),
                   jax.ShapeDtypeStruct((m, SMALL_W), F32),
                   jax.ShapeDtypeStruct((SMALL_W, m), F32)),
        grid=(m // tm, BIG_W // tn),
        in_specs=[
            pl.BlockSpec((tm, D_MODEL), lambda i, j: (i, 0)),
            pl.BlockSpec((None, N_MOD, D_MODEL), lambda i, j: (i // tiles_per_seq, 0, 0)),
            pl.BlockSpec((1, D_MODEL), lambda i, j: (0, 0)),
            pl.BlockSpec((None, D_MODEL, tn), lambda i, j: (layer, 0, j)),
            pl.BlockSpec((None, D_MODEL, SMALL_W), lambda i, j: (layer, 0, 0)),
            pl.BlockSpec((None, SMALL_W, D_MODEL), lambda i, j: (layer, 0, 0)),
        ],
        out_specs=(pl.BlockSpec((tm, tn), lambda i, j: (i, j)),
                   pl.BlockSpec((tm, SMALL_W), lambda i, j: (i, 0)),
                   pl.BlockSpec((SMALL_W, tm), lambda i, j: (0, i))),
        scratch_shapes=[pltpu.VMEM((tm, D_MODEL), BF16)],
        compiler_params=_params("parallel", "arbitrary"),
        name="in_proj",
    )(x2, mod, g, w_big, w_small, w_small_t)


def _causal_conv(xbuf, cw_ref, rows):
    acc = cw_ref[CONV_K - 1:CONV_K, :] * xbuf[HALO:HALO + rows, :]
    for j in range(CONV_K - 1):
        off = HALO - (CONV_K - 1) + j
        acc = acc + cw_ref[j:j + 1, :] * xbuf[off:off + rows, :]
    return acc


def _gdn_kernel(q_ref, k_ref, v_ref, z_ref, sm_ref, smt_ref, cw_ref,
                alr_ref, alc_ref, dtr_ref, dtc_ref, nw_ref, o_ref, xbuf, s_scr):
    rows = TIME_TILE

    @pl.when(pl.program_id(1) == 0)
    def _():
        xbuf[0:HALO, :] = jnp.zeros((HALO, xbuf.shape[1]), F32)
        s_scr[...] = jnp.zeros(s_scr.shape, F32)

    xbuf[HALO:HALO + rows, 0:GDN_QK_W] = q_ref[...]
    xbuf[HALO:HALO + rows, GDN_QK_W:2 * GDN_QK_W] = k_ref[...]
    xbuf[HALO:HALO + rows, 2 * GDN_QK_W:2 * GDN_QK_W + GDN_V_W] = v_ref[...]
    qkv = _silu(_causal_conv(xbuf, cw_ref, rows))
    xbuf[0:HALO, :] = xbuf[rows:rows + HALO, :]

    sm = sm_ref[...]
    smt = smt_ref[...]
    beta_c = _sigmoid(sm)
    beta_r = _sigmoid(smt)
    g_c = -jnp.exp(alr_ref[...]) * _softplus(sm + dtr_ref[...])
    g_r = -jnp.exp(alc_ref[...]) * _softplus(smt + dtc_ref[...])

    ri = lax.broadcasted_iota(jnp.int32, (rows, rows), 0)
    ci = lax.broadcasted_iota(jnp.int32, (rows, rows), 1)
    same = (ri // GDN_CHUNK) == (ci // GDN_CHUNK)
    causal = same & (ci <= ri)
    strict = same & (ci < ri)
    gc_c = _dot(causal.astype(F32), g_c, precision=HIGHEST)
    gc_r = _dot(g_r, (same & (ri <= ci)).astype(F32), precision=HIGHEST)
    eye = (ri == ci).astype(F32)
    level_masks = []
    s = 1
    while s < GDN_CHUNK:
        level_masks.append((ri // (2 * s) == ci // (2 * s)) & (ri % (2 * s) >= s) & (ci % (2 * s) < s))
        s *= 2
    zeros_half = jnp.zeros((GDN_CHUNK, GDN_DV), F32)
    col_row = lax.broadcasted_iota(jnp.int32, (1, rows), 1)

    heads = range(GDN_HEADS)
    n_chunks = rows // GDN_CHUNK
    qs, kts, gccs, gcrs, a_qks, a_kks, rhss = [], [], [], [], [], [], []
    for h in heads:
        qh = qkv[:, h * GDN_DK:(h + 1) * GDN_DK]
        kh = qkv[:, GDN_QK_W + h * GDN_DK:GDN_QK_W + (h + 1) * GDN_DK]
        vh = qkv[:, 2 * GDN_QK_W + h * GDN_DV:2 * GDN_QK_W + (h + 1) * GDN_DV]
        qh = qh * lax.rsqrt(jnp.sum(qh * qh, axis=-1, keepdims=True) + EPS) * (GDN_DK ** -0.5)
        kh = kh * lax.rsqrt(jnp.sum(kh * kh, axis=-1, keepdims=True) + EPS)
        bc = beta_c[:, SMALL_BETA + h:SMALL_BETA + h + 1]
        gcc = gc_c[:, SMALL_ALPHA + h:SMALL_ALPHA + h + 1]
        gcr = gc_r[SMALL_ALPHA + h:SMALL_ALPHA + h + 1, :]
        e = jnp.exp(jnp.where(causal, gcc - gcr, 0.0))
        kb = kh * bc
        kt = kh.T
        qk_kk = _bdot(jnp.concatenate([qh, kb], axis=0), kt)
        eg = jnp.exp(gcc)
        qs.append(qh * eg)
        kts.append(kt)
        gccs.append(gcc)
        gcrs.append(gcr)
        a_qks.append(jnp.where(causal, qk_kk[:rows] * e, 0.0))
        a_kks.append(jnp.where(strict, qk_kk[rows:] * e, 0.0))
        rhss.append(jnp.concatenate([vh * bc, kb * eg], axis=1))

    tinv = [eye - jnp.where(level_masks[0], a_kks[h], 0.0) for h in heads]
    for lm in level_masks[1:]:
        et = [_bdot(jnp.where(lm, a_kks[h], 0.0), tinv[h]) for h in heads]
        tinv = [tinv[h] - _bdot(tinv[h], et[h]) for h in heads]
    uws = [_bdot(tinv[h], rhss[h]) for h in heads]

    states = [s_scr[h] for h in heads]
    outs = [[] for _ in heads]
    for c in range(n_chunks):
        lo = c * GDN_CHUNK
        hi = lo + GDN_CHUNK
        in_chunk = (col_row >= lo) & (col_row < hi)
        rs = [_bdot(jnp.concatenate([uws[h][lo:hi, GDN_DV:], qs[h][lo:hi]], axis=0), states[h])
              for h in heads]
        v_pads = []
        for h in heads:
            pads = [zeros_half] * n_chunks
            pads[c] = uws[h][lo:hi, :GDN_DV] - rs[h][:GDN_CHUNK]
            v_pads.append(jnp.concatenate(pads, axis=0))
        new_states = []
        for h in heads:
            g_last = gccs[h][hi - 1:hi, :]
            tail = jnp.where(in_chunk, jnp.exp(jnp.where(in_chunk, g_last - gcrs[h], 0.0)), 0.0)
            new_states.append(states[h] * jnp.exp(g_last) + _bdot(kts[h] * tail, v_pads[h]))
        states = new_states
        for h in heads:
            outs[h].append(rs[h][GDN_CHUNK:] + _bdot(a_qks[h][lo:hi, :], v_pads[h]))
    for h in heads:
        s_scr[h] = states[h]
        o = _rms(jnp.concatenate(outs[h], axis=0)) * nw_ref[...]
        zg = _silu(z_ref[:, h * GDN_DV:(h + 1) * GDN_DV])
        o_ref[:, h * GDN_DV:(h + 1) * GDN_DV] = (o * zg).astype(BF16)


def _gdn(big, small, small_t, conv_w, a_log, dt_bias, norm_w, bsz, seq):
    m = bsz * seq
    nt = seq // TIME_TILE
    qkv_w = 2 * GDN_QK_W + GDN_V_W

    def pad_row(v):
        return jnp.zeros((1, SMALL_W), F32).at[0, SMALL_ALPHA:SMALL_ALPHA + GDN_HEADS].set(v)

    alr = pad_row(a_log)
    dtr = pad_row(dt_bias)
    col = lambda cb: pl.BlockSpec((TIME_TILE, MIX_WIDTH), lambda b, t: (b * nt + t, cb))
    const = lambda shape: pl.BlockSpec(shape, lambda b, t: (0, 0))
    return pl.pallas_call(
        _gdn_kernel,
        out_shape=jax.ShapeDtypeStruct((m, MIX_WIDTH), BF16),
        grid=(bsz, nt),
        in_specs=[
            col(0), col(1), col(2), col(3),
            pl.BlockSpec((TIME_TILE, SMALL_W), lambda b, t: (b * nt + t, 0)),
            pl.BlockSpec((SMALL_W, TIME_TILE), lambda b, t: (0, b * nt + t)),
            const((CONV_K, qkv_w)),
            const((1, SMALL_W)), const((SMALL_W, 1)),
            const((1, SMALL_W)), const((SMALL_W, 1)),
            const((1, GDN_DV)),
        ],
        out_specs=pl.BlockSpec((TIME_TILE, MIX_WIDTH), lambda b, t: (b * nt + t, 0)),
        scratch_shapes=[pltpu.VMEM((HALO + TIME_TILE, qkv_w), F32),
                        pltpu.VMEM((GDN_HEADS, GDN_DK, GDN_DV), F32)],
        compiler_params=_params("parallel", "arbitrary"),
        name="gdn",
    )(big, big, big, big, small, small_t, conv_w, alr, alr.T, dtr, dtr.T,
      norm_w.reshape(1, GDN_DV))


def _lru_kernel(x_ref, y_ref, cw_ref, cb_ref, wa_ref, ba_ref, wi_ref, bi_ref, lam_ref,
                o_ref, xbuf, h_scr):
    rows = LRU_TIME_TILE
    first = pl.program_id(1) == 0

    @pl.when(first)
    def _():
        xbuf[0:HALO, :] = jnp.zeros((HALO, LRU_WIDTH), F32)
        h_scr[...] = jnp.zeros(h_scr.shape, F32)

    xbuf[HALO:HALO + rows, :] = x_ref[...]
    xl = _causal_conv(xbuf, cw_ref, rows) + cb_ref[...]
    xbuf[0:HALO, :] = xbuf[rows:rows + HALO, :]

    xlb = xl.astype(BF16)
    r_parts = []
    i_parts = []
    for k in range(LRU_BLOCKS):
        blk = xlb[:, k * LRU_BLOCK_W:(k + 1) * LRU_BLOCK_W]
        r_parts.append(_dot(blk, wa_ref[k]))
        i_parts.append(_dot(blk, wi_ref[k]))
    r = _sigmoid(jnp.concatenate(r_parts, axis=1) + ba_ref[...])
    ig = _sigmoid(jnp.concatenate(i_parts, axis=1) + bi_ref[...])
    log_a = -LRU_C * r * _softplus(-lam_ref[...])
    a = jnp.exp(log_a)
    mult = jnp.sqrt(-jnp.tanh(log_a) * (a * a + 1.0))
    row = lax.broadcasted_iota(jnp.int32, (rows, 1), 0)
    mult = jnp.where(first & (row == 0), 1.0, mult)
    b = mult * ig * xl

    row8 = row[:SUBLANES]
    shift = 1
    while shift < SUBLANES:
        a_roll = pltpu.roll(a, shift, 0)
        b_roll = pltpu.roll(b, shift, 0)
        keep = row8 >= shift
        a_prev = jnp.concatenate([jnp.where(keep, a_roll[:SUBLANES], 1.0), a_roll[SUBLANES:]], axis=0)
        b_prev = jnp.concatenate([jnp.where(keep, b_roll[:SUBLANES], 0.0), b_roll[SUBLANES:]], axis=0)
        b = a * b_prev + b
        a = a * a_prev
        shift *= 2
    while shift < rows:
        b = jnp.concatenate([b[:shift], a[shift:] * b[:rows - shift] + b[shift:]], axis=0)
        a = jnp.concatenate([a[:shift], a[shift:] * a[:rows - shift]], axis=0)
        shift *= 2
    hs = a * h_scr[0:1, :] + b
    h_scr[0:1, :] = hs[rows - 1:rows, :]
    o_ref[...] = (_gelu_tanh(y_ref[...]) * hs).astype(BF16)


def _lru(big, conv_w, conv_b, w_a, b_a, w_i, b_i, lam, bsz, seq):
    m = bsz * seq
    nt = seq // LRU_TIME_TILE
    base = BIG_LRU // LRU_WIDTH
    row1 = lambda v: v.reshape(1, LRU_WIDTH)
    const2 = lambda shape: pl.BlockSpec(shape, lambda b, t: (0, 0))
    const3 = lambda shape: pl.BlockSpec(shape, lambda b, t: (0, 0, 0))
    return pl.pallas_call(
        _lru_kernel,
        out_shape=jax.ShapeDtypeStruct((m, LRU_WIDTH), BF16),
        grid=(bsz, nt),
        in_specs=[
            pl.BlockSpec((LRU_TIME_TILE, LRU_WIDTH), lambda b, t: (b * nt + t, base)),
            pl.BlockSpec((LRU_TIME_TILE, LRU_WIDTH), lambda b, t: (b * nt + t, base + 1)),
            const2((CONV_K, LRU_WIDTH)), const2((1, LRU_WIDTH)),
            const3((LRU_BLOCKS, LRU_BLOCK_W, LRU_BLOCK_W)), const2((1, LRU_WIDTH)),
            const3((LRU_BLOCKS, LRU_BLOCK_W, LRU_BLOCK_W)), const2((1, LRU_WIDTH)),
            const2((1, LRU_WIDTH)),
        ],
        out_specs=pl.BlockSpec((LRU_TIME_TILE, LRU_WIDTH), lambda b, t: (b * nt + t, 0)),
        scratch_shapes=[pltpu.VMEM((HALO + LRU_TIME_TILE, LRU_WIDTH), F32),
                        pltpu.VMEM((SUBLANES, LRU_WIDTH), F32)],
        compiler_params=_params("parallel", "arbitrary"),
        name="rg_lru",
    )(big, big, conv_w, row1(conv_b), w_a.astype(BF16), row1(b_a), w_i.astype(BF16),
      row1(b_i), row1(lam))


def _gla_kernel(q_ref, k_ref, v_ref, z_ref, sm_ref, wg_ref, bg_ref, nw_ref, o_ref, st_scr):
    rows = TIME_TILE

    @pl.when(pl.program_id(1) == 0)
    def _():
        st_scr[...] = jnp.zeros(st_scr.shape, F32)

    log_a = _log_sigmoid(_dot(sm_ref[...].astype(BF16), wg_ref[...]) + bg_ref[...]) / GLA_GATE_NORM
    ri = lax.broadcasted_iota(jnp.int32, (rows, rows), 0)
    ci = lax.broadcasted_iota(jnp.int32, (rows, rows), 1)
    bcum = _dot((ci <= ri).astype(F32), log_a, precision=HIGHEST)
    row = lax.broadcasted_iota(jnp.int32, (rows, 1), 0)

    def boundary_rows(bh, width, offset, first_zero):
        out = jnp.zeros_like(bh) if first_zero else None
        for g in range(rows // width):
            src = g * width + offset
            if src < 0:
                continue
            val = jnp.broadcast_to(bh[src:src + 1, :], bh.shape)
            out = val if out is None else jnp.where(row // width == g, val, out)
        return out

    for h in range(GLA_HEADS):
        bh = bcum[:, h * GLA_DK:(h + 1) * GLA_DK]
        q = q_ref[:, h * GLA_DK:(h + 1) * GLA_DK] * (GLA_DK ** -0.5)
        k = k_ref[:, h * GLA_DK:(h + 1) * GLA_DK]
        v = v_ref[:, h * GLA_DV:(h + 1) * GLA_DV]
        b_loc = bh - boundary_rows(bh, GLA_BASE_CHUNK, -1, True)
        att = _bdot_nt(q * jnp.exp(b_loc), k * jnp.exp(-b_loc))
        att = jnp.where((ri // GLA_BASE_CHUNK == ci // GLA_BASE_CHUNK) & (ci <= ri), att, 0.0)
        s = GLA_BASE_CHUNK
        while s < rows:
            ref = boundary_rows(bh, 2 * s, s - 1, False)
            qs = q * jnp.exp(jnp.minimum(bh - ref, 0.0))
            ks = k * jnp.exp(jnp.minimum(ref - bh, 0.0))
            mask = ((ri // (2 * s) == ci // (2 * s)) & (ri % (2 * s) >= s) & (ci % (2 * s) < s))
            att = att + jnp.where(mask, _bdot_nt(qs, ks), 0.0)
            s *= 2
        st = st_scr[h]
        o = _bdot(att, v) + _bdot_nt(q * jnp.exp(bh), st)
        b_end = bh[rows - 1:rows, :]
        k_end = k * jnp.exp(b_end - bh)
        st_scr[h] = st * jnp.exp(b_end) + _bdot(v.T, k_end)
        o = _rms(o) * nw_ref[...]
        zg = _silu(z_ref[:, h * GLA_DV:(h + 1) * GLA_DV])
        o_ref[:, h * GLA_DV:(h + 1) * GLA_DV] = (o * zg).astype(BF16)


def _gla(big, small, w_gate, b_gate, norm_w, bsz, seq):
    m = bsz * seq
    nt = seq // TIME_TILE
    wg = jnp.zeros((SMALL_W, GLA_QK_W), F32).at[SMALL_GK:SMALL_GK + GLA_GATE_RANK].set(w_gate)
    qk_base = BIG_GLA // GLA_QK_W
    v_base = (BIG_GLA + 2 * GLA_QK_W) // GLA_V_W
    const = lambda shape: pl.BlockSpec(shape, lambda b, t: (0, 0))
    return pl.pallas_call(
        _gla_kernel,
        out_shape=jax.ShapeDtypeStruct((m, MIX_WIDTH), BF16),
        grid=(bsz, nt),
        in_specs=[
            pl.BlockSpec((TIME_TILE, GLA_QK_W), lambda b, t: (b * nt + t, qk_base)),
            pl.BlockSpec((TIME_TILE, GLA_QK_W), lambda b, t: (b * nt + t, qk_base + 1)),
            pl.BlockSpec((TIME_TILE, GLA_V_W), lambda b, t: (b * nt + t, v_base)),
            pl.BlockSpec((TIME_TILE, GLA_V_W), lambda b, t: (b * nt + t, v_base + 1)),
            pl.BlockSpec((TIME_TILE, SMALL_W), lambda b, t: (b * nt + t, 0)),
            const((SMALL_W, GLA_QK_W)), const((1, GLA_QK_W)), const((1, GLA_DV)),
        ],
        out_specs=pl.BlockSpec((TIME_TILE, MIX_WIDTH), lambda b, t: (b * nt + t, 0)),
        scratch_shapes=[pltpu.VMEM((GLA_HEADS, GLA_DV, GLA_DK), F32)],
        compiler_params=_params("parallel", "arbitrary"),
        name="gla",
    )(big, big, big, big, small, wg.astype(BF16), b_gate.reshape(1, GLA_QK_W),
      norm_w.reshape(1, GLA_DV))


def _merge_kernel(ga_ref, gb_ref, gc_ref, oa_ref, ob_ref, oc_ref, wb_ref, o_ref):
    acc = _sigmoid(ga_ref[...]) * _dot(oa_ref[...], wb_ref[0])
    acc = acc + _sigmoid(gb_ref[...]) * _dot(ob_ref[...], wb_ref[1])
    acc = acc + _sigmoid(gc_ref[...]) * _dot(oc_ref[...], wb_ref[2])
    o_ref[...] = acc.astype(BF16)


def _merge(big, oa, ob, oc, w_branch, layer, seq):
    m = oa.shape[0]
    tm = min(1024, seq)
    tn = 512
    gate = lambda n: pl.BlockSpec(
        (tm, tn), lambda i, j: (i, (BIG_GATES + n * D_MODEL) // tn + j))
    branch = pl.BlockSpec((tm, MIX_WIDTH), lambda i, j: (i, 0))
    return pl.pallas_call(
        _merge_kernel,
        out_shape=jax.ShapeDtypeStruct((m, D_MODEL), BF16),
        grid=(m // tm, D_MODEL // tn),
        in_specs=[gate(0), gate(1), gate(2), branch, branch, branch,
                  pl.BlockSpec((None, N_BRANCH, MIX_WIDTH, tn), lambda i, j: (layer, 0, 0, j))],
        out_specs=pl.BlockSpec((tm, tn), lambda i, j: (i, j)),
        compiler_params=_params("parallel", "arbitrary"),
        name="merge",
    )(big, big, big, oa, ob, oc, w_branch)


def _out_proj_kernel(m_ref, x_ref, mod_ref, g_ref, w_ref, o_ref):
    y = _dot(m_ref[...], w_ref[...])
    o_ref[...] = x_ref[...] + mod_ref[MOD_GT1:MOD_GT1 + 1, :] * (_rms(y) * g_ref[...])


def _out_proj(merged, x2, mod, g, w_out, layer, seq):
    m = x2.shape[0]
    tm = min(512, seq)
    tiles_per_seq = seq // tm
    return pl.pallas_call(
        _out_proj_kernel,
        out_shape=jax.ShapeDtypeStruct((m, D_MODEL), F32),
        grid=(m // tm,),
        in_specs=[
            pl.BlockSpec((tm, D_MODEL), lambda i: (i, 0)),
            pl.BlockSpec((tm, D_MODEL), lambda i: (i, 0)),
            pl.BlockSpec((None, N_MOD, D_MODEL), lambda i: (i // tiles_per_seq, 0, 0)),
            pl.BlockSpec((1, D_MODEL), lambda i: (0, 0)),
            pl.BlockSpec((None, D_MODEL, D_MODEL), lambda i: (layer, 0, 0)),
        ],
        out_specs=pl.BlockSpec((tm, D_MODEL), lambda i: (i, 0)),
        compiler_params=_params("parallel"),
        name="out_proj",
    )(merged, x2, mod, g, w_out)


def _mlp_kernel(x_ref, mod_ref, gpre_ref, gpost_ref, w1_ref, w2_ref, o_ref, h_scr, acc_scr):
    f = pl.program_id(1)

    @pl.when(f == 0)
    def _():
        h = _rms(x_ref[...]) * gpre_ref[...]
        h = h * (1.0 + mod_ref[MOD_SC2:MOD_SC2 + 1, :]) + mod_ref[MOD_SH2:MOD_SH2 + 1, :]
        h_scr[...] = h.astype(BF16)
        acc_scr[...] = jnp.zeros(acc_scr.shape, F32)

    a = jnp.maximum(_dot(h_scr[...], w1_ref[...]), 0.0)
    acc_scr[...] += _dot((a * a).astype(BF16), w2_ref[...])

    @pl.when(f == pl.num_programs(1) - 1)
    def _():
        y = _rms(acc_scr[...]) * gpost_ref[...]
        o_ref[...] = x_ref[...] + mod_ref[MOD_GT2:MOD_GT2 + 1, :] * y


def _mlp(x2, mod, g_pre, g_post, w1, w2, layer, seq):
    m = x2.shape[0]
    tm = min(512, seq)
    tf = 1024
    tiles_per_seq = seq // tm
    return pl.pallas_call(
        _mlp_kernel,
        out_shape=jax.ShapeDtypeStruct((m, D_MODEL), F32),
        grid=(m // tm, D_FF // tf),
        in_specs=[
            pl.BlockSpec((tm, D_MODEL), lambda i, f: (i, 0)),
            pl.BlockSpec((None, N_MOD, D_MODEL), lambda i, f: (i // tiles_per_seq, 0, 0)),
            pl.BlockSpec((1, D_MODEL), lambda i, f: (0, 0)),
            pl.BlockSpec((1, D_MODEL), lambda i, f: (0, 0)),
            pl.BlockSpec((None, D_MODEL, tf), lambda i, f: (layer, 0, f)),
            pl.BlockSpec((None, tf, D_MODEL), lambda i, f: (layer, f, 0)),
        ],
        out_specs=pl.BlockSpec((tm, D_MODEL), lambda i, f: (i, 0)),
        scratch_shapes=[pltpu.VMEM((tm, D_MODEL), BF16), pltpu.VMEM((tm, D_MODEL), F32)],
        compiler_params=_params("parallel", "arbitrary"),
        name="mlp",
    )(x2, mod, g_pre, g_post, w1, w2)


def _split_w_in(w_in):
    offs = [0]
    for s in IN_SPLITS:
        offs.append(offs[-1] + s)
    part = lambda lo, hi: w_in[:, :, offs[lo]:offs[hi]]
    w_big = jnp.concatenate([part(0, 4), part(6, 12), part(13, 14)], axis=2).astype(BF16)
    pad = jnp.zeros(w_in.shape[:2] + (SMALL_W - 2 * GDN_HEADS - GLA_GATE_RANK,), w_in.dtype)
    w_small = jnp.concatenate([part(4, 6), part(12, 13), pad], axis=2).astype(BF16)
    return w_big, w_small, jnp.swapaxes(w_small, 1, 2)


def kernel(x, c, w_ada, b_ada, g_pre_mix, g_post_mix, g_pre_mlp, g_post_mlp, w_in, conv_gdn, gdn_a_log, gdn_dt_bias, gdn_norm, conv_lru, conv_lru_b, lru_w_a, lru_b_a, lru_w_i, lru_b_i, lru_lambda, gla_w_gate, gla_b_gate, gla_norm, w_branch, w_out, w_mlp1, w_mlp2):
    bsz, seq, _ = x.shape
    n_layers = w_in.shape[0]
    assert seq % LRU_TIME_TILE == 0 and x.shape[2] == D_MODEL
    row = lambda v: v.reshape(1, D_MODEL)

    mod_all = _ada_mod(c, w_ada, b_ada).reshape(n_layers, bsz, N_MOD, D_MODEL)
    x2 = x.reshape(bsz * seq, D_MODEL)
    w_big, w_small, w_small_t = _split_w_in(w_in)
    w_branch_b = w_branch.astype(BF16)
    w_out_b = w_out.astype(BF16)
    w_mlp1_b = w_mlp1.astype(BF16)
    w_mlp2_b = w_mlp2.astype(BF16)
    for l in range(n_layers):
        mod = mod_all[l]
        big, small, small_t = _in_proj(x2, mod, row(g_pre_mix[l]), w_big, w_small, w_small_t,
                                       l, seq)
        oa = _gdn(big, small, small_t, conv_gdn[l], gdn_a_log[l], gdn_dt_bias[l], gdn_norm[l],
                  bsz, seq)
        ob = _lru(big, conv_lru[l], conv_lru_b[l], lru_w_a[l], lru_b_a[l], lru_w_i[l],
                  lru_b_i[l], lru_lambda[l], bsz, seq)
        oc = _gla(big, small, gla_w_gate[l], gla_b_gate[l], gla_norm[l], bsz, seq)
        merged = _merge(big, oa, ob, oc, w_branch_b, l, seq)
        x2 = _out_proj(merged, x2, mod, row(g_post_mix[l]), w_out_b, l, seq)
        x2 = _mlp(x2, mod, row(g_pre_mlp[l]), row(g_post_mlp[l]), w_mlp1_b, w_mlp2_b, l, seq)
    return x2.reshape(bsz, seq, D_MODEL)
```

```python
import jax
import jax.numpy as jnp
from jax import lax
from jax.experimental import pallas as pl
from jax.experimental.pallas import tpu as pltpu

F32 = jnp.float32
BF16 = jnp.bfloat16
HIGHEST = lax.Precision.HIGHEST

D_MODEL = 2048
MIX_WIDTH = D_MODEL // 2
N_BRANCH = 3
GDN_DK = 128
GDN_DV = 128
GDN_HEADS = MIX_WIDTH // GDN_DV
GDN_CHUNK = 64
LRU_WIDTH = MIX_WIDTH
LRU_BLOCKS = 8
LRU_BLOCK_W = LRU_WIDTH // LRU_BLOCKS
LRU_C = 8.0
GLA_HEADS = 4
GLA_DV = MIX_WIDTH // GLA_HEADS
GLA_DK = GLA_DV // 2
GLA_GATE_RANK = 16
GLA_GATE_NORM = 16.0
GLA_BASE_CHUNK = 16
CONV_K = 4
D_FF = 4 * D_MODEL
N_MOD = 6
EPS = 1e-6

GDN_QK_W = GDN_HEADS * GDN_DK
GDN_V_W = GDN_HEADS * GDN_DV
GLA_QK_W = GLA_HEADS * GLA_DK
GLA_V_W = GLA_HEADS * GLA_DV
IN_SPLITS = (GDN_QK_W, GDN_QK_W, GDN_V_W, GDN_V_W, GDN_HEADS, GDN_HEADS,
             LRU_WIDTH, LRU_WIDTH,
             GLA_QK_W, GLA_QK_W, GLA_V_W, GLA_V_W, GLA_GATE_RANK,
             N_BRANCH * D_MODEL)
IN_OFFS = tuple(sum(IN_SPLITS[:i]) for i in range(len(IN_SPLITS) + 1))

LANES = 128
SUBLANES = 8
VMEM_LIMIT_BYTES = 56 * 1024 * 1024

BIG_GDN = 0
BIG_LRU = BIG_GDN + 4 * MIX_WIDTH
BIG_GLA = BIG_LRU + 2 * LRU_WIDTH
BIG_GATES = BIG_GLA + 2 * GLA_QK_W + 2 * GLA_V_W
BIG_W = BIG_GATES + N_BRANCH * D_MODEL
SMALL_BETA = 0
SMALL_ALPHA = GDN_HEADS
SMALL_GK = 2 * GDN_HEADS
SMALL_W = LANES

MOD_SH1, MOD_SC1, MOD_GT1, MOD_SH2, MOD_SC2, MOD_GT2 = range(N_MOD)

TIME_TILE = 128
GDN_BATCH_TILE = 2
GLA_BATCH_TILE = 4
LRU_TIME_TILE = 256
HALO = SUBLANES

W_PREP_TILE = 512
W_PREP_RUN1_TILE = IN_OFFS[4] // W_PREP_TILE
W_PREP_RUN2_TILE = (IN_OFFS[4] + IN_OFFS[12] - IN_OFFS[6]) // W_PREP_TILE
W_PREP_SHIFT1 = IN_OFFS[6] - IN_OFFS[4]
W_PREP_SHIFT2 = W_PREP_SHIFT1 + IN_OFFS[13] - IN_OFFS[12]


def _sigmoid(x):
    return 1.0 / (1.0 + jnp.exp(-x))


def _silu(x):
    return x * _sigmoid(x)


def _softplus(x):
    return jnp.maximum(x, 0.0) + jnp.log(1.0 + jnp.exp(-jnp.abs(x)))


def _log_sigmoid(x):
    return -_softplus(-x)


def _gelu_tanh(x):
    c = 0.7978845608028654
    return 0.5 * x * (1.0 + jnp.tanh(c * (x + 0.044715 * (x * x * x))))


def _rms(x):
    return x * lax.rsqrt(jnp.mean(x * x, axis=-1, keepdims=True) + EPS)


def _dot(a, b, precision=None):
    return jnp.dot(a, b, preferred_element_type=F32, precision=precision)


def _dot_nt(a, b):
    return lax.dot_general(a, b, (((1,), (1,)), ((), ())), preferred_element_type=F32)


def _bdot(a, b):
    return _dot(a.astype(BF16), b.astype(BF16))


def _bdot_nt(a, b):
    return _dot_nt(a.astype(BF16), b.astype(BF16))


def _params(*sem):
    return pltpu.CompilerParams(dimension_semantics=sem, vmem_limit_bytes=VMEM_LIMIT_BYTES)


def _ada_kernel(c_ref, w_ref, b_ref, o_ref):
    sc = _silu(c_ref[...]).astype(BF16)
    o_ref[...] = _dot(sc, w_ref[...].astype(BF16)) + b_ref[...]


def _ada_mod(c, w_ada, b_ada):
    n_layers, _, n_out = w_ada.shape
    bsz = c.shape[0]
    tn = 1024
    return pl.pallas_call(
        _ada_kernel,
        out_shape=jax.ShapeDtypeStruct((n_layers, bsz, n_out), F32),
        grid=(n_layers, n_out // tn),
        in_specs=[
            pl.BlockSpec((bsz, D_MODEL), lambda l, j: (0, 0)),
            pl.BlockSpec((None, D_MODEL, tn), lambda l, j: (l, 0, j)),
            pl.BlockSpec((None, 1, tn), lambda l, j: (l, 0, j)),
        ],
        out_specs=pl.BlockSpec((None, bsz, tn), lambda l, j: (l, 0, j)),
        compiler_params=_params("parallel", "parallel"),
        name="ada_mod",
    )(c, w_ada, b_ada.reshape(n_layers, 1, n_out))


def _w_prep_kernel(a_ref, b_ref, o_ref):
    j = pl.program_id(1)

    def shifted(shift):
        row = jnp.concatenate([a_ref[...], b_ref[...]], axis=1)
        return pltpu.roll(row, row.shape[1] - shift, 1)[:, :W_PREP_TILE].astype(BF16)

    @pl.when(j < W_PREP_RUN1_TILE)
    def _():
        o_ref[...] = a_ref[...].astype(BF16)

    @pl.when((j >= W_PREP_RUN1_TILE) & (j < W_PREP_RUN2_TILE))
    def _():
        o_ref[...] = shifted(W_PREP_SHIFT1)

    @pl.when(j >= W_PREP_RUN2_TILE)
    def _():
        o_ref[...] = shifted(W_PREP_SHIFT2)


def _split_w_in(w_in):
    n_layers = w_in.shape[0]
    lanes_per_tile = W_PREP_TILE // LANES
    w_big = pl.pallas_call(
        _w_prep_kernel,
        out_shape=jax.ShapeDtypeStruct((n_layers, D_MODEL, BIG_W), BF16),
        grid=(n_layers, BIG_W // W_PREP_TILE),
        in_specs=[
            pl.BlockSpec((None, D_MODEL, W_PREP_TILE), lambda l, j: (l, 0, j)),
            pl.BlockSpec((None, D_MODEL, LANES), lambda l, j: (l, 0, (j + 1) * lanes_per_tile)),
        ],
        out_specs=pl.BlockSpec((None, D_MODEL, W_PREP_TILE), lambda l, j: (l, 0, j)),
        compiler_params=_params("parallel", "parallel"),
        name="w_prep",
    )(w_in, w_in)
    part = lambda lo, hi: w_in[:, :, IN_OFFS[lo]:IN_OFFS[hi]]
    pad = jnp.zeros(w_in.shape[:2] + (SMALL_W - 2 * GDN_HEADS - GLA_GATE_RANK,), w_in.dtype)
    w_small = jnp.concatenate([part(4, 6), part(12, 13), pad], axis=2).astype(BF16)
    return w_big, w_small, jnp.swapaxes(w_small, 1, 2)


def _in_proj_kernel(x_ref, mod_ref, g_ref, w_ref, ws_ref, wst_ref,
                    big_ref, small_ref, smallt_ref, h_scr):
    @pl.when(pl.program_id(1) == 0)
    def _():
        h = _rms(x_ref[...]) * g_ref[...]
        h = h * (1.0 + mod_ref[MOD_SC1:MOD_SC1 + 1, :]) + mod_ref[MOD_SH1:MOD_SH1 + 1, :]
        hb = h.astype(BF16)
        h_scr[...] = hb
        small_ref[...] = _dot(hb, ws_ref[...])
        smallt_ref[...] = _dot_nt(wst_ref[...], hb)

    big_ref[...] = _dot(h_scr[...], w_ref[...])


def _in_proj(x2, mod, g, w_big, w_small, w_small_t, layer, seq):
    m = x2.shape[0]
    tm = min(1024, seq)
    tn = 1024
    tiles_per_seq = seq // tm
    return pl.pallas_call(
        _in_proj_kernel,
        out_shape=(jax.ShapeDtypeStruct((m, BIG_W), F32),
                   jax.ShapeDtypeStruct((m, SMALL_W), F32),
                   jax.ShapeDtypeStruct((SMALL_W, m), F32)),
        grid=(m // tm, BIG_W // tn),
        in_specs=[
            pl.BlockSpec((tm, D_MODEL), lambda i, j: (i, 0)),
            pl.BlockSpec((None, N_MOD, D_MODEL), lambda i, j: (i // tiles_per_seq, 0, 0)),
            pl.BlockSpec((1, D_MODEL), lambda i, j: (0, 0)),
            pl.BlockSpec((None, D_MODEL, tn), lambda i, j: (layer, 0, j)),
            pl.BlockSpec((None, D_MODEL, SMALL_W), lambda i, j: (layer, 0, 0)),
            pl.BlockSpec((None, SMALL_W, D_MODEL), lambda i, j: (layer, 0, 0)),
        ],
        out_specs=(pl.BlockSpec((tm, tn), lambda i, j: (i, j)),
                   pl.BlockSpec((tm, SMALL_W), lambda i, j: (i, 0)),
                   pl.BlockSpec((SMALL_W, tm), lambda i, j: (0, i))),
        scratch_shapes=[pltpu.VMEM((tm, D_MODEL), BF16)],
        compiler_params=_params("parallel", "arbitrary"),
        name="in_proj",
    )(x2, mod, g, w_big, w_small, w_small_t)


def _causal_conv(xbuf, cw_ref, rows):
    acc = cw_ref[CONV_K - 1:CONV_K, :] * xbuf[HALO:HALO + rows, :]
    for j in range(CONV_K - 1):
        off = HALO - (CONV_K - 1) + j
        acc = acc + cw_ref[j:j + 1, :] * xbuf[off:off + rows, :]
    return acc


def _gdn_kernel(*refs):
    nb = GDN_BATCH_TILE
    q_ref, k_ref, v_ref, z_ref, sm_ref = refs[:5]
    smt_refs = refs[5:5 + nb]
    cw_ref, alr_ref, alc_ref, dtr_ref, dtc_ref, nw_ref, o_ref, xbuf, s_scr = refs[5 + nb:]
    rows = TIME_TILE

    @pl.when(pl.program_id(1) == 0)
    def _():
        xbuf[:, 0:HALO, :] = jnp.zeros((nb, HALO, xbuf.shape[2]), F32)
        s_scr[...] = jnp.zeros(s_scr.shape, F32)

    ri = lax.broadcasted_iota(jnp.int32, (rows, rows), 0)
    ci = lax.broadcasted_iota(jnp.int32, (rows, rows), 1)
    same = (ri // GDN_CHUNK) == (ci // GDN_CHUNK)
    causal = same & (ci <= ri)
    strict = same & (ci < ri)
    cum_c = causal.astype(F32)
    cum_r = (same & (ri <= ci)).astype(F32)
    eye = (ri == ci).astype(F32)
    level_masks = []
    s = 1
    while s < GDN_CHUNK:
        level_masks.append((ri // (2 * s) == ci // (2 * s)) & (ri % (2 * s) >= s) & (ci % (2 * s) < s))
        s *= 2
    zeros_half = jnp.zeros((GDN_CHUNK, GDN_DV), F32)
    col_row = lax.broadcasted_iota(jnp.int32, (1, rows), 1)
    n_chunks = rows // GDN_CHUNK

    units = [(bb, h) for bb in range(nb) for h in range(GDN_HEADS)]
    idx = range(len(units))
    qs, kts, gccs, gcrs, a_qks, a_kks, rhss = [], [], [], [], [], [], []
    for bb in range(nb):
        xb = xbuf.at[bb]
        xb[HALO:HALO + rows, 0:GDN_QK_W] = q_ref[bb]
        xb[HALO:HALO + rows, GDN_QK_W:2 * GDN_QK_W] = k_ref[bb]
        xb[HALO:HALO + rows, 2 * GDN_QK_W:2 * GDN_QK_W + GDN_V_W] = v_ref[bb]
        qkv = _silu(_causal_conv(xb, cw_ref, rows))
        xb[0:HALO, :] = xb[rows:rows + HALO, :]

        sm = sm_ref[bb]
        smt = smt_refs[bb][...]
        beta_c = _sigmoid(sm)
        g_c = -jnp.exp(alr_ref[...]) * _softplus(sm + dtr_ref[...])
        g_r = -jnp.exp(alc_ref[...]) * _softplus(smt + dtc_ref[...])
        gc_c = _dot(cum_c, g_c, precision=HIGHEST)
        gc_r = _dot(g_r, cum_r, precision=HIGHEST)

        for h in range(GDN_HEADS):
            qh = qkv[:, h * GDN_DK:(h + 1) * GDN_DK]
            kh = qkv[:, GDN_QK_W + h * GDN_DK:GDN_QK_W + (h + 1) * GDN_DK]
            vh = qkv[:, 2 * GDN_QK_W + h * GDN_DV:2 * GDN_QK_W + (h + 1) * GDN_DV]
            qh = qh * lax.rsqrt(jnp.sum(qh * qh, axis=-1, keepdims=True) + EPS) * (GDN_DK ** -0.5)
            kh = kh * lax.rsqrt(jnp.sum(kh * kh, axis=-1, keepdims=True) + EPS)
            bc = beta_c[:, SMALL_BETA + h:SMALL_BETA + h + 1]
            gcc = gc_c[:, SMALL_ALPHA + h:SMALL_ALPHA + h + 1]
            gcr = gc_r[SMALL_ALPHA + h:SMALL_ALPHA + h + 1, :]
            e = jnp.exp(jnp.where(causal, gcc - gcr, 0.0))
            kb = kh * bc
            kt = kh.T
            qk_kk = _bdot(jnp.concatenate([qh, kb], axis=0), kt)
            eg = jnp.exp(gcc)
            qs.append(qh * eg)
            kts.append(kt)
            gccs.append(gcc)
            gcrs.append(gcr)
            a_qks.append(jnp.where(causal, qk_kk[:rows] * e, 0.0))
            a_kks.append(jnp.where(strict, qk_kk[rows:] * e, 0.0))
            rhss.append(jnp.concatenate([vh * bc, kb * eg], axis=1))

    tinv = [eye - jnp.where(level_masks[0], a_kks[u], 0.0) for u in idx]
    for lm in level_masks[1:]:
        et = [_bdot(jnp.where(lm, a_kks[u], 0.0), tinv[u]) for u in idx]
        tinv = [tinv[u] - _bdot(tinv[u], et[u]) for u in idx]
    uws = [_bdot(tinv[u], rhss[u]) for u in idx]

    states = [s_scr[bb, h] for bb, h in units]
    outs = [[] for _ in idx]
    for c in range(n_chunks):
        lo = c * GDN_CHUNK
        hi = lo + GDN_CHUNK
        in_chunk = (col_row >= lo) & (col_row < hi)
        rs = [_bdot(jnp.concatenate([uws[u][lo:hi, GDN_DV:], qs[u][lo:hi]], axis=0), states[u])
              for u in idx]
        v_pads = []
        for u in idx:
            pads = [zeros_half] * n_chunks
            pads[c] = uws[u][lo:hi, :GDN_DV] - rs[u][:GDN_CHUNK]
            v_pads.append(jnp.concatenate(pads, axis=0))
        new_states = []
        for u in idx:
            g_last = gccs[u][hi - 1:hi, :]
            tail = jnp.where(in_chunk, jnp.exp(jnp.where(in_chunk, g_last - gcrs[u], 0.0)), 0.0)
            new_states.append(states[u] * jnp.exp(g_last) + _bdot(kts[u] * tail, v_pads[u]))
        states = new_states
        for u in idx:
            outs[u].append(rs[u][GDN_CHUNK:] + _bdot(a_qks[u][lo:hi, :], v_pads[u]))
    for u, (bb, h) in enumerate(units):
        s_scr[bb, h] = states[u]
        o = _rms(jnp.concatenate(outs[u], axis=0)) * nw_ref[...]
        zg = _silu(z_ref[bb, :, h * GDN_DV:(h + 1) * GDN_DV])
        o_ref[bb, :, h * GDN_DV:(h + 1) * GDN_DV] = (o * zg).astype(BF16)


def _gdn(big, small, small_t, conv_w, a_log, dt_bias, norm_w, bsz, seq):
    nb = GDN_BATCH_TILE
    nt = seq // TIME_TILE
    qkv_w = 2 * GDN_QK_W + GDN_V_W
    big3 = big.reshape(bsz, seq, BIG_W)
    small3 = small.reshape(bsz, seq, SMALL_W)

    def pad_row(v):
        return jnp.zeros((1, SMALL_W), F32).at[0, SMALL_ALPHA:SMALL_ALPHA + GDN_HEADS].set(v)

    alr = pad_row(a_log)
    dtr = pad_row(dt_bias)
    col = lambda cb: pl.BlockSpec((nb, TIME_TILE, MIX_WIDTH), lambda g, t: (g, t, cb))
    const = lambda shape: pl.BlockSpec(shape, lambda g, t: (0, 0))
    small_t_spec = lambda bb: pl.BlockSpec(
        (SMALL_W, TIME_TILE), lambda g, t: (0, (g * nb + bb) * nt + t))
    out = pl.pallas_call(
        _gdn_kernel,
        out_shape=jax.ShapeDtypeStruct((bsz, seq, MIX_WIDTH), BF16),
        grid=(bsz // nb, nt),
        in_specs=[
            col(0), col(1), col(2), col(3),
            pl.BlockSpec((nb, TIME_TILE, SMALL_W), lambda g, t: (g, t, 0)),
            *[small_t_spec(bb) for bb in range(nb)],
            const((CONV_K, qkv_w)),
            const((1, SMALL_W)), const((SMALL_W, 1)),
            const((1, SMALL_W)), const((SMALL_W, 1)),
            const((1, GDN_DV)),
        ],
        out_specs=pl.BlockSpec((nb, TIME_TILE, MIX_WIDTH), lambda g, t: (g, t, 0)),
        scratch_shapes=[pltpu.VMEM((nb, HALO + TIME_TILE, qkv_w), F32),
                        pltpu.VMEM((nb, GDN_HEADS, GDN_DK, GDN_DV), F32)],
        compiler_params=_params("parallel", "arbitrary"),
        name="gdn",
    )(big3, big3, big3, big3, small3, *([small_t] * nb), conv_w, alr, alr.T, dtr, dtr.T,
      norm_w.reshape(1, GDN_DV))
    return out.reshape(bsz * seq, MIX_WIDTH)


def _lru_kernel(x_ref, y_ref, cw_ref, cb_ref, wa_ref, ba_ref, wi_ref, bi_ref, lam_ref,
                o_ref, xbuf, h_scr):
    rows = LRU_TIME_TILE
    first = pl.program_id(1) == 0

    @pl.when(first)
    def _():
        xbuf[0:HALO, :] = jnp.zeros((HALO, LRU_WIDTH), F32)
        h_scr[...] = jnp.zeros(h_scr.shape, F32)

    xbuf[HALO:HALO + rows, :] = x_ref[...]
    xl = _causal_conv(xbuf, cw_ref, rows) + cb_ref[...]
    xbuf[0:HALO, :] = xbuf[rows:rows + HALO, :]

    xlb = xl.astype(BF16)
    r_parts = []
    i_parts = []
    for k in range(LRU_BLOCKS):
        blk = xlb[:, k * LRU_BLOCK_W:(k + 1) * LRU_BLOCK_W]
        r_parts.append(_dot(blk, wa_ref[k]))
        i_parts.append(_dot(blk, wi_ref[k]))
    r = _sigmoid(jnp.concatenate(r_parts, axis=1) + ba_ref[...])
    ig = _sigmoid(jnp.concatenate(i_parts, axis=1) + bi_ref[...])
    log_a = -LRU_C * r * _softplus(-lam_ref[...])
    a = jnp.exp(log_a)
    mult = jnp.sqrt(-jnp.tanh(log_a) * (a * a + 1.0))
    row = lax.broadcasted_iota(jnp.int32, (rows, 1), 0)
    mult = jnp.where(first & (row == 0), 1.0, mult)
    b = mult * ig * xl

    row8 = row[:SUBLANES]
    shift = 1
    while shift < SUBLANES:
        a_roll = pltpu.roll(a, shift, 0)
        b_roll = pltpu.roll(b, shift, 0)
        keep = row8 >= shift
        a_prev = jnp.concatenate([jnp.where(keep, a_roll[:SUBLANES], 1.0), a_roll[SUBLANES:]], axis=0)
        b_prev = jnp.concatenate([jnp.where(keep, b_roll[:SUBLANES], 0.0), b_roll[SUBLANES:]], axis=0)
        b = a * b_prev + b
        a = a * a_prev
        shift *= 2
    while shift < rows:
        b = jnp.concatenate([b[:shift], a[shift:] * b[:rows - shift] + b[shift:]], axis=0)
        a = jnp.concatenate([a[:shift], a[shift:] * a[:rows - shift]], axis=0)
        shift *= 2
    hs = a * h_scr[0:1, :] + b
    h_scr[0:1, :] = hs[rows - 1:rows, :]
    o_ref[...] = (_gelu_tanh(y_ref[...]) * hs).astype(BF16)


def _lru(big, conv_w, conv_b, w_a, b_a, w_i, b_i, lam, bsz, seq):
    m = bsz * seq
    nt = seq // LRU_TIME_TILE
    base = BIG_LRU // LRU_WIDTH
    row1 = lambda v: v.reshape(1, LRU_WIDTH)
    const2 = lambda shape: pl.BlockSpec(shape, lambda b, t: (0, 0))
    const3 = lambda shape: pl.BlockSpec(shape, lambda b, t: (0, 0, 0))
    return pl.pallas_call(
        _lru_kernel,
        out_shape=jax.ShapeDtypeStruct((m, LRU_WIDTH), BF16),
        grid=(bsz, nt),
        in_specs=[
            pl.BlockSpec((LRU_TIME_TILE, LRU_WIDTH), lambda b, t: (b * nt + t, base)),
            pl.BlockSpec((LRU_TIME_TILE, LRU_WIDTH), lambda b, t: (b * nt + t, base + 1)),
            const2((CONV_K, LRU_WIDTH)), const2((1, LRU_WIDTH)),
            const3((LRU_BLOCKS, LRU_BLOCK_W, LRU_BLOCK_W)), const2((1, LRU_WIDTH)),
            const3((LRU_BLOCKS, LRU_BLOCK_W, LRU_BLOCK_W)), const2((1, LRU_WIDTH)),
            const2((1, LRU_WIDTH)),
        ],
        out_specs=pl.BlockSpec((LRU_TIME_TILE, LRU_WIDTH), lambda b, t: (b * nt + t, 0)),
        scratch_shapes=[pltpu.VMEM((HALO + LRU_TIME_TILE, LRU_WIDTH), F32),
                        pltpu.VMEM((SUBLANES, LRU_WIDTH), F32)],
        compiler_params=_params("parallel", "arbitrary"),
        name="rg_lru",
    )(big, big, conv_w, row1(conv_b), w_a.astype(BF16), row1(b_a), w_i.astype(BF16),
      row1(b_i), row1(lam))


def _gla_kernel(q_ref, k_ref, v_ref, z_ref, sm_ref, wg_ref, bg_ref, nw_ref, o_ref, st_scr):
    rows = TIME_TILE

    @pl.when(pl.program_id(1) == 0)
    def _():
        st_scr[...] = jnp.zeros(st_scr.shape, F32)

    nb = GLA_BATCH_TILE
    ri = lax.broadcasted_iota(jnp.int32, (rows, rows), 0)
    ci = lax.broadcasted_iota(jnp.int32, (rows, rows), 1)
    tri = (ci <= ri).astype(F32)
    bcums = []
    for bb in range(nb):
        log_a = _log_sigmoid(_dot(sm_ref[bb].astype(BF16), wg_ref[...]) + bg_ref[...]) / GLA_GATE_NORM
        bcums.append(_dot(tri, log_a, precision=HIGHEST))
    row = lax.broadcasted_iota(jnp.int32, (rows, 1), 0)

    def boundary_rows(bh, width, offset, first_zero):
        out = jnp.zeros_like(bh) if first_zero else None
        for g in range(rows // width):
            src = g * width + offset
            if src < 0:
                continue
            val = jnp.broadcast_to(bh[src:src + 1, :], bh.shape)
            out = val if out is None else jnp.where(row // width == g, val, out)
        return out

    units = [(bb, h) for bb in range(nb) for h in range(GLA_HEADS)]
    heads = range(len(units))
    bhs = [bcums[bb][:, h * GLA_DK:(h + 1) * GLA_DK] for bb, h in units]
    qh = [q_ref[bb, :, h * GLA_DK:(h + 1) * GLA_DK] * (GLA_DK ** -0.5) for bb, h in units]
    kh = [k_ref[bb, :, h * GLA_DK:(h + 1) * GLA_DK] for bb, h in units]
    vh = [v_ref[bb, :, h * GLA_DV:(h + 1) * GLA_DV] for bb, h in units]
    b_loc = [bhs[h] - boundary_rows(bhs[h], GLA_BASE_CHUNK, -1, True) for h in heads]
    base_mask = (ri // GLA_BASE_CHUNK == ci // GLA_BASE_CHUNK) & (ci <= ri)
    att = [jnp.where(base_mask,
                     _bdot_nt(qh[h] * jnp.exp(b_loc[h]), kh[h] * jnp.exp(-b_loc[h])), 0.0)
           for h in heads]
    s = GLA_BASE_CHUNK
    while s < rows:
        mask = ((ri // (2 * s) == ci // (2 * s)) & (ri % (2 * s) >= s) & (ci % (2 * s) < s))
        refs = [boundary_rows(bhs[h], 2 * s, s - 1, False) for h in heads]
        cross = [_bdot_nt(qh[h] * jnp.exp(jnp.minimum(bhs[h] - refs[h], 0.0)),
                          kh[h] * jnp.exp(jnp.minimum(refs[h] - bhs[h], 0.0))) for h in heads]
        att = [att[h] + jnp.where(mask, cross[h], 0.0) for h in heads]
        s *= 2
    sts = [st_scr[bb, h] for bb, h in units]
    o_inter = [_bdot_nt(qh[u] * jnp.exp(bhs[u]), sts[u]) for u in heads]
    o_intra = [_bdot(att[u], vh[u]) for u in heads]
    b_end = [bhs[u][rows - 1:rows, :] for u in heads]
    upd = [_bdot(vh[u].T, kh[u] * jnp.exp(b_end[u] - bhs[u])) for u in heads]
    for u, (bb, h) in enumerate(units):
        st_scr[bb, h] = sts[u] * jnp.exp(b_end[u]) + upd[u]
        o = _rms(o_intra[u] + o_inter[u]) * nw_ref[...]
        zg = _silu(z_ref[bb, :, h * GLA_DV:(h + 1) * GLA_DV])
        o_ref[bb, :, h * GLA_DV:(h + 1) * GLA_DV] = (o * zg).astype(BF16)


def _gla(big, small, w_gate, b_gate, norm_w, bsz, seq):
    nb = GLA_BATCH_TILE
    nt = seq // TIME_TILE
    big3 = big.reshape(bsz, seq, BIG_W)
    small3 = small.reshape(bsz, seq, SMALL_W)
    wg = jnp.zeros((SMALL_W, GLA_QK_W), F32).at[SMALL_GK:SMALL_GK + GLA_GATE_RANK].set(w_gate)
    qk_base = BIG_GLA // GLA_QK_W
    v_base = (BIG_GLA + 2 * GLA_QK_W) // GLA_V_W
    const = lambda shape: pl.BlockSpec(shape, lambda g, t: (0, 0))
    tile = lambda width, cb: pl.BlockSpec((nb, TIME_TILE, width), lambda g, t: (g, t, cb))
    out = pl.pallas_call(
        _gla_kernel,
        out_shape=jax.ShapeDtypeStruct((bsz, seq, MIX_WIDTH), BF16),
        grid=(bsz // nb, nt),
        in_specs=[
            tile(GLA_QK_W, qk_base), tile(GLA_QK_W, qk_base + 1),
            tile(GLA_V_W, v_base), tile(GLA_V_W, v_base + 1),
            tile(SMALL_W, 0),
            const((SMALL_W, GLA_QK_W)), const((1, GLA_QK_W)), const((1, GLA_DV)),
        ],
        out_specs=tile(MIX_WIDTH, 0),
        scratch_shapes=[pltpu.VMEM((nb, GLA_HEADS, GLA_DV, GLA_DK), F32)],
        compiler_params=_params("parallel", "arbitrary"),
        name="gla",
    )(big3, big3, big3, big3, small3, wg.astype(BF16), b_gate.reshape(1, GLA_QK_W),
      norm_w.reshape(1, GLA_DV))
    return out.reshape(bsz * seq, MIX_WIDTH)


def _merge_kernel(ga_ref, gb_ref, gc_ref, oa_ref, ob_ref, oc_ref, wb_ref, o_ref):
    acc = _sigmoid(ga_ref[...]) * _dot(oa_ref[...], wb_ref[0])
    acc = acc + _sigmoid(gb_ref[...]) * _dot(ob_ref[...], wb_ref[1])
    acc = acc + _sigmoid(gc_ref[...]) * _dot(oc_ref[...], wb_ref[2])
    o_ref[...] = acc.astype(BF16)


def _merge(big, oa, ob, oc, w_branch, layer, seq):
    m = oa.shape[0]
    tm = min(1024, seq)
    tn = 512
    gate = lambda n: pl.BlockSpec(
        (tm, tn), lambda i, j: (i, (BIG_GATES + n * D_MODEL) // tn + j))
    branch = pl.BlockSpec((tm, MIX_WIDTH), lambda i, j: (i, 0))
    return pl.pallas_call(
        _merge_kernel,
        out_shape=jax.ShapeDtypeStruct((m, D_MODEL), BF16),
        grid=(m // tm, D_MODEL // tn),
        in_specs=[gate(0), gate(1), gate(2), branch, branch, branch,
                  pl.BlockSpec((None, N_BRANCH, MIX_WIDTH, tn), lambda i, j: (layer, 0, 0, j))],
        out_specs=pl.BlockSpec((tm, tn), lambda i, j: (i, j)),
        compiler_params=_params("parallel", "arbitrary"),
        name="merge",
    )(big, big, big, oa, ob, oc, w_branch)


def _out_proj_kernel(m_ref, x_ref, mod_ref, g_ref, w_ref, o_ref):
    y = _dot(m_ref[...], w_ref[...])
    o_ref[...] = x_ref[...] + mod_ref[MOD_GT1:MOD_GT1 + 1, :] * (_rms(y) * g_ref[...])


def _out_proj(merged, x2, mod, g, w_out, layer, seq):
    m = x2.shape[0]
    tm = min(512, seq)
    tiles_per_seq = seq // tm
    return pl.pallas_call(
        _out_proj_kernel,
        out_shape=jax.ShapeDtypeStruct((m, D_MODEL), F32),
        grid=(m // tm,),
        in_specs=[
            pl.BlockSpec((tm, D_MODEL), lambda i: (i, 0)),
            pl.BlockSpec((tm, D_MODEL), lambda i: (i, 0)),
            pl.BlockSpec((None, N_MOD, D_MODEL), lambda i: (i // tiles_per_seq, 0, 0)),
            pl.BlockSpec((1, D_MODEL), lambda i: (0, 0)),
            pl.BlockSpec((None, D_MODEL, D_MODEL), lambda i: (layer, 0, 0)),
        ],
        out_specs=pl.BlockSpec((tm, D_MODEL), lambda i: (i, 0)),
        compiler_params=_params("parallel"),
        name="out_proj",
    )(merged, x2, mod, g, w_out)


def _mlp_kernel(x_ref, mod_ref, gpre_ref, gpost_ref, w1_ref, w2_ref, o_ref, h_scr, acc_scr):
    f = pl.program_id(1)

    @pl.when(f == 0)
    def _():
        h = _rms(x_ref[...]) * gpre_ref[...]
        h = h * (1.0 + mod_ref[MOD_SC2:MOD_SC2 + 1, :]) + mod_ref[MOD_SH2:MOD_SH2 + 1, :]
        h_scr[...] = h.astype(BF16)
        acc_scr[...] = jnp.zeros(acc_scr.shape, F32)

    a = jnp.maximum(_dot(h_scr[...], w1_ref[...]), 0.0)
    acc_scr[...] += _dot((a * a).astype(BF16), w2_ref[...])

    @pl.when(f == pl.num_programs(1) - 1)
    def _():
        y = _rms(acc_scr[...]) * gpost_ref[...]
        o_ref[...] = x_ref[...] + mod_ref[MOD_GT2:MOD_GT2 + 1, :] * y


def _mlp(x2, mod, g_pre, g_post, w1, w2, layer, seq):
    m = x2.shape[0]
    tm = min(512, seq)
    tf = 1024
    tiles_per_seq = seq // tm
    return pl.pallas_call(
        _mlp_kernel,
        out_shape=jax.ShapeDtypeStruct((m, D_MODEL), F32),
        grid=(m // tm, D_FF // tf),
        in_specs=[
            pl.BlockSpec((tm, D_MODEL), lambda i, f: (i, 0)),
            pl.BlockSpec((None, N_MOD, D_MODEL), lambda i, f: (i // tiles_per_seq, 0, 0)),
            pl.BlockSpec((1, D_MODEL), lambda i, f: (0, 0)),
            pl.BlockSpec((1, D_MODEL), lambda i, f: (0, 0)),
            pl.BlockSpec((None, D_MODEL, tf), lambda i, f: (layer, 0, f)),
            pl.BlockSpec((None, tf, D_MODEL), lambda i, f: (layer, f, 0)),
        ],
        out_specs=pl.BlockSpec((tm, D_MODEL), lambda i, f: (i, 0)),
        scratch_shapes=[pltpu.VMEM((tm, D_MODEL), BF16), pltpu.VMEM((tm, D_MODEL), F32)],
        compiler_params=_params("parallel", "arbitrary"),
        name="mlp",
    )(x2, mod, g_pre, g_post, w1, w2)


def kernel(x, c, w_ada, b_ada, g_pre_mix, g_post_mix, g_pre_mlp, g_post_mlp, w_in, conv_gdn, gdn_a_log, gdn_dt_bias, gdn_norm, conv_lru, conv_lru_b, lru_w_a, lru_b_a, lru_w_i, lru_b_i, lru_lambda, gla_w_gate, gla_b_gate, gla_norm, w_branch, w_out, w_mlp1, w_mlp2):
    bsz, seq, _ = x.shape
    n_layers = w_in.shape[0]
    assert seq % LRU_TIME_TILE == 0 and x.shape[2] == D_MODEL and bsz % GDN_BATCH_TILE == 0
    row = lambda v: v.reshape(1, D_MODEL)

    mod_all = _ada_mod(c, w_ada, b_ada).reshape(n_layers, bsz, N_MOD, D_MODEL)
    x2 = x.reshape(bsz * seq, D_MODEL)
    w_big, w_small, w_small_t = _split_w_in(w_in)
    w_branch_b = w_branch.astype(BF16)
    w_out_b = w_out.astype(BF16)
    w_mlp1_b = w_mlp1.astype(BF16)
    w_mlp2_b = w_mlp2.astype(BF16)
    for l in range(n_layers):
        mod = mod_all[l]
        big, small, small_t = _in_proj(x2, mod, row(g_pre_mix[l]), w_big, w_small, w_small_t,
                                       l, seq)
        oa = _gdn(big, small, small_t, conv_gdn[l], gdn_a_log[l], gdn_dt_bias[l], gdn_norm[l],
                  bsz, seq)
        ob = _lru(big, conv_lru[l], conv_lru_b[l], lru_w_a[l], lru_b_a[l], lru_w_i[l],
                  lru_b_i[l], lru_lambda[l], bsz, seq)
        oc = _gla(big, small, gla_w_gate[l], gla_b_gate[l], gla_norm[l], bsz, seq)
        merged = _merge(big, oa, ob, oc, w_branch_b, l, seq)
        x2 = _out_proj(merged, x2, mod, row(g_post_mix[l]), w_out_b, l, seq)
        x2 = _mlp(x2, mod, row(g_pre_mlp[l]), row(g_post_mlp[l]), w_mlp1_b, w_mlp2_b, l, seq)
    return x2.reshape(bsz, seq, D_MODEL)
```

```python
import jax
import jax.numpy as jnp
from jax import lax
from jax.experimental import pallas as pl
from jax.experimental.pallas import tpu as pltpu

F32 = jnp.float32
BF16 = jnp.bfloat16
HIGHEST = lax.Precision.HIGHEST

D_MODEL = 2048
MIX_WIDTH = D_MODEL // 2
N_BRANCH = 3
GDN_DK = 128
GDN_DV = 128
GDN_HEADS = MIX_WIDTH // GDN_DV
GDN_CHUNK = 64
LRU_WIDTH = MIX_WIDTH
LRU_BLOCKS = 8
LRU_BLOCK_W = LRU_WIDTH // LRU_BLOCKS
LRU_C = 8.0
GLA_HEADS = 4
GLA_DV = MIX_WIDTH // GLA_HEADS
GLA_DK = GLA_DV // 2
GLA_GATE_RANK = 16
GLA_GATE_NORM = 16.0
GLA_BASE_CHUNK = 16
CONV_K = 4
D_FF = 4 * D_MODEL
N_MOD = 6
EPS = 1e-6

GDN_QK_W = GDN_HEADS * GDN_DK
GDN_V_W = GDN_HEADS * GDN_DV
GLA_QK_W = GLA_HEADS * GLA_DK
GLA_V_W = GLA_HEADS * GLA_DV
IN_SPLITS = (GDN_QK_W, GDN_QK_W, GDN_V_W, GDN_V_W, GDN_HEADS, GDN_HEADS,
             LRU_WIDTH, LRU_WIDTH,
             GLA_QK_W, GLA_QK_W, GLA_V_W, GLA_V_W, GLA_GATE_RANK,
             N_BRANCH * D_MODEL)
IN_OFFS = tuple(sum(IN_SPLITS[:i]) for i in range(len(IN_SPLITS) + 1))

LANES = 128
SUBLANES = 8
VMEM_LIMIT_BYTES = 56 * 1024 * 1024

BIG_GDN = 0
BIG_LRU = BIG_GDN + 4 * MIX_WIDTH
BIG_GLA = BIG_LRU + 2 * LRU_WIDTH
BIG_GATES = BIG_GLA + 2 * GLA_QK_W + 2 * GLA_V_W
BIG_W = BIG_GATES + N_BRANCH * D_MODEL
SMALL_BETA = 0
SMALL_ALPHA = GDN_HEADS
SMALL_GK = 2 * GDN_HEADS
SMALL_W = LANES

MOD_SH1, MOD_SC1, MOD_GT1, MOD_SH2, MOD_SC2, MOD_GT2 = range(N_MOD)

TIME_TILE = 128
GDN_BATCH_TILE = 2
GLA_BATCH_TILE = 4
LRU_TIME_TILE = 256
HALO = SUBLANES

W_PREP_TILE = 512
W_PREP_RUN1_TILE = IN_OFFS[4] // W_PREP_TILE
W_PREP_RUN2_TILE = (IN_OFFS[4] + IN_OFFS[12] - IN_OFFS[6]) // W_PREP_TILE
W_PREP_SHIFT1 = IN_OFFS[6] - IN_OFFS[4]
W_PREP_SHIFT2 = W_PREP_SHIFT1 + IN_OFFS[13] - IN_OFFS[12]
W_PREP_NARROW1 = slice(IN_OFFS[4] - W_PREP_RUN1_TILE * W_PREP_TILE,
                       IN_OFFS[6] - W_PREP_RUN1_TILE * W_PREP_TILE)
W_PREP_NARROW2 = slice(IN_OFFS[12] - W_PREP_RUN2_TILE * W_PREP_TILE,
                       IN_OFFS[13] - W_PREP_RUN2_TILE * W_PREP_TILE)
assert (W_PREP_NARROW1.start, W_PREP_NARROW1.stop) == (SMALL_BETA, SMALL_GK)
assert (W_PREP_NARROW2.start, W_PREP_NARROW2.stop) == (SMALL_GK, SMALL_GK + GLA_GATE_RANK)


def _sigmoid(x):
    return 1.0 / (1.0 + jnp.exp(-x))


def _silu(x):
    return x * _sigmoid(x)


def _softplus(x):
    return jnp.maximum(x, 0.0) + jnp.log(1.0 + jnp.exp(-jnp.abs(x)))


def _log_sigmoid(x):
    return -_softplus(-x)


def _gelu_tanh(x):
    c = 0.7978845608028654
    return 0.5 * x * (1.0 + jnp.tanh(c * (x + 0.044715 * (x * x * x))))


def _rms(x):
    return x * lax.rsqrt(jnp.mean(x * x, axis=-1, keepdims=True) + EPS)


def _dot(a, b, precision=None):
    return jnp.dot(a, b, preferred_element_type=F32, precision=precision)


def _dot_nt(a, b):
    return lax.dot_general(a, b, (((1,), (1,)), ((), ())), preferred_element_type=F32)


def _bdot(a, b):
    return _dot(a.astype(BF16), b.astype(BF16))


def _bdot_nt(a, b):
    return _dot_nt(a.astype(BF16), b.astype(BF16))


def _params(*sem):
    return pltpu.CompilerParams(dimension_semantics=sem, vmem_limit_bytes=VMEM_LIMIT_BYTES)


def _ada_kernel(c_ref, w_ref, b_ref, o_ref):
    sc = _silu(c_ref[...]).astype(BF16)
    o_ref[...] = _dot(sc, w_ref[...].astype(BF16)) + b_ref[...]


def _ada_mod(c, w_ada, b_ada):
    n_layers, _, n_out = w_ada.shape
    bsz = c.shape[0]
    tn = 1024
    return pl.pallas_call(
        _ada_kernel,
        out_shape=jax.ShapeDtypeStruct((n_layers, bsz, n_out), F32),
        grid=(n_layers, n_out // tn),
        in_specs=[
            pl.BlockSpec((bsz, D_MODEL), lambda l, j: (0, 0)),
            pl.BlockSpec((None, D_MODEL, tn), lambda l, j: (l, 0, j)),
            pl.BlockSpec((None, 1, tn), lambda l, j: (l, 0, j)),
        ],
        out_specs=pl.BlockSpec((None, bsz, tn), lambda l, j: (l, 0, j)),
        compiler_params=_params("parallel", "parallel"),
        name="ada_mod",
    )(c, w_ada, b_ada.reshape(n_layers, 1, n_out))


def _w_prep_kernel(a_ref, b_ref, o_ref, ws_ref):
    j = pl.program_id(1)

    def shifted(shift):
        row = jnp.concatenate([a_ref[...], b_ref[...]], axis=1)
        return pltpu.roll(row, row.shape[1] - shift, 1)[:, :W_PREP_TILE].astype(BF16)

    @pl.when(j == 0)
    def _():
        ws_ref[...] = jnp.zeros(ws_ref.shape, BF16)

    @pl.when(j < W_PREP_RUN1_TILE)
    def _():
        o_ref[...] = a_ref[...].astype(BF16)

    @pl.when((j >= W_PREP_RUN1_TILE) & (j < W_PREP_RUN2_TILE))
    def _():
        o_ref[...] = shifted(W_PREP_SHIFT1)

    @pl.when(j >= W_PREP_RUN2_TILE)
    def _():
        o_ref[...] = shifted(W_PREP_SHIFT2)

    @pl.when(j == W_PREP_RUN1_TILE)
    def _():
        ws_ref[:, W_PREP_NARROW1] = a_ref[:, W_PREP_NARROW1].astype(BF16)

    @pl.when(j == W_PREP_RUN2_TILE)
    def _():
        ws_ref[:, W_PREP_NARROW2] = a_ref[:, W_PREP_NARROW2].astype(BF16)


def _split_w_in(w_in):
    n_layers = w_in.shape[0]
    lanes_per_tile = W_PREP_TILE // LANES
    w_big, w_small = pl.pallas_call(
        _w_prep_kernel,
        out_shape=(jax.ShapeDtypeStruct((n_layers, D_MODEL, BIG_W), BF16),
                   jax.ShapeDtypeStruct((n_layers, D_MODEL, SMALL_W), BF16)),
        grid=(n_layers, BIG_W // W_PREP_TILE),
        in_specs=[
            pl.BlockSpec((None, D_MODEL, W_PREP_TILE), lambda l, j: (l, 0, j)),
            pl.BlockSpec((None, D_MODEL, LANES), lambda l, j: (l, 0, (j + 1) * lanes_per_tile)),
        ],
        out_specs=(pl.BlockSpec((None, D_MODEL, W_PREP_TILE), lambda l, j: (l, 0, j)),
                   pl.BlockSpec((None, D_MODEL, SMALL_W), lambda l, j: (l, 0, 0))),
        compiler_params=_params("parallel", "arbitrary"),
        name="w_prep",
    )(w_in, w_in)
    return w_big, w_small, jnp.swapaxes(w_small, 1, 2)


def _in_proj_kernel(x_ref, mod_ref, g_ref, w_ref, ws_ref, wst_ref,
                    big_ref, small_ref, smallt_ref, h_ref):
    @pl.when(pl.program_id(1) == 0)
    def _():
        h = _rms(x_ref[...]) * g_ref[...]
        h = h * (1.0 + mod_ref[MOD_SC1:MOD_SC1 + 1, :]) + mod_ref[MOD_SH1:MOD_SH1 + 1, :]
        hb = h.astype(BF16)
        h_ref[...] = hb
        small_ref[...] = _dot(hb, ws_ref[...])
        smallt_ref[...] = _dot_nt(wst_ref[...], hb)

    big_ref[...] = _dot(h_ref[...], w_ref[...])


def _in_proj(x2, mod, g, w_big, w_small, w_small_t, layer, seq):
    m = x2.shape[0]
    tm = min(1024, seq)
    tn = 1024
    tiles_per_seq = seq // tm
    return pl.pallas_call(
        _in_proj_kernel,
        out_shape=(jax.ShapeDtypeStruct((m, BIG_GATES), F32),
                   jax.ShapeDtypeStruct((m, SMALL_W), F32),
                   jax.ShapeDtypeStruct((SMALL_W, m), F32),
                   jax.ShapeDtypeStruct((m, D_MODEL), BF16)),
        grid=(m // tm, BIG_GATES // tn),
        in_specs=[
            pl.BlockSpec((tm, D_MODEL), lambda i, j: (i, 0)),
            pl.BlockSpec((None, N_MOD, D_MODEL), lambda i, j: (i // tiles_per_seq, 0, 0)),
            pl.BlockSpec((1, D_MODEL), lambda i, j: (0, 0)),
            pl.BlockSpec((None, D_MODEL, tn), lambda i, j: (layer, 0, j)),
            pl.BlockSpec((None, D_MODEL, SMALL_W), lambda i, j: (layer, 0, 0)),
            pl.BlockSpec((None, SMALL_W, D_MODEL), lambda i, j: (layer, 0, 0)),
        ],
        out_specs=(pl.BlockSpec((tm, tn), lambda i, j: (i, j)),
                   pl.BlockSpec((tm, SMALL_W), lambda i, j: (i, 0)),
                   pl.BlockSpec((SMALL_W, tm), lambda i, j: (0, i)),
                   pl.BlockSpec((tm, D_MODEL), lambda i, j: (i, 0))),
        compiler_params=_params("parallel", "arbitrary"),
        name="in_proj",
    )(x2, mod, g, w_big, w_small, w_small_t)


def _causal_conv(xbuf, cw_ref, rows):
    acc = cw_ref[CONV_K - 1:CONV_K, :] * xbuf[HALO:HALO + rows, :]
    for j in range(CONV_K - 1):
        off = HALO - (CONV_K - 1) + j
        acc = acc + cw_ref[j:j + 1, :] * xbuf[off:off + rows, :]
    return acc


def _gdn_kernel(*refs):
    nb = GDN_BATCH_TILE
    q_ref, k_ref, v_ref, z_ref, sm_ref = refs[:5]
    smt_refs = refs[5:5 + nb]
    cw_ref, alr_ref, alc_ref, dtr_ref, dtc_ref, nw_ref, o_ref, xbuf, s_scr = refs[5 + nb:]
    rows = TIME_TILE

    @pl.when(pl.program_id(1) == 0)
    def _():
        xbuf[:, 0:HALO, :] = jnp.zeros((nb, HALO, xbuf.shape[2]), F32)
        s_scr[...] = jnp.zeros(s_scr.shape, F32)

    ri = lax.broadcasted_iota(jnp.int32, (rows, rows), 0)
    ci = lax.broadcasted_iota(jnp.int32, (rows, rows), 1)
    same = (ri // GDN_CHUNK) == (ci // GDN_CHUNK)
    causal = same & (ci <= ri)
    strict = same & (ci < ri)
    cum_c = causal.astype(F32)
    cum_r = (same & (ri <= ci)).astype(F32)
    eye = (ri == ci).astype(F32)
    level_masks = []
    s = 1
    while s < GDN_CHUNK:
        level_masks.append((ri // (2 * s) == ci // (2 * s)) & (ri % (2 * s) >= s) & (ci % (2 * s) < s))
        s *= 2
    zeros_half = jnp.zeros((GDN_CHUNK, GDN_DV), F32)
    col_row = lax.broadcasted_iota(jnp.int32, (1, rows), 1)
    n_chunks = rows // GDN_CHUNK

    units = [(bb, h) for bb in range(nb) for h in range(GDN_HEADS)]
    idx = range(len(units))
    qs, kts, gccs, gcrs, a_qks, a_kks, rhss = [], [], [], [], [], [], []
    for bb in range(nb):
        xb = xbuf.at[bb]
        xb[HALO:HALO + rows, 0:GDN_QK_W] = q_ref[bb]
        xb[HALO:HALO + rows, GDN_QK_W:2 * GDN_QK_W] = k_ref[bb]
        xb[HALO:HALO + rows, 2 * GDN_QK_W:2 * GDN_QK_W + GDN_V_W] = v_ref[bb]
        qkv = _silu(_causal_conv(xb, cw_ref, rows))
        xb[0:HALO, :] = xb[rows:rows + HALO, :]

        sm = sm_ref[bb]
        smt = smt_refs[bb][...]
        beta_c = _sigmoid(sm)
        g_c = -jnp.exp(alr_ref[...]) * _softplus(sm + dtr_ref[...])
        g_r = -jnp.exp(alc_ref[...]) * _softplus(smt + dtc_ref[...])
        gc_c = _dot(cum_c, g_c, precision=HIGHEST)
        gc_r = _dot(g_r, cum_r, precision=HIGHEST)

        for h in range(GDN_HEADS):
            qh = qkv[:, h * GDN_DK:(h + 1) * GDN_DK]
            kh = qkv[:, GDN_QK_W + h * GDN_DK:GDN_QK_W + (h + 1) * GDN_DK]
            vh = qkv[:, 2 * GDN_QK_W + h * GDN_DV:2 * GDN_QK_W + (h + 1) * GDN_DV]
            qh = qh * lax.rsqrt(jnp.sum(qh * qh, axis=-1, keepdims=True) + EPS) * (GDN_DK ** -0.5)
            kh = kh * lax.rsqrt(jnp.sum(kh * kh, axis=-1, keepdims=True) + EPS)
            bc = beta_c[:, SMALL_BETA + h:SMALL_BETA + h + 1]
            gcc = gc_c[:, SMALL_ALPHA + h:SMALL_ALPHA + h + 1]
            gcr = gc_r[SMALL_ALPHA + h:SMALL_ALPHA + h + 1, :]
            e = jnp.exp(jnp.where(causal, gcc - gcr, 0.0))
            kb = kh * bc
            kt = kh.T
            qk_kk = _bdot(jnp.concatenate([qh, kb], axis=0), kt)
            eg = jnp.exp(gcc)
            qs.append(qh * eg)
            kts.append(kt)
            gccs.append(gcc)
            gcrs.append(gcr)
            a_qks.append(jnp.where(causal, qk_kk[:rows] * e, 0.0))
            a_kks.append(jnp.where(strict, qk_kk[rows:] * e, 0.0))
            rhss.append(jnp.concatenate([vh * bc, kb * eg], axis=1))

    tinv = [eye - jnp.where(level_masks[0], a_kks[u], 0.0) for u in idx]
    for lm in level_masks[1:]:
        tb = [tinv[u].astype(BF16) for u in idx]
        et = [_dot(jnp.where(lm, a_kks[u], 0.0).astype(BF16), tb[u]) for u in idx]
        tinv = [tinv[u] - _dot(tb[u], et[u].astype(BF16)) for u in idx]
    uws = [_bdot(tinv[u], rhss[u]) for u in idx]

    states = [s_scr[bb, h] for bb, h in units]
    outs = [[] for _ in idx]
    for c in range(n_chunks):
        lo = c * GDN_CHUNK
        hi = lo + GDN_CHUNK
        in_chunk = (col_row >= lo) & (col_row < hi)
        rs = [_bdot(jnp.concatenate([uws[u][lo:hi, GDN_DV:], qs[u][lo:hi]], axis=0), states[u])
              for u in idx]
        v_pads = []
        for u in idx:
            pads = [zeros_half] * n_chunks
            pads[c] = uws[u][lo:hi, :GDN_DV] - rs[u][:GDN_CHUNK]
            v_pads.append(jnp.concatenate(pads, axis=0))
        new_states = []
        for u in idx:
            g_last = gccs[u][hi - 1:hi, :]
            tail = jnp.where(in_chunk, jnp.exp(jnp.where(in_chunk, g_last - gcrs[u], 0.0)), 0.0)
            new_states.append(states[u] * jnp.exp(g_last) + _bdot(kts[u] * tail, v_pads[u]))
        states = new_states
        for u in idx:
            outs[u].append(rs[u][GDN_CHUNK:] + _bdot(a_qks[u][lo:hi, :], v_pads[u]))
    for u, (bb, h) in enumerate(units):
        s_scr[bb, h] = states[u]
        o = _rms(jnp.concatenate(outs[u], axis=0)) * nw_ref[...]
        zg = _silu(z_ref[bb, :, h * GDN_DV:(h + 1) * GDN_DV])
        o_ref[bb, :, h * GDN_DV:(h + 1) * GDN_DV] = (o * zg).astype(BF16)


def _gdn(big, small, small_t, conv_w, a_log, dt_bias, norm_w, bsz, seq):
    nb = GDN_BATCH_TILE
    nt = seq // TIME_TILE
    qkv_w = 2 * GDN_QK_W + GDN_V_W
    big3 = big.reshape(bsz, seq, big.shape[1])
    small3 = small.reshape(bsz, seq, SMALL_W)

    def pad_row(v):
        return jnp.zeros((1, SMALL_W), F32).at[0, SMALL_ALPHA:SMALL_ALPHA + GDN_HEADS].set(v)

    alr = pad_row(a_log)
    dtr = pad_row(dt_bias)
    col = lambda cb: pl.BlockSpec((nb, TIME_TILE, MIX_WIDTH), lambda g, t: (g, t, cb))
    const = lambda shape: pl.BlockSpec(shape, lambda g, t: (0, 0))
    small_t_spec = lambda bb: pl.BlockSpec(
        (SMALL_W, TIME_TILE), lambda g, t: (0, (g * nb + bb) * nt + t))
    out = pl.pallas_call(
        _gdn_kernel,
        out_shape=jax.ShapeDtypeStruct((bsz, seq, MIX_WIDTH), BF16),
        grid=(bsz // nb, nt),
        in_specs=[
            col(0), col(1), col(2), col(3),
            pl.BlockSpec((nb, TIME_TILE, SMALL_W), lambda g, t: (g, t, 0)),
            *[small_t_spec(bb) for bb in range(nb)],
            const((CONV_K, qkv_w)),
            const((1, SMALL_W)), const((SMALL_W, 1)),
            const((1, SMALL_W)), const((SMALL_W, 1)),
            const((1, GDN_DV)),
        ],
        out_specs=pl.BlockSpec((nb, TIME_TILE, MIX_WIDTH), lambda g, t: (g, t, 0)),
        scratch_shapes=[pltpu.VMEM((nb, HALO + TIME_TILE, qkv_w), F32),
                        pltpu.VMEM((nb, GDN_HEADS, GDN_DK, GDN_DV), F32)],
        compiler_params=_params("parallel", "arbitrary"),
        name="gdn",
    )(big3, big3, big3, big3, small3, *([small_t] * nb), conv_w, alr, alr.T, dtr, dtr.T,
      norm_w.reshape(1, GDN_DV))
    return out.reshape(bsz * seq, MIX_WIDTH)


def _lru_kernel(x_ref, y_ref, cw_ref, cb_ref, wa_ref, ba_ref, wi_ref, bi_ref, lam_ref,
                o_ref, xbuf, h_scr):
    rows = LRU_TIME_TILE
    first = pl.program_id(1) == 0

    @pl.when(first)
    def _():
        xbuf[0:HALO, :] = jnp.zeros((HALO, LRU_WIDTH), F32)
        h_scr[...] = jnp.zeros(h_scr.shape, F32)

    xbuf[HALO:HALO + rows, :] = x_ref[...]
    xl = _causal_conv(xbuf, cw_ref, rows) + cb_ref[...]
    xbuf[0:HALO, :] = xbuf[rows:rows + HALO, :]

    xlb = xl.astype(BF16)
    r_parts = []
    i_parts = []
    for k in range(LRU_BLOCKS):
        blk = xlb[:, k * LRU_BLOCK_W:(k + 1) * LRU_BLOCK_W]
        r_parts.append(_dot(blk, wa_ref[k]))
        i_parts.append(_dot(blk, wi_ref[k]))
    r = _sigmoid(jnp.concatenate(r_parts, axis=1) + ba_ref[...])
    ig = _sigmoid(jnp.concatenate(i_parts, axis=1) + bi_ref[...])
    log_a = -LRU_C * r * _softplus(-lam_ref[...])
    a = jnp.exp(log_a)
    mult = jnp.sqrt(-jnp.tanh(log_a) * (a * a + 1.0))
    row = lax.broadcasted_iota(jnp.int32, (rows, 1), 0)
    mult = jnp.where(first & (row == 0), 1.0, mult)
    b = mult * ig * xl

    row8 = row[:SUBLANES]
    shift = 1
    while shift < SUBLANES:
        a_roll = pltpu.roll(a, shift, 0)
        b_roll = pltpu.roll(b, shift, 0)
        keep = row8 >= shift
        a_prev = jnp.concatenate([jnp.where(keep, a_roll[:SUBLANES], 1.0), a_roll[SUBLANES:]], axis=0)
        b_prev = jnp.concatenate([jnp.where(keep, b_roll[:SUBLANES], 0.0), b_roll[SUBLANES:]], axis=0)
        b = a * b_prev + b
        a = a * a_prev
        shift *= 2
    while shift < rows:
        b = jnp.concatenate([b[:shift], a[shift:] * b[:rows - shift] + b[shift:]], axis=0)
        a = jnp.concatenate([a[:shift], a[shift:] * a[:rows - shift]], axis=0)
        shift *= 2
    hs = a * h_scr[0:1, :] + b
    h_scr[0:1, :] = hs[rows - 1:rows, :]
    o_ref[...] = (_gelu_tanh(y_ref[...]) * hs).astype(BF16)


def _lru(big, conv_w, conv_b, w_a, b_a, w_i, b_i, lam, bsz, seq):
    m = bsz * seq
    nt = seq // LRU_TIME_TILE
    base = BIG_LRU // LRU_WIDTH
    row1 = lambda v: v.reshape(1, LRU_WIDTH)
    const2 = lambda shape: pl.BlockSpec(shape, lambda b, t: (0, 0))
    const3 = lambda shape: pl.BlockSpec(shape, lambda b, t: (0, 0, 0))
    return pl.pallas_call(
        _lru_kernel,
        out_shape=jax.ShapeDtypeStruct((m, LRU_WIDTH), BF16),
        grid=(bsz, nt),
        in_specs=[
            pl.BlockSpec((LRU_TIME_TILE, LRU_WIDTH), lambda b, t: (b * nt + t, base)),
            pl.BlockSpec((LRU_TIME_TILE, LRU_WIDTH), lambda b, t: (b * nt + t, base + 1)),
            const2((CONV_K, LRU_WIDTH)), const2((1, LRU_WIDTH)),
            const3((LRU_BLOCKS, LRU_BLOCK_W, LRU_BLOCK_W)), const2((1, LRU_WIDTH)),
            const3((LRU_BLOCKS, LRU_BLOCK_W, LRU_BLOCK_W)), const2((1, LRU_WIDTH)),
            const2((1, LRU_WIDTH)),
        ],
        out_specs=pl.BlockSpec((LRU_TIME_TILE, LRU_WIDTH), lambda b, t: (b * nt + t, 0)),
        scratch_shapes=[pltpu.VMEM((HALO + LRU_TIME_TILE, LRU_WIDTH), F32),
                        pltpu.VMEM((SUBLANES, LRU_WIDTH), F32)],
        compiler_params=_params("parallel", "arbitrary"),
        name="rg_lru",
    )(big, big, conv_w, row1(conv_b), w_a.astype(BF16), row1(b_a), w_i.astype(BF16),
      row1(b_i), row1(lam))


def _gla_kernel(q_ref, k_ref, v_ref, z_ref, sm_ref, wg_ref, bg_ref, nw_ref, o_ref, st_scr):
    rows = TIME_TILE

    @pl.when(pl.program_id(1) == 0)
    def _():
        st_scr[...] = jnp.zeros(st_scr.shape, F32)

    nb = GLA_BATCH_TILE
    ri = lax.broadcasted_iota(jnp.int32, (rows, rows), 0)
    ci = lax.broadcasted_iota(jnp.int32, (rows, rows), 1)
    tri = (ci <= ri).astype(F32)
    bcums = []
    for bb in range(nb):
        log_a = _log_sigmoid(_dot(sm_ref[bb].astype(BF16), wg_ref[...]) + bg_ref[...]) / GLA_GATE_NORM
        bcums.append(_dot(tri, log_a, precision=HIGHEST))
    row = lax.broadcasted_iota(jnp.int32, (rows, 1), 0)

    def boundary_rows(bh, width, offset, first_zero):
        out = jnp.zeros_like(bh) if first_zero else None
        for g in range(rows // width):
            src = g * width + offset
            if src < 0:
                continue
            val = jnp.broadcast_to(bh[src:src + 1, :], bh.shape)
            out = val if out is None else jnp.where(row // width == g, val, out)
        return out

    units = [(bb, h) for bb in range(nb) for h in range(GLA_HEADS)]
    heads = range(len(units))
    bhs = [bcums[bb][:, h * GLA_DK:(h + 1) * GLA_DK] for bb, h in units]
    qh = [q_ref[bb, :, h * GLA_DK:(h + 1) * GLA_DK] * (GLA_DK ** -0.5) for bb, h in units]
    kh = [k_ref[bb, :, h * GLA_DK:(h + 1) * GLA_DK] for bb, h in units]
    vh = [v_ref[bb, :, h * GLA_DV:(h + 1) * GLA_DV] for bb, h in units]
    b_loc = [bhs[h] - boundary_rows(bhs[h], GLA_BASE_CHUNK, -1, True) for h in heads]
    base_mask = (ri // GLA_BASE_CHUNK == ci // GLA_BASE_CHUNK) & (ci <= ri)
    att = [jnp.where(base_mask,
                     _bdot_nt(qh[h] * jnp.exp(b_loc[h]), kh[h] * jnp.exp(-b_loc[h])), 0.0)
           for h in heads]
    s = GLA_BASE_CHUNK
    while s < rows:
        mask = ((ri // (2 * s) == ci // (2 * s)) & (ri % (2 * s) >= s) & (ci % (2 * s) < s))
        refs = [boundary_rows(bhs[h], 2 * s, s - 1, False) for h in heads]
        cross = [_bdot_nt(qh[h] * jnp.exp(jnp.minimum(bhs[h] - refs[h], 0.0)),
                          kh[h] * jnp.exp(jnp.minimum(refs[h] - bhs[h], 0.0))) for h in heads]
        att = [att[h] + jnp.where(mask, cross[h], 0.0) for h in heads]
        s *= 2
    sts = [st_scr[bb, h] for bb, h in units]
    o_inter = [_bdot_nt(qh[u] * jnp.exp(bhs[u]), sts[u]) for u in heads]
    o_intra = [_bdot(att[u], vh[u]) for u in heads]
    b_end = [bhs[u][rows - 1:rows, :] for u in heads]
    upd = [_bdot(vh[u].T, kh[u] * jnp.exp(b_end[u] - bhs[u])) for u in heads]
    for u, (bb, h) in enumerate(units):
        st_scr[bb, h] = sts[u] * jnp.exp(b_end[u]) + upd[u]
        o = _rms(o_intra[u] + o_inter[u]) * nw_ref[...]
        zg = _silu(z_ref[bb, :, h * GLA_DV:(h + 1) * GLA_DV])
        o_ref[bb, :, h * GLA_DV:(h + 1) * GLA_DV] = (o * zg).astype(BF16)


def _gla(big, small, w_gate, b_gate, norm_w, bsz, seq):
    nb = GLA_BATCH_TILE
    nt = seq // TIME_TILE
    big3 = big.reshape(bsz, seq, big.shape[1])
    small3 = small.reshape(bsz, seq, SMALL_W)
    wg = jnp.zeros((SMALL_W, GLA_QK_W), F32).at[SMALL_GK:SMALL_GK + GLA_GATE_RANK].set(w_gate)
    qk_base = BIG_GLA // GLA_QK_W
    v_base = (BIG_GLA + 2 * GLA_QK_W) // GLA_V_W
    const = lambda shape: pl.BlockSpec(shape, lambda g, t: (0, 0))
    tile = lambda width, cb: pl.BlockSpec((nb, TIME_TILE, width), lambda g, t: (g, t, cb))
    out = pl.pallas_call(
        _gla_kernel,
        out_shape=jax.ShapeDtypeStruct((bsz, seq, MIX_WIDTH), BF16),
        grid=(bsz // nb, nt),
        in_specs=[
            tile(GLA_QK_W, qk_base), tile(GLA_QK_W, qk_base + 1),
            tile(GLA_V_W, v_base), tile(GLA_V_W, v_base + 1),
            tile(SMALL_W, 0),
            const((SMALL_W, GLA_QK_W)), const((1, GLA_QK_W)), const((1, GLA_DV)),
        ],
        out_specs=tile(MIX_WIDTH, 0),
        scratch_shapes=[pltpu.VMEM((nb, GLA_HEADS, GLA_DV, GLA_DK), F32)],
        compiler_params=_params("parallel", "arbitrary"),
        name="gla",
    )(big3, big3, big3, big3, small3, wg.astype(BF16), b_gate.reshape(1, GLA_QK_W),
      norm_w.reshape(1, GLA_DV))
    return out.reshape(bsz * seq, MIX_WIDTH)


def _merge_kernel(h_ref, wga_ref, wgb_ref, wgc_ref, oa_ref, ob_ref, oc_ref, wb_ref, o_ref):
    h = h_ref[...]
    acc = _sigmoid(_dot(h, wga_ref[...])) * _dot(oa_ref[...], wb_ref[0])
    acc = acc + _sigmoid(_dot(h, wgb_ref[...])) * _dot(ob_ref[...], wb_ref[1])
    acc = acc + _sigmoid(_dot(h, wgc_ref[...])) * _dot(oc_ref[...], wb_ref[2])
    o_ref[...] = acc.astype(BF16)


def _merge(h, oa, ob, oc, w_big, w_branch, layer, seq):
    m = oa.shape[0]
    tm = min(1024, seq)
    tn = 512
    gate_w = lambda n: pl.BlockSpec(
        (None, D_MODEL, tn), lambda i, j: (layer, 0, (BIG_GATES + n * D_MODEL) // tn + j))
    branch = pl.BlockSpec((tm, MIX_WIDTH), lambda i, j: (i, 0))
    return pl.pallas_call(
        _merge_kernel,
        out_shape=jax.ShapeDtypeStruct((m, D_MODEL), BF16),
        grid=(m // tm, D_MODEL // tn),
        in_specs=[pl.BlockSpec((tm, D_MODEL), lambda i, j: (i, 0)),
                  gate_w(0), gate_w(1), gate_w(2), branch, branch, branch,
                  pl.BlockSpec((None, N_BRANCH, MIX_WIDTH, tn), lambda i, j: (layer, 0, 0, j))],
        out_specs=pl.BlockSpec((tm, tn), lambda i, j: (i, j)),
        compiler_params=_params("parallel", "arbitrary"),
        name="merge",
    )(h, w_big, w_big, w_big, oa, ob, oc, w_branch)


def _out_proj_kernel(m_ref, x_ref, mod_ref, g_ref, w_ref, o_ref):
    y = _dot(m_ref[...], w_ref[...])
    o_ref[...] = x_ref[...] + mod_ref[MOD_GT1:MOD_GT1 + 1, :] * (_rms(y) * g_ref[...])


def _out_proj(merged, x2, mod, g, w_out, layer, seq):
    m = x2.shape[0]
    tm = min(512, seq)
    tiles_per_seq = seq // tm
    return pl.pallas_call(
        _out_proj_kernel,
        out_shape=jax.ShapeDtypeStruct((m, D_MODEL), F32),
        grid=(m // tm,),
        in_specs=[
            pl.BlockSpec((tm, D_MODEL), lambda i: (i, 0)),
            pl.BlockSpec((tm, D_MODEL), lambda i: (i, 0)),
            pl.BlockSpec((None, N_MOD, D_MODEL), lambda i: (i // tiles_per_seq, 0, 0)),
            pl.BlockSpec((1, D_MODEL), lambda i: (0, 0)),
            pl.BlockSpec((None, D_MODEL, D_MODEL), lambda i: (layer, 0, 0)),
        ],
        out_specs=pl.BlockSpec((tm, D_MODEL), lambda i: (i, 0)),
        compiler_params=_params("parallel"),
        name="out_proj",
    )(merged, x2, mod, g, w_out)


def _mlp_kernel(x_ref, mod_ref, gpre_ref, gpost_ref, w1_ref, w2_ref, o_ref, h_scr, acc_scr):
    f = pl.program_id(1)

    @pl.when(f == 0)
    def _():
        h = _rms(x_ref[...]) * gpre_ref[...]
        h = h * (1.0 + mod_ref[MOD_SC2:MOD_SC2 + 1, :]) + mod_ref[MOD_SH2:MOD_SH2 + 1, :]
        h_scr[...] = h.astype(BF16)
        acc_scr[...] = jnp.zeros(acc_scr.shape, F32)

    a = jnp.maximum(_dot(h_scr[...], w1_ref[...]), 0.0)
    acc_scr[...] += _dot((a * a).astype(BF16), w2_ref[...])

    @pl.when(f == pl.num_programs(1) - 1)
    def _():
        y = _rms(acc_scr[...]) * gpost_ref[...]
        o_ref[...] = x_ref[...] + mod_ref[MOD_GT2:MOD_GT2 + 1, :] * y


def _mlp(x2, mod, g_pre, g_post, w1, w2, layer, seq):
    m = x2.shape[0]
    tm = min(512, seq)
    tf = 1024
    tiles_per_seq = seq // tm
    return pl.pallas_call(
        _mlp_kernel,
        out_shape=jax.ShapeDtypeStruct((m, D_MODEL), F32),
        grid=(m // tm, D_FF // tf),
        in_specs=[
            pl.BlockSpec((tm, D_MODEL), lambda i, f: (i, 0)),
            pl.BlockSpec((None, N_MOD, D_MODEL), lambda i, f: (i // tiles_per_seq, 0, 0)),
            pl.BlockSpec((1, D_MODEL), lambda i, f: (0, 0)),
            pl.BlockSpec((1, D_MODEL), lambda i, f: (0, 0)),
            pl.BlockSpec((None, D_MODEL, tf), lambda i, f: (layer, 0, f)),
            pl.BlockSpec((None, tf, D_MODEL), lambda i, f: (layer, f, 0)),
        ],
        out_specs=pl.BlockSpec((tm, D_MODEL), lambda i, f: (i, 0)),
        scratch_shapes=[pltpu.VMEM((tm, D_MODEL), BF16), pltpu.VMEM((tm, D_MODEL), F32)],
        compiler_params=_params("parallel", "arbitrary"),
        name="mlp",
    )(x2, mod, g_pre, g_post, w1, w2)


def kernel(x, c, w_ada, b_ada, g_pre_mix, g_post_mix, g_pre_mlp, g_post_mlp, w_in, conv_gdn, gdn_a_log, gdn_dt_bias, gdn_norm, conv_lru, conv_lru_b, lru_w_a, lru_b_a, lru_w_i, lru_b_i, lru_lambda, gla_w_gate, gla_b_gate, gla_norm, w_branch, w_out, w_mlp1, w_mlp2):
    bsz, seq, _ = x.shape
    n_layers = w_in.shape[0]
    assert seq % LRU_TIME_TILE == 0 and x.shape[2] == D_MODEL and bsz % GDN_BATCH_TILE == 0
    row = lambda v: v.reshape(1, D_MODEL)

    mod_all = _ada_mod(c, w_ada, b_ada).reshape(n_layers, bsz, N_MOD, D_MODEL)
    x2 = x.reshape(bsz * seq, D_MODEL)
    w_big, w_small, w_small_t = _split_w_in(w_in)
    w_branch_b = w_branch.astype(BF16)
    w_out_b = w_out.astype(BF16)
    w_mlp1_b = w_mlp1.astype(BF16)
    w_mlp2_b = w_mlp2.astype(BF16)
    for l in range(n_layers):
        mod = mod_all[l]
        big, small, small_t, h = _in_proj(x2, mod, row(g_pre_mix[l]), w_big, w_small, w_small_t,
                                          l, seq)
        oa = _gdn(big, small, small_t, conv_gdn[l], gdn_a_log[l], gdn_dt_bias[l], gdn_norm[l],
                  bsz, seq)
        ob = _lru(big, conv_lru[l], conv_lru_b[l], lru_w_a[l], lru_b_a[l], lru_w_i[l],
                  lru_b_i[l], lru_lambda[l], bsz, seq)
        oc = _gla(big, small, gla_w_gate[l], gla_b_gate[l], gla_norm[l], bsz, seq)
        merged = _merge(h, oa, ob, oc, w_big, w_branch_b, l, seq)
        x2 = _out_proj(merged, x2, mod, row(g_post_mix[l]), w_out_b, l, seq)
        x2 = _mlp(x2, mod, row(g_pre_mlp[l]), row(g_post_mlp[l]), w_mlp1_b, w_mlp2_b, l, seq)
    return x2.reshape(bsz, seq, D_MODEL)
```

```python
import functools

import jax
import jax.numpy as jnp
from jax import lax
from jax.experimental import pallas as pl
from jax.experimental.pallas import tpu as pltpu

F32 = jnp.float32
BF16 = jnp.bfloat16
HIGHEST = lax.Precision.HIGHEST

D_MODEL = 2048
MIX_WIDTH = D_MODEL // 2
N_BRANCH = 3
GDN_DK = 128
GDN_DV = 128
GDN_HEADS = MIX_WIDTH // GDN_DV
GDN_CHUNK = 64
LRU_WIDTH = MIX_WIDTH
LRU_BLOCKS = 8
LRU_BLOCK_W = LRU_WIDTH // LRU_BLOCKS
LRU_C = 8.0
GLA_HEADS = 4
GLA_DV = MIX_WIDTH // GLA_HEADS
GLA_DK = GLA_DV // 2
GLA_GATE_RANK = 16
GLA_GATE_NORM = 16.0
GLA_BASE_CHUNK = 16
CONV_K = 4
D_FF = 4 * D_MODEL
N_MOD = 6
EPS = 1e-6

GDN_QK_W = GDN_HEADS * GDN_DK
GDN_V_W = GDN_HEADS * GDN_DV
GLA_QK_W = GLA_HEADS * GLA_DK
GLA_V_W = GLA_HEADS * GLA_DV
IN_SPLITS = (GDN_QK_W, GDN_QK_W, GDN_V_W, GDN_V_W, GDN_HEADS, GDN_HEADS,
             LRU_WIDTH, LRU_WIDTH,
             GLA_QK_W, GLA_QK_W, GLA_V_W, GLA_V_W, GLA_GATE_RANK,
             N_BRANCH * D_MODEL)
IN_OFFS = tuple(sum(IN_SPLITS[:i]) for i in range(len(IN_SPLITS) + 1))

LANES = 128
SUBLANES = 8
VMEM_LIMIT_BYTES = 56 * 1024 * 1024

BIG_GDN = 0
BIG_LRU = BIG_GDN + 4 * MIX_WIDTH
BIG_GLA = BIG_LRU + 2 * LRU_WIDTH
BIG_GATES = BIG_GLA + 2 * GLA_QK_W + 2 * GLA_V_W
BIG_W = BIG_GATES + N_BRANCH * D_MODEL
SMALL_BETA = 0
SMALL_ALPHA = GDN_HEADS
SMALL_GK = 2 * GDN_HEADS
SMALL_W = LANES

MOD_SH1, MOD_SC1, MOD_GT1, MOD_SH2, MOD_SC2, MOD_GT2 = range(N_MOD)

TIME_TILE = 128
GDN_BATCH_TILE = 2
GLA_BATCH_TILE = 4
LRU_TIME_TILE = 256
HALO = SUBLANES

W_PREP_TILE = 512
W_PREP_RUN1_TILE = IN_OFFS[4] // W_PREP_TILE
W_PREP_RUN2_TILE = (IN_OFFS[4] + IN_OFFS[12] - IN_OFFS[6]) // W_PREP_TILE
W_PREP_SHIFT1 = IN_OFFS[6] - IN_OFFS[4]
W_PREP_SHIFT2 = W_PREP_SHIFT1 + IN_OFFS[13] - IN_OFFS[12]
W_PREP_NEXT = W_PREP_SHIFT2
W_PREP_NARROW1 = slice(IN_OFFS[4] - W_PREP_RUN1_TILE * W_PREP_TILE,
                       IN_OFFS[6] - W_PREP_RUN1_TILE * W_PREP_TILE)
W_PREP_NARROW2 = slice(IN_OFFS[12] - W_PREP_RUN2_TILE * W_PREP_TILE,
                       IN_OFFS[13] - W_PREP_RUN2_TILE * W_PREP_TILE)
assert (W_PREP_NARROW1.start, W_PREP_NARROW1.stop) == (SMALL_BETA, SMALL_GK)
assert (W_PREP_NARROW2.start, W_PREP_NARROW2.stop) == (SMALL_GK, SMALL_GK + GLA_GATE_RANK)
assert W_PREP_TILE % W_PREP_NEXT == 0 and IN_OFFS[-1] % W_PREP_NEXT == 0


def _sigmoid(x):
    return 1.0 / (1.0 + jnp.exp(-x))


def _silu(x):
    return x * _sigmoid(x)


def _softplus(x):
    return jnp.maximum(x, 0.0) + jnp.log(1.0 + jnp.exp(-jnp.abs(x)))


def _log_sigmoid(x):
    return -_softplus(-x)


def _gelu_tanh(x):
    c = 0.7978845608028654
    return 0.5 * x * (1.0 + jnp.tanh(c * (x + 0.044715 * (x * x * x))))


def _rms(x):
    return x * lax.rsqrt(jnp.mean(x * x, axis=-1, keepdims=True) + EPS)


def _dot(a, b, precision=None):
    return jnp.dot(a, b, preferred_element_type=F32, precision=precision)


def _dot_nt(a, b):
    return lax.dot_general(a, b, (((1,), (1,)), ((), ())), preferred_element_type=F32)


def _bdot(a, b):
    return _dot(a.astype(BF16), b.astype(BF16))


def _bdot_nt(a, b):
    return _dot_nt(a.astype(BF16), b.astype(BF16))


def _params(*sem):
    return pltpu.CompilerParams(dimension_semantics=sem, vmem_limit_bytes=VMEM_LIMIT_BYTES)


def _ada_kernel(c_ref, w_ref, b_ref, o_ref):
    sc = _silu(c_ref[...]).astype(BF16)
    o_ref[...] = _dot(sc, w_ref[...].astype(BF16)) + b_ref[...]


def _ada_mod(c, w_ada, b_ada):
    n_layers, _, n_out = w_ada.shape
    bsz = c.shape[0]
    tn = 1024
    return pl.pallas_call(
        _ada_kernel,
        out_shape=jax.ShapeDtypeStruct((n_layers, bsz, n_out), F32),
        grid=(n_layers, n_out // tn),
        in_specs=[
            pl.BlockSpec((bsz, D_MODEL), lambda l, j: (0, 0)),
            pl.BlockSpec((None, D_MODEL, tn), lambda l, j: (l, 0, j)),
            pl.BlockSpec((None, 1, tn), lambda l, j: (l, 0, j)),
        ],
        out_specs=pl.BlockSpec((None, bsz, tn), lambda l, j: (l, 0, j)),
        compiler_params=_params("parallel", "parallel"),
        name="ada_mod",
    )(c, w_ada, b_ada.reshape(n_layers, 1, n_out))


def _w_prep_kernel(a_ref, b_ref, o_ref, wst_ref):
    j = pl.program_id(1)

    def emit(shift):
        src = a_ref[...] if shift == 0 else jnp.concatenate([a_ref[shift:, :], b_ref[:shift, :]],
                                                            axis=0)
        o_ref[...] = src.T.astype(BF16)

    @pl.when(j == 0)
    def _():
        wst_ref[...] = jnp.zeros(wst_ref.shape, BF16)

    @pl.when(j < W_PREP_RUN1_TILE)
    def _():
        emit(0)

    @pl.when((j >= W_PREP_RUN1_TILE) & (j < W_PREP_RUN2_TILE))
    def _():
        emit(W_PREP_SHIFT1)

    @pl.when(j >= W_PREP_RUN2_TILE)
    def _():
        emit(W_PREP_SHIFT2)

    @pl.when(j == W_PREP_RUN1_TILE)
    def _():
        wst_ref[W_PREP_NARROW1, :] = a_ref[W_PREP_NARROW1, :].astype(BF16)

    @pl.when(j == W_PREP_RUN2_TILE)
    def _():
        wst_ref[W_PREP_NARROW2, :] = a_ref[W_PREP_NARROW2, :].astype(BF16)


def _split_w_in(w_in):
    n_layers = w_in.shape[0]
    w_t = jnp.swapaxes(w_in, 1, 2)
    next_per_tile = W_PREP_TILE // W_PREP_NEXT
    w_big, w_small_t = pl.pallas_call(
        _w_prep_kernel,
        out_shape=(jax.ShapeDtypeStruct((n_layers, D_MODEL, BIG_W), BF16),
                   jax.ShapeDtypeStruct((n_layers, SMALL_W, D_MODEL), BF16)),
        grid=(n_layers, BIG_W // W_PREP_TILE),
        in_specs=[
            pl.BlockSpec((None, W_PREP_TILE, D_MODEL), lambda l, j: (l, j, 0)),
            pl.BlockSpec((None, W_PREP_NEXT, D_MODEL), lambda l, j: (l, (j + 1) * next_per_tile, 0)),
        ],
        out_specs=(pl.BlockSpec((None, D_MODEL, W_PREP_TILE), lambda l, j: (l, 0, j)),
                   pl.BlockSpec((None, SMALL_W, D_MODEL), lambda l, j: (l, 0, 0))),
        compiler_params=_params("parallel", "arbitrary"),
        name="w_prep",
    )(w_t, w_t)
    return w_big, w_small_t


def _in_proj_kernel(xn_ref, modn_ref, g_ref, w_ref, wst_ref,
                    big_ref, small_ref, smallt_ref, h_ref, h_scr, *, chunks_per_tile):
    r = pl.program_id(0)
    j = pl.program_id(1)
    chunk = xn_ref.shape[0]

    def build_chunk():
        h = _rms(xn_ref[...]) * g_ref[...]
        h = h * (1.0 + modn_ref[MOD_SC1:MOD_SC1 + 1, :]) + modn_ref[MOD_SH1:MOD_SH1 + 1, :]
        row0 = pl.multiple_of(jnp.minimum(j, chunks_per_tile - 1) * chunk, chunk)
        h_scr[r % 2, pl.ds(row0, chunk), :] = h.astype(BF16)

    @pl.when(r == 0)
    def _():
        build_chunk()

    @pl.when(r > 0)
    def _():
        prev = (r + 1) % 2

        @pl.when(j == 0)
        def _():
            hb = h_scr[prev]
            h_ref[...] = hb
            st = _dot_nt(wst_ref[...], hb)
            smallt_ref[...] = st
            small_ref[...] = st.T

        big_ref[...] = _dot(h_scr[prev], w_ref[...])
        build_chunk()


def _in_proj(x2, mod, g, w_big, w_small_t, layer, seq):
    m = x2.shape[0]
    tm = min(1024, seq)
    tn = 1024
    n_i = m // tm
    n_j = BIG_GATES // tn
    chunks_per_tile = min(n_j, SUBLANES)
    chunk = tm // chunks_per_tile
    tiles_per_seq = seq // tm
    build = lambda r: jnp.minimum(r, n_i - 1)
    out = lambda r: jnp.maximum(r - 1, 0)
    return pl.pallas_call(
        functools.partial(_in_proj_kernel, chunks_per_tile=chunks_per_tile),
        out_shape=(jax.ShapeDtypeStruct((m, BIG_GATES), F32),
                   jax.ShapeDtypeStruct((m, SMALL_W), F32),
                   jax.ShapeDtypeStruct((SMALL_W, m), F32),
                   jax.ShapeDtypeStruct((m, D_MODEL), BF16)),
        grid=(n_i + 1, n_j),
        in_specs=[
            pl.BlockSpec((chunk, D_MODEL), lambda r, j: (
                build(r) * chunks_per_tile + jnp.minimum(j, chunks_per_tile - 1), 0)),
            pl.BlockSpec((None, N_MOD, D_MODEL), lambda r, j: (build(r) // tiles_per_seq, 0, 0)),
            pl.BlockSpec((1, D_MODEL), lambda r, j: (0, 0)),
            pl.BlockSpec((None, D_MODEL, tn), lambda r, j: (layer, 0, j)),
            pl.BlockSpec((None, SMALL_W, D_MODEL), lambda r, j: (layer, 0, 0)),
        ],
        out_specs=(pl.BlockSpec((tm, tn), lambda r, j: (out(r), jnp.where(r == 0, 0, j))),
                   pl.BlockSpec((tm, SMALL_W), lambda r, j: (out(r), 0)),
                   pl.BlockSpec((SMALL_W, tm), lambda r, j: (0, out(r))),
                   pl.BlockSpec((tm, D_MODEL), lambda r, j: (out(r), 0))),
        scratch_shapes=[pltpu.VMEM((2, tm, D_MODEL), BF16)],
        compiler_params=_params("arbitrary", "arbitrary"),
        name="in_proj",
    )(x2, mod, g, w_big, w_small_t)


def _causal_conv(xbuf, cw_ref, rows):
    acc = cw_ref[CONV_K - 1:CONV_K, :] * xbuf[HALO:HALO + rows, :]
    for j in range(CONV_K - 1):
        off = HALO - (CONV_K - 1) + j
        acc = acc + cw_ref[j:j + 1, :] * xbuf[off:off + rows, :]
    return acc


def _gdn_kernel(*refs):
    nb = GDN_BATCH_TILE
    q_ref, k_ref, v_ref, z_ref, sm_ref = refs[:5]
    smt_refs = refs[5:5 + nb]
    cw_ref, alr_ref, alc_ref, dtr_ref, dtc_ref, nw_ref, o_ref, xbuf, s_scr = refs[5 + nb:]
    rows = TIME_TILE

    @pl.when(pl.program_id(1) == 0)
    def _():
        xbuf[:, 0:HALO, :] = jnp.zeros((nb, HALO, xbuf.shape[2]), F32)
        s_scr[...] = jnp.zeros(s_scr.shape, F32)

    ri = lax.broadcasted_iota(jnp.int32, (rows, rows), 0)
    ci = lax.broadcasted_iota(jnp.int32, (rows, rows), 1)
    same = (ri // GDN_CHUNK) == (ci // GDN_CHUNK)
    causal = same & (ci <= ri)
    strict = same & (ci < ri)
    cum_c = causal.astype(F32)
    cum_r = (same & (ri <= ci)).astype(F32)
    eye = (ri == ci).astype(F32)
    level_masks = []
    s = 1
    while s < GDN_CHUNK:
        level_masks.append((ri // (2 * s) == ci // (2 * s)) & (ri % (2 * s) >= s) & (ci % (2 * s) < s))
        s *= 2
    zeros_half = jnp.zeros((GDN_CHUNK, GDN_DV), F32)
    col_row = lax.broadcasted_iota(jnp.int32, (1, rows), 1)
    n_chunks = rows // GDN_CHUNK

    units = [(bb, h) for bb in range(nb) for h in range(GDN_HEADS)]
    idx = range(len(units))
    qs, kts, gccs, gcrs, a_qks, a_kks, rhss = [], [], [], [], [], [], []
    for bb in range(nb):
        xb = xbuf.at[bb]
        xb[HALO:HALO + rows, 0:GDN_QK_W] = q_ref[bb]
        xb[HALO:HALO + rows, GDN_QK_W:2 * GDN_QK_W] = k_ref[bb]
        xb[HALO:HALO + rows, 2 * GDN_QK_W:2 * GDN_QK_W + GDN_V_W] = v_ref[bb]
        qkv = _silu(_causal_conv(xb, cw_ref, rows))
        xb[0:HALO, :] = xb[rows:rows + HALO, :]

        sm = sm_ref[bb]
        smt = smt_refs[bb][...]
        beta_c = _sigmoid(sm)
        g_c = -jnp.exp(alr_ref[...]) * _softplus(sm + dtr_ref[...])
        g_r = -jnp.exp(alc_ref[...]) * _softplus(smt + dtc_ref[...])
        gc_c = _dot(cum_c, g_c, precision=HIGHEST)
        gc_r = _dot(g_r, cum_r, precision=HIGHEST)

        for h in range(GDN_HEADS):
            qh = qkv[:, h * GDN_DK:(h + 1) * GDN_DK]
            kh = qkv[:, GDN_QK_W + h * GDN_DK:GDN_QK_W + (h + 1) * GDN_DK]
            vh = qkv[:, 2 * GDN_QK_W + h * GDN_DV:2 * GDN_QK_W + (h + 1) * GDN_DV]
            qh = qh * lax.rsqrt(jnp.sum(qh * qh, axis=-1, keepdims=True) + EPS) * (GDN_DK ** -0.5)
            kh = kh * lax.rsqrt(jnp.sum(kh * kh, axis=-1, keepdims=True) + EPS)
            bc = beta_c[:, SMALL_BETA + h:SMALL_BETA + h + 1]
            gcc = gc_c[:, SMALL_ALPHA + h:SMALL_ALPHA + h + 1]
            gcr = gc_r[SMALL_ALPHA + h:SMALL_ALPHA + h + 1, :]
            e = jnp.exp(jnp.where(causal, gcc - gcr, 0.0))
            kb = kh * bc
            kt = kh.T
            qk_kk = _bdot(jnp.concatenate([qh, kb], axis=0), kt)
            eg = jnp.exp(gcc)
            qs.append(qh * eg)
            kts.append(kt)
            gccs.append(gcc)
            gcrs.append(gcr)
            a_qks.append(jnp.where(causal, qk_kk[:rows] * e, 0.0))
            a_kks.append(jnp.where(strict, qk_kk[rows:] * e, 0.0))
            rhss.append(jnp.concatenate([vh * bc, kb * eg], axis=1))

    tinv = [eye - jnp.where(level_masks[0], a_kks[u], 0.0) for u in idx]
    for lm in level_masks[1:]:
        tb = [tinv[u].astype(BF16) for u in idx]
        et = [_dot(jnp.where(lm, a_kks[u], 0.0).astype(BF16), tb[u]) for u in idx]
        tinv = [tinv[u] - _dot(tb[u], et[u].astype(BF16)) for u in idx]
    uws = [_bdot(tinv[u], rhss[u]) for u in idx]

    states = [s_scr[bb, h] for bb, h in units]
    outs = [[] for _ in idx]
    for c in range(n_chunks):
        lo = c * GDN_CHUNK
        hi = lo + GDN_CHUNK
        in_chunk = (col_row >= lo) & (col_row < hi)
        rs = [_bdot(jnp.concatenate([uws[u][lo:hi, GDN_DV:], qs[u][lo:hi]], axis=0), states[u])
              for u in idx]
        v_pads = []
        for u in idx:
            pads = [zeros_half] * n_chunks
            pads[c] = uws[u][lo:hi, :GDN_DV] - rs[u][:GDN_CHUNK]
            v_pads.append(jnp.concatenate(pads, axis=0))
        new_states = []
        for u in idx:
            g_last = gccs[u][hi - 1:hi, :]
            tail = jnp.where(in_chunk, jnp.exp(jnp.where(in_chunk, g_last - gcrs[u], 0.0)), 0.0)
            new_states.append(states[u] * jnp.exp(g_last) + _bdot(kts[u] * tail, v_pads[u]))
        states = new_states
        for u in idx:
            outs[u].append(rs[u][GDN_CHUNK:] + _bdot(a_qks[u][lo:hi, :], v_pads[u]))
    for u, (bb, h) in enumerate(units):
        s_scr[bb, h] = states[u]
        o = _rms(jnp.concatenate(outs[u], axis=0)) * nw_ref[...]
        zg = _silu(z_ref[bb, :, h * GDN_DV:(h + 1) * GDN_DV])
        o_ref[bb, :, h * GDN_DV:(h + 1) * GDN_DV] = (o * zg).astype(BF16)


def _gdn(big, small, small_t, conv_w, a_log, dt_bias, norm_w, bsz, seq):
    nb = GDN_BATCH_TILE
    nt = seq // TIME_TILE
    qkv_w = 2 * GDN_QK_W + GDN_V_W
    big3 = big.reshape(bsz, seq, big.shape[1])
    small3 = small.reshape(bsz, seq, SMALL_W)

    def pad_row(v):
        return jnp.zeros((1, SMALL_W), F32).at[0, SMALL_ALPHA:SMALL_ALPHA + GDN_HEADS].set(v)

    alr = pad_row(a_log)
    dtr = pad_row(dt_bias)
    col = lambda cb: pl.BlockSpec((nb, TIME_TILE, MIX_WIDTH), lambda g, t: (g, t, cb))
    const = lambda shape: pl.BlockSpec(shape, lambda g, t: (0, 0))
    small_t_spec = lambda bb: pl.BlockSpec(
        (SMALL_W, TIME_TILE), lambda g, t: (0, (g * nb + bb) * nt + t))
    out = pl.pallas_call(
        _gdn_kernel,
        out_shape=jax.ShapeDtypeStruct((bsz, seq, MIX_WIDTH), BF16),
        grid=(bsz // nb, nt),
        in_specs=[
            col(0), col(1), col(2), col(3),
            pl.BlockSpec((nb, TIME_TILE, SMALL_W), lambda g, t: (g, t, 0)),
            *[small_t_spec(bb) for bb in range(nb)],
            const((CONV_K, qkv_w)),
            const((1, SMALL_W)), const((SMALL_W, 1)),
            const((1, SMALL_W)), const((SMALL_W, 1)),
            const((1, GDN_DV)),
        ],
        out_specs=pl.BlockSpec((nb, TIME_TILE, MIX_WIDTH), lambda g, t: (g, t, 0)),
        scratch_shapes=[pltpu.VMEM((nb, HALO + TIME_TILE, qkv_w), F32),
                        pltpu.VMEM((nb, GDN_HEADS, GDN_DK, GDN_DV), F32)],
        compiler_params=_params("parallel", "arbitrary"),
        name="gdn",
    )(big3, big3, big3, big3, small3, *([small_t] * nb), conv_w, alr, alr.T, dtr, dtr.T,
      norm_w.reshape(1, GDN_DV))
    return out.reshape(bsz * seq, MIX_WIDTH)


def _lru_kernel(x_ref, y_ref, cw_ref, cb_ref, wa_ref, ba_ref, wi_ref, bi_ref, lam_ref,
                o_ref, xbuf, h_scr):
    rows = LRU_TIME_TILE
    first = pl.program_id(1) == 0

    @pl.when(first)
    def _():
        xbuf[0:HALO, :] = jnp.zeros((HALO, LRU_WIDTH), F32)
        h_scr[...] = jnp.zeros(h_scr.shape, F32)

    xbuf[HALO:HALO + rows, :] = x_ref[...]
    xl = _causal_conv(xbuf, cw_ref, rows) + cb_ref[...]
    xbuf[0:HALO, :] = xbuf[rows:rows + HALO, :]

    xlb = xl.astype(BF16)
    r_parts = []
    i_parts = []
    for k in range(LRU_BLOCKS):
        blk = xlb[:, k * LRU_BLOCK_W:(k + 1) * LRU_BLOCK_W]
        r_parts.append(_dot(blk, wa_ref[k]))
        i_parts.append(_dot(blk, wi_ref[k]))
    r = _sigmoid(jnp.concatenate(r_parts, axis=1) + ba_ref[...])
    ig = _sigmoid(jnp.concatenate(i_parts, axis=1) + bi_ref[...])
    log_a = -LRU_C * r * _softplus(-lam_ref[...])
    a = jnp.exp(log_a)
    mult = jnp.sqrt(-jnp.tanh(log_a) * (a * a + 1.0))
    row = lax.broadcasted_iota(jnp.int32, (rows, 1), 0)
    mult = jnp.where(first & (row == 0), 1.0, mult)
    b = mult * ig * xl

    row8 = row[:SUBLANES]
    shift = 1
    while shift < SUBLANES:
        a_roll = pltpu.roll(a, shift, 0)
        b_roll = pltpu.roll(b, shift, 0)
        keep = row8 >= shift
        a_prev = jnp.concatenate([jnp.where(keep, a_roll[:SUBLANES], 1.0), a_roll[SUBLANES:]], axis=0)
        b_prev = jnp.concatenate([jnp.where(keep, b_roll[:SUBLANES], 0.0), b_roll[SUBLANES:]], axis=0)
        b = a * b_prev + b
        a = a * a_prev
        shift *= 2
    while shift < rows:
        b = jnp.concatenate([b[:shift], a[shift:] * b[:rows - shift] + b[shift:]], axis=0)
        a = jnp.concatenate([a[:shift], a[shift:] * a[:rows - shift]], axis=0)
        shift *= 2
    hs = a * h_scr[0:1, :] + b
    h_scr[0:1, :] = hs[rows - 1:rows, :]
    o_ref[...] = (_gelu_tanh(y_ref[...]) * hs).astype(BF16)


def _lru(big, conv_w, conv_b, w_a, b_a, w_i, b_i, lam, bsz, seq):
    m = bsz * seq
    nt = seq // LRU_TIME_TILE
    base = BIG_LRU // LRU_WIDTH
    row1 = lambda v: v.reshape(1, LRU_WIDTH)
    const2 = lambda shape: pl.BlockSpec(shape, lambda b, t: (0, 0))
    const3 = lambda shape: pl.BlockSpec(shape, lambda b, t: (0, 0, 0))
    return pl.pallas_call(
        _lru_kernel,
        out_shape=jax.ShapeDtypeStruct((m, LRU_WIDTH), BF16),
        grid=(bsz, nt),
        in_specs=[
            pl.BlockSpec((LRU_TIME_TILE, LRU_WIDTH), lambda b, t: (b * nt + t, base)),
            pl.BlockSpec((LRU_TIME_TILE, LRU_WIDTH), lambda b, t: (b * nt + t, base + 1)),
            const2((CONV_K, LRU_WIDTH)), const2((1, LRU_WIDTH)),
            const3((LRU_BLOCKS, LRU_BLOCK_W, LRU_BLOCK_W)), const2((1, LRU_WIDTH)),
            const3((LRU_BLOCKS, LRU_BLOCK_W, LRU_BLOCK_W)), const2((1, LRU_WIDTH)),
            const2((1, LRU_WIDTH)),
        ],
        out_specs=pl.BlockSpec((LRU_TIME_TILE, LRU_WIDTH), lambda b, t: (b * nt + t, 0)),
        scratch_shapes=[pltpu.VMEM((HALO + LRU_TIME_TILE, LRU_WIDTH), F32),
                        pltpu.VMEM((SUBLANES, LRU_WIDTH), F32)],
        compiler_params=_params("parallel", "arbitrary"),
        name="rg_lru",
    )(big, big, conv_w, row1(conv_b), w_a.astype(BF16), row1(b_a), w_i.astype(BF16),
      row1(b_i), row1(lam))


def _gla_kernel(q_ref, k_ref, v_ref, z_ref, sm_ref, wg_ref, bg_ref, nw_ref, o_ref, st_scr):
    rows = TIME_TILE

    @pl.when(pl.program_id(1) == 0)
    def _():
        st_scr[...] = jnp.zeros(st_scr.shape, F32)

    nb = GLA_BATCH_TILE
    ri = lax.broadcasted_iota(jnp.int32, (rows, rows), 0)
    ci = lax.broadcasted_iota(jnp.int32, (rows, rows), 1)
    tri = (ci <= ri).astype(F32)
    bcums = []
    for bb in range(nb):
        log_a = _log_sigmoid(_dot(sm_ref[bb].astype(BF16), wg_ref[...]) + bg_ref[...]) / GLA_GATE_NORM
        bcums.append(_dot(tri, log_a, precision=HIGHEST))
    row = lax.broadcasted_iota(jnp.int32, (rows, 1), 0)

    def boundary_rows(bh, width, offset, first_zero):
        out = jnp.zeros_like(bh) if first_zero else None
        for g in range(rows // width):
            src = g * width + offset
            if src < 0:
                continue
            val = jnp.broadcast_to(bh[src:src + 1, :], bh.shape)
            out = val if out is None else jnp.where(row // width == g, val, out)
        return out

    units = [(bb, h) for bb in range(nb) for h in range(GLA_HEADS)]
    heads = range(len(units))
    bhs = [bcums[bb][:, h * GLA_DK:(h + 1) * GLA_DK] for bb, h in units]
    qh = [q_ref[bb, :, h * GLA_DK:(h + 1) * GLA_DK] * (GLA_DK ** -0.5) for bb, h in units]
    kh = [k_ref[bb, :, h * GLA_DK:(h + 1) * GLA_DK] for bb, h in units]
    vh = [v_ref[bb, :, h * GLA_DV:(h + 1) * GLA_DV] for bb, h in units]
    b_loc = [bhs[h] - boundary_rows(bhs[h], GLA_BASE_CHUNK, -1, True) for h in heads]
    base_mask = (ri // GLA_BASE_CHUNK == ci // GLA_BASE_CHUNK) & (ci <= ri)
    att = [jnp.where(base_mask,
                     _bdot_nt(qh[h] * jnp.exp(b_loc[h]), kh[h] * jnp.exp(-b_loc[h])), 0.0)
           for h in heads]
    s = GLA_BASE_CHUNK
    while s < rows:
        mask = ((ri // (2 * s) == ci // (2 * s)) & (ri % (2 * s) >= s) & (ci % (2 * s) < s))
        refs = [boundary_rows(bhs[h], 2 * s, s - 1, False) for h in heads]
        cross = [_bdot_nt(qh[h] * jnp.exp(jnp.minimum(bhs[h] - refs[h], 0.0)),
                          kh[h] * jnp.exp(jnp.minimum(refs[h] - bhs[h], 0.0))) for h in heads]
        att = [att[h] + jnp.where(mask, cross[h], 0.0) for h in heads]
        s *= 2
    sts = [st_scr[bb, h] for bb, h in units]
    o_inter = [_bdot_nt(qh[u] * jnp.exp(bhs[u]), sts[u]) for u in heads]
    o_intra = [_bdot(att[u], vh[u]) for u in heads]
    b_end = [bhs[u][rows - 1:rows, :] for u in heads]
    upd = [_bdot(vh[u].T, kh[u] * jnp.exp(b_end[u] - bhs[u])) for u in heads]
    for u, (bb, h) in enumerate(units):
        st_scr[bb, h] = sts[u] * jnp.exp(b_end[u]) + upd[u]
        o = _rms(o_intra[u] + o_inter[u]) * nw_ref[...]
        zg = _silu(z_ref[bb, :, h * GLA_DV:(h + 1) * GLA_DV])
        o_ref[bb, :, h * GLA_DV:(h + 1) * GLA_DV] = (o * zg).astype(BF16)


def _gla(big, small, w_gate, b_gate, norm_w, bsz, seq):
    nb = GLA_BATCH_TILE
    nt = seq // TIME_TILE
    big3 = big.reshape(bsz, seq, big.shape[1])
    small3 = small.reshape(bsz, seq, SMALL_W)
    wg = jnp.zeros((SMALL_W, GLA_QK_W), F32).at[SMALL_GK:SMALL_GK + GLA_GATE_RANK].set(w_gate)
    qk_base = BIG_GLA // GLA_QK_W
    v_base = (BIG_GLA + 2 * GLA_QK_W) // GLA_V_W
    const = lambda shape: pl.BlockSpec(shape, lambda g, t: (0, 0))
    tile = lambda width, cb: pl.BlockSpec((nb, TIME_TILE, width), lambda g, t: (g, t, cb))
    out = pl.pallas_call(
        _gla_kernel,
        out_shape=jax.ShapeDtypeStruct((bsz, seq, MIX_WIDTH), BF16),
        grid=(bsz // nb, nt),
        in_specs=[
            tile(GLA_QK_W, qk_base), tile(GLA_QK_W, qk_base + 1),
            tile(GLA_V_W, v_base), tile(GLA_V_W, v_base + 1),
            tile(SMALL_W, 0),
            const((SMALL_W, GLA_QK_W)), const((1, GLA_QK_W)), const((1, GLA_DV)),
        ],
        out_specs=tile(MIX_WIDTH, 0),
        scratch_shapes=[pltpu.VMEM((nb, GLA_HEADS, GLA_DV, GLA_DK), F32)],
        compiler_params=_params("parallel", "arbitrary"),
        name="gla",
    )(big3, big3, big3, big3, small3, wg.astype(BF16), b_gate.reshape(1, GLA_QK_W),
      norm_w.reshape(1, GLA_DV))
    return out.reshape(bsz * seq, MIX_WIDTH)


def _merge_kernel(h_ref, wga_ref, wgb_ref, wgc_ref, oa_ref, ob_ref, oc_ref, wb_ref, o_ref):
    h = h_ref[...]
    acc = _sigmoid(_dot(h, wga_ref[...])) * _dot(oa_ref[...], wb_ref[0])
    acc = acc + _sigmoid(_dot(h, wgb_ref[...])) * _dot(ob_ref[...], wb_ref[1])
    acc = acc + _sigmoid(_dot(h, wgc_ref[...])) * _dot(oc_ref[...], wb_ref[2])
    o_ref[...] = acc.astype(BF16)


def _merge(h, oa, ob, oc, w_big, w_branch, layer, seq):
    m = oa.shape[0]
    tm = min(1024, seq)
    tn = 512
    gate_w = lambda n: pl.BlockSpec(
        (None, D_MODEL, tn), lambda i, j: (layer, 0, (BIG_GATES + n * D_MODEL) // tn + j))
    branch = pl.BlockSpec((tm, MIX_WIDTH), lambda i, j: (i, 0))
    return pl.pallas_call(
        _merge_kernel,
        out_shape=jax.ShapeDtypeStruct((m, D_MODEL), BF16),
        grid=(m // tm, D_MODEL // tn),
        in_specs=[pl.BlockSpec((tm, D_MODEL), lambda i, j: (i, 0)),
                  gate_w(0), gate_w(1), gate_w(2), branch, branch, branch,
                  pl.BlockSpec((None, N_BRANCH, MIX_WIDTH, tn), lambda i, j: (layer, 0, 0, j))],
        out_specs=pl.BlockSpec((tm, tn), lambda i, j: (i, j)),
        compiler_params=_params("parallel", "arbitrary"),
        name="merge",
    )(h, w_big, w_big, w_big, oa, ob, oc, w_branch)


def _out_proj_kernel(m_ref, x_ref, mod_ref, g_ref, w_ref, o_ref):
    y = _dot(m_ref[...], w_ref[...])
    o_ref[...] = x_ref[...] + mod_ref[MOD_GT1:MOD_GT1 + 1, :] * (_rms(y) * g_ref[...])


def _out_proj(merged, x2, mod, g, w_out, layer, seq):
    m = x2.shape[0]
    tm = min(512, seq)
    tiles_per_seq = seq // tm
    return pl.pallas_call(
        _out_proj_kernel,
        out_shape=jax.ShapeDtypeStruct((m, D_MODEL), F32),
        grid=(m // tm,),
        in_specs=[
            pl.BlockSpec((tm, D_MODEL), lambda i: (i, 0)),
            pl.BlockSpec((tm, D_MODEL), lambda i: (i, 0)),
            pl.BlockSpec((None, N_MOD, D_MODEL), lambda i: (i // tiles_per_seq, 0, 0)),
            pl.BlockSpec((1, D_MODEL), lambda i: (0, 0)),
            pl.BlockSpec((None, D_MODEL, D_MODEL), lambda i: (layer, 0, 0)),
        ],
        out_specs=pl.BlockSpec((tm, D_MODEL), lambda i: (i, 0)),
        compiler_params=_params("parallel"),
        name="out_proj",
    )(merged, x2, mod, g, w_out)


def _mlp_kernel(x_ref, xn_ref, mod_ref, modn_ref, gpre_ref, gpost_ref, w1_ref, w2_ref, o_ref,
                h_scr, acc_scr):
    i = pl.program_id(0)
    f = pl.program_id(1)
    cur = i % 2
    chunk = xn_ref.shape[0]

    def modulated(x, m_ref):
        h = _rms(x) * gpre_ref[...]
        h = h * (1.0 + m_ref[MOD_SC2:MOD_SC2 + 1, :]) + m_ref[MOD_SH2:MOD_SH2 + 1, :]
        return h.astype(BF16)

    @pl.when((i == 0) & (f == 0))
    def _():
        h_scr[0] = modulated(x_ref[...], mod_ref)

    @pl.when(f == 0)
    def _():
        acc_scr[...] = jnp.zeros(acc_scr.shape, F32)

    a = jnp.maximum(_dot(h_scr[cur], w1_ref[...]), 0.0)
    acc_scr[...] += _dot((a * a).astype(BF16), w2_ref[...])
    h_scr[1 - cur, pl.ds(pl.multiple_of(f * chunk, chunk), chunk), :] = modulated(xn_ref[...], modn_ref)

    @pl.when(f == pl.num_programs(1) - 1)
    def _():
        y = _rms(acc_scr[...]) * gpost_ref[...]
        o_ref[...] = x_ref[...] + mod_ref[MOD_GT2:MOD_GT2 + 1, :] * y


def _mlp(x2, mod, g_pre, g_post, w1, w2, layer, seq):
    m = x2.shape[0]
    tm = min(512, seq)
    tf = 1024
    n_f = D_FF // tf
    n_i = m // tm
    chunk = tm // n_f
    tiles_per_seq = seq // tm
    nxt = lambda i: jnp.minimum(i + 1, n_i - 1)
    return pl.pallas_call(
        _mlp_kernel,
        out_shape=jax.ShapeDtypeStruct((m, D_MODEL), F32),
        grid=(n_i, n_f),
        in_specs=[
            pl.BlockSpec((tm, D_MODEL), lambda i, f: (i, 0)),
            pl.BlockSpec((chunk, D_MODEL), lambda i, f: (nxt(i) * n_f + f, 0)),
            pl.BlockSpec((None, N_MOD, D_MODEL), lambda i, f: (i // tiles_per_seq, 0, 0)),
            pl.BlockSpec((None, N_MOD, D_MODEL), lambda i, f: (nxt(i) // tiles_per_seq, 0, 0)),
            pl.BlockSpec((1, D_MODEL), lambda i, f: (0, 0)),
            pl.BlockSpec((1, D_MODEL), lambda i, f: (0, 0)),
            pl.BlockSpec((None, D_MODEL, tf), lambda i, f: (layer, 0, f)),
            pl.BlockSpec((None, tf, D_MODEL), lambda i, f: (layer, f, 0)),
        ],
        out_specs=pl.BlockSpec((tm, D_MODEL), lambda i, f: (i, 0)),
        scratch_shapes=[pltpu.VMEM((2, tm, D_MODEL), BF16), pltpu.VMEM((tm, D_MODEL), F32)],
        compiler_params=_params("arbitrary", "arbitrary"),
        name="mlp",
    )(x2, x2, mod, mod, g_pre, g_post, w1, w2)


def kernel(x, c, w_ada, b_ada, g_pre_mix, g_post_mix, g_pre_mlp, g_post_mlp, w_in, conv_gdn, gdn_a_log, gdn_dt_bias, gdn_norm, conv_lru, conv_lru_b, lru_w_a, lru_b_a, lru_w_i, lru_b_i, lru_lambda, gla_w_gate, gla_b_gate, gla_norm, w_branch, w_out, w_mlp1, w_mlp2):
    bsz, seq, _ = x.shape
    n_layers = w_in.shape[0]
    assert seq % LRU_TIME_TILE == 0 and x.shape[2] == D_MODEL and bsz % GDN_BATCH_TILE == 0
    row = lambda v: v.reshape(1, D_MODEL)

    mod_all = _ada_mod(c, w_ada, b_ada).reshape(n_layers, bsz, N_MOD, D_MODEL)
    x2 = x.reshape(bsz * seq, D_MODEL)
    w_big, w_small_t = _split_w_in(w_in)
    w_branch_b = w_branch.astype(BF16)
    w_out_b = w_out.astype(BF16)
    w_mlp1_b = w_mlp1.astype(BF16)
    w_mlp2_b = w_mlp2.astype(BF16)
    for l in range(n_layers):
        mod = mod_all[l]
        big, small, small_t, h = _in_proj(x2, mod, row(g_pre_mix[l]), w_big, w_small_t, l, seq)
        oa = _gdn(big, small, small_t, conv_gdn[l], gdn_a_log[l], gdn_dt_bias[l], gdn_norm[l],
                  bsz, seq)
        ob = _lru(big, conv_lru[l], conv_lru_b[l], lru_w_a[l], lru_b_a[l], lru_w_i[l],
                  lru_b_i[l], lru_lambda[l], bsz, seq)
        oc = _gla(big, small, gla_w_gate[l], gla_b_gate[l], gla_norm[l], bsz, seq)
        merged = _merge(h, oa, ob, oc, w_big, w_branch_b, l, seq)
        x2 = _out_proj(merged, x2, mod, row(g_post_mix[l]), w_out_b, l, seq)
        x2 = _mlp(x2, mod, row(g_pre_mlp[l]), row(g_post_mlp[l]), w_mlp1_b, w_mlp2_b, l, seq)
    return x2.reshape(bsz, seq, D_MODEL)
```

```python
import functools

import jax
import jax.numpy as jnp
from jax import lax
from jax.experimental import pallas as pl
from jax.experimental.pallas import tpu as pltpu

F32 = jnp.float32
BF16 = jnp.bfloat16
HIGHEST = lax.Precision.HIGHEST

D_MODEL = 2048
MIX_WIDTH = D_MODEL // 2
N_BRANCH = 3
GDN_DK = 128
GDN_DV = 128
GDN_HEADS = MIX_WIDTH // GDN_DV
GDN_CHUNK = 64
LRU_WIDTH = MIX_WIDTH
LRU_BLOCKS = 8
LRU_BLOCK_W = LRU_WIDTH // LRU_BLOCKS
LRU_C = 8.0
GLA_HEADS = 4
GLA_DV = MIX_WIDTH // GLA_HEADS
GLA_DK = GLA_DV // 2
GLA_GATE_RANK = 16
GLA_GATE_NORM = 16.0
GLA_BASE_CHUNK = 16
CONV_K = 4
D_FF = 4 * D_MODEL
N_MOD = 6
EPS = 1e-6

GDN_QK_W = GDN_HEADS * GDN_DK
GDN_V_W = GDN_HEADS * GDN_DV
GLA_QK_W = GLA_HEADS * GLA_DK
GLA_V_W = GLA_HEADS * GLA_DV
IN_SPLITS = (GDN_QK_W, GDN_QK_W, GDN_V_W, GDN_V_W, GDN_HEADS, GDN_HEADS,
             LRU_WIDTH, LRU_WIDTH,
             GLA_QK_W, GLA_QK_W, GLA_V_W, GLA_V_W, GLA_GATE_RANK,
             N_BRANCH * D_MODEL)
IN_OFFS = tuple(sum(IN_SPLITS[:i]) for i in range(len(IN_SPLITS) + 1))

LANES = 128
SUBLANES = 8
VMEM_LIMIT_BYTES = 56 * 1024 * 1024

BIG_GDN = 0
BIG_LRU = BIG_GDN + 4 * MIX_WIDTH
BIG_GLA = BIG_LRU + 2 * LRU_WIDTH
BIG_GATES = BIG_GLA + 2 * GLA_QK_W + 2 * GLA_V_W
BIG_W = BIG_GATES + N_BRANCH * D_MODEL
SMALL_BETA = 0
SMALL_ALPHA = GDN_HEADS
SMALL_GK = 2 * GDN_HEADS
SMALL_W = LANES

MOD_SH1, MOD_SC1, MOD_GT1, MOD_SH2, MOD_SC2, MOD_GT2 = range(N_MOD)

TIME_TILE = 128
GDN_BATCH_TILE = 2
GLA_BATCH_TILE = 4
LRU_TIME_TILE = 256
HALO = SUBLANES

W_PREP_TILE = 512
W_PREP_RUN1_TILE = IN_OFFS[4] // W_PREP_TILE
W_PREP_RUN2_TILE = (IN_OFFS[4] + IN_OFFS[12] - IN_OFFS[6]) // W_PREP_TILE
W_PREP_SHIFT1 = IN_OFFS[6] - IN_OFFS[4]
W_PREP_SHIFT2 = W_PREP_SHIFT1 + IN_OFFS[13] - IN_OFFS[12]
W_PREP_NEXT = W_PREP_SHIFT2
W_PREP_NARROW1 = slice(IN_OFFS[4] - W_PREP_RUN1_TILE * W_PREP_TILE,
                       IN_OFFS[6] - W_PREP_RUN1_TILE * W_PREP_TILE)
W_PREP_NARROW2 = slice(IN_OFFS[12] - W_PREP_RUN2_TILE * W_PREP_TILE,
                       IN_OFFS[13] - W_PREP_RUN2_TILE * W_PREP_TILE)
assert (W_PREP_NARROW1.start, W_PREP_NARROW1.stop) == (SMALL_BETA, SMALL_GK)
assert (W_PREP_NARROW2.start, W_PREP_NARROW2.stop) == (SMALL_GK, SMALL_GK + GLA_GATE_RANK)
assert W_PREP_TILE % W_PREP_NEXT == 0 and IN_OFFS[-1] % W_PREP_NEXT == 0


def _sigmoid(x):
    return 1.0 / (1.0 + jnp.exp(-x))


def _silu(x):
    return x * _sigmoid(x)


def _softplus(x):
    return jnp.maximum(x, 0.0) + jnp.log(1.0 + jnp.exp(-jnp.abs(x)))


def _log_sigmoid(x):
    return -_softplus(-x)


def _gelu_tanh(x):
    c = 0.7978845608028654
    return 0.5 * x * (1.0 + jnp.tanh(c * (x + 0.044715 * (x * x * x))))


def _rms(x):
    return x * lax.rsqrt(jnp.mean(x * x, axis=-1, keepdims=True) + EPS)


def _dot(a, b, precision=None):
    return jnp.dot(a, b, preferred_element_type=F32, precision=precision)


def _dot_nt(a, b):
    return lax.dot_general(a, b, (((1,), (1,)), ((), ())), preferred_element_type=F32)


def _bdot(a, b):
    return _dot(a.astype(BF16), b.astype(BF16))


def _bdot_nt(a, b):
    return _dot_nt(a.astype(BF16), b.astype(BF16))


def _params(*sem):
    return pltpu.CompilerParams(dimension_semantics=sem, vmem_limit_bytes=VMEM_LIMIT_BYTES)


def _ada_kernel(c_ref, w_ref, b_ref, o_ref):
    sc = _silu(c_ref[...]).astype(BF16)
    o_ref[...] = _dot(sc, w_ref[...].astype(BF16)) + b_ref[...]


def _ada_mod(c, w_ada, b_ada):
    n_layers, _, n_out = w_ada.shape
    bsz = c.shape[0]
    tn = 1024
    return pl.pallas_call(
        _ada_kernel,
        out_shape=jax.ShapeDtypeStruct((n_layers, bsz, n_out), F32),
        grid=(n_layers, n_out // tn),
        in_specs=[
            pl.BlockSpec((bsz, D_MODEL), lambda l, j: (0, 0)),
            pl.BlockSpec((None, D_MODEL, tn), lambda l, j: (l, 0, j)),
            pl.BlockSpec((None, 1, tn), lambda l, j: (l, 0, j)),
        ],
        out_specs=pl.BlockSpec((None, bsz, tn), lambda l, j: (l, 0, j)),
        compiler_params=_params("parallel", "parallel"),
        name="ada_mod",
    )(c, w_ada, b_ada.reshape(n_layers, 1, n_out))


def _w_prep_kernel(a_ref, b_ref, o_ref, wst_ref):
    j = pl.program_id(1)

    def emit(shift):
        src = a_ref[...] if shift == 0 else jnp.concatenate([a_ref[shift:, :], b_ref[:shift, :]],
                                                            axis=0)
        o_ref[...] = src.T.astype(BF16)

    @pl.when(j == 0)
    def _():
        wst_ref[...] = jnp.zeros(wst_ref.shape, BF16)

    @pl.when(j < W_PREP_RUN1_TILE)
    def _():
        emit(0)

    @pl.when((j >= W_PREP_RUN1_TILE) & (j < W_PREP_RUN2_TILE))
    def _():
        emit(W_PREP_SHIFT1)

    @pl.when(j >= W_PREP_RUN2_TILE)
    def _():
        emit(W_PREP_SHIFT2)

    @pl.when(j == W_PREP_RUN1_TILE)
    def _():
        wst_ref[W_PREP_NARROW1, :] = a_ref[W_PREP_NARROW1, :].astype(BF16)

    @pl.when(j == W_PREP_RUN2_TILE)
    def _():
        wst_ref[W_PREP_NARROW2, :] = a_ref[W_PREP_NARROW2, :].astype(BF16)


def _split_w_in(w_in):
    n_layers = w_in.shape[0]
    w_t = jnp.swapaxes(w_in, 1, 2)
    next_per_tile = W_PREP_TILE // W_PREP_NEXT
    w_big, w_small_t = pl.pallas_call(
        _w_prep_kernel,
        out_shape=(jax.ShapeDtypeStruct((n_layers, D_MODEL, BIG_W), BF16),
                   jax.ShapeDtypeStruct((n_layers, SMALL_W, D_MODEL), BF16)),
        grid=(n_layers, BIG_W // W_PREP_TILE),
        in_specs=[
            pl.BlockSpec((None, W_PREP_TILE, D_MODEL), lambda l, j: (l, j, 0)),
            pl.BlockSpec((None, W_PREP_NEXT, D_MODEL), lambda l, j: (l, (j + 1) * next_per_tile, 0)),
        ],
        out_specs=(pl.BlockSpec((None, D_MODEL, W_PREP_TILE), lambda l, j: (l, 0, j)),
                   pl.BlockSpec((None, SMALL_W, D_MODEL), lambda l, j: (l, 0, 0))),
        compiler_params=_params("parallel", "arbitrary"),
        name="w_prep",
    )(w_t, w_t)
    return w_big, w_small_t


def _in_proj_kernel(xn_ref, modn_ref, g_ref, w_ref, wst_ref,
                    big_ref, small_ref, smallt_ref, h_ref, h_scr, *, chunks_per_tile):
    r = pl.program_id(0)
    j = pl.program_id(1)
    chunk = xn_ref.shape[0]

    def build_chunk():
        h = _rms(xn_ref[...]) * g_ref[...]
        h = h * (1.0 + modn_ref[MOD_SC1:MOD_SC1 + 1, :]) + modn_ref[MOD_SH1:MOD_SH1 + 1, :]
        row0 = pl.multiple_of(jnp.minimum(j, chunks_per_tile - 1) * chunk, chunk)
        h_scr[r % 2, pl.ds(row0, chunk), :] = h.astype(BF16)

    @pl.when(r == 0)
    def _():
        build_chunk()

    @pl.when(r > 0)
    def _():
        prev = (r + 1) % 2

        @pl.when(j == 0)
        def _():
            hb = h_scr[prev]
            h_ref[...] = hb
            st = _dot_nt(wst_ref[...], hb)
            smallt_ref[...] = st
            small_ref[...] = st.T

        big_ref[...] = _dot(h_scr[prev], w_ref[...])
        build_chunk()


def _in_proj(x2, mod, g, w_big, w_small_t, layer, seq):
    m = x2.shape[0]
    tm = min(1024, seq)
    tn = 1024
    n_i = m // tm
    n_j = BIG_GATES // tn
    chunks_per_tile = min(n_j, SUBLANES)
    chunk = tm // chunks_per_tile
    tiles_per_seq = seq // tm
    build = lambda r: jnp.minimum(r, n_i - 1)
    out = lambda r: jnp.maximum(r - 1, 0)
    return pl.pallas_call(
        functools.partial(_in_proj_kernel, chunks_per_tile=chunks_per_tile),
        out_shape=(jax.ShapeDtypeStruct((m, BIG_GATES), F32),
                   jax.ShapeDtypeStruct((m, SMALL_W), F32),
                   jax.ShapeDtypeStruct((SMALL_W, m), F32),
                   jax.ShapeDtypeStruct((m, D_MODEL), BF16)),
        grid=(n_i + 1, n_j),
        in_specs=[
            pl.BlockSpec((chunk, D_MODEL), lambda r, j: (
                build(r) * chunks_per_tile + jnp.minimum(j, chunks_per_tile - 1), 0)),
            pl.BlockSpec((None, N_MOD, D_MODEL), lambda r, j: (build(r) // tiles_per_seq, 0, 0)),
            pl.BlockSpec((1, D_MODEL), lambda r, j: (0, 0)),
            pl.BlockSpec((None, D_MODEL, tn), lambda r, j: (layer, 0, j)),
            pl.BlockSpec((None, SMALL_W, D_MODEL), lambda r, j: (layer, 0, 0)),
        ],
        out_specs=(pl.BlockSpec((tm, tn), lambda r, j: (out(r), jnp.where(r == 0, 0, j))),
                   pl.BlockSpec((tm, SMALL_W), lambda r, j: (out(r), 0)),
                   pl.BlockSpec((SMALL_W, tm), lambda r, j: (0, out(r))),
                   pl.BlockSpec((tm, D_MODEL), lambda r, j: (out(r), 0))),
        scratch_shapes=[pltpu.VMEM((2, tm, D_MODEL), BF16)],
        compiler_params=_params("arbitrary", "arbitrary"),
        name="in_proj",
    )(x2, mod, g, w_big, w_small_t)


def _causal_conv(xbuf, cw_ref, rows):
    acc = cw_ref[CONV_K - 1:CONV_K, :] * xbuf[HALO:HALO + rows, :]
    for j in range(CONV_K - 1):
        off = HALO - (CONV_K - 1) + j
        acc = acc + cw_ref[j:j + 1, :] * xbuf[off:off + rows, :]
    return acc


def _gdn_kernel(*refs):
    nb = GDN_BATCH_TILE
    q_ref, k_ref, v_ref, z_ref, sm_ref = refs[:5]
    smt_refs = refs[5:5 + nb]
    cw_ref, alr_ref, alc_ref, dtr_ref, dtc_ref, nw_ref, o_ref, xbuf, s_scr = refs[5 + nb:]
    rows = TIME_TILE

    @pl.when(pl.program_id(1) == 0)
    def _():
        xbuf[:, 0:HALO, :] = jnp.zeros((nb, HALO, xbuf.shape[2]), F32)
        s_scr[...] = jnp.zeros(s_scr.shape, F32)

    ri = lax.broadcasted_iota(jnp.int32, (rows, rows), 0)
    ci = lax.broadcasted_iota(jnp.int32, (rows, rows), 1)
    same = (ri // GDN_CHUNK) == (ci // GDN_CHUNK)
    causal = same & (ci <= ri)
    strict = same & (ci < ri)
    cum_c = causal.astype(F32)
    cum_r = (same & (ri <= ci)).astype(F32)
    eye = (ri == ci).astype(F32)
    level_masks = []
    s = 1
    while s < GDN_CHUNK:
        level_masks.append((ri // (2 * s) == ci // (2 * s)) & (ri % (2 * s) >= s) & (ci % (2 * s) < s))
        s *= 2
    zeros_half = jnp.zeros((GDN_CHUNK, GDN_DV), F32)
    col_row = lax.broadcasted_iota(jnp.int32, (1, rows), 1)
    n_chunks = rows // GDN_CHUNK

    units = [(bb, h) for bb in range(nb) for h in range(GDN_HEADS)]
    idx = range(len(units))
    qs, kts, gccs, gcrs, a_qks, a_kks, rhss = [], [], [], [], [], [], []
    for bb in range(nb):
        xb = xbuf.at[bb]
        xb[HALO:HALO + rows, 0:GDN_QK_W] = q_ref[bb]
        xb[HALO:HALO + rows, GDN_QK_W:2 * GDN_QK_W] = k_ref[bb]
        xb[HALO:HALO + rows, 2 * GDN_QK_W:2 * GDN_QK_W + GDN_V_W] = v_ref[bb]
        qkv = _silu(_causal_conv(xb, cw_ref, rows))
        xb[0:HALO, :] = xb[rows:rows + HALO, :]

        sm = sm_ref[bb]
        smt = smt_refs[bb][...]
        beta_c = _sigmoid(sm)
        g_c = -jnp.exp(alr_ref[...]) * _softplus(sm + dtr_ref[...])
        g_r = -jnp.exp(alc_ref[...]) * _softplus(smt + dtc_ref[...])
        gc_c = _dot(cum_c, g_c, precision=HIGHEST)
        gc_r = _dot(g_r, cum_r, precision=HIGHEST)

        for h in range(GDN_HEADS):
            qh = qkv[:, h * GDN_DK:(h + 1) * GDN_DK]
            kh = qkv[:, GDN_QK_W + h * GDN_DK:GDN_QK_W + (h + 1) * GDN_DK]
            vh = qkv[:, 2 * GDN_QK_W + h * GDN_DV:2 * GDN_QK_W + (h + 1) * GDN_DV]
            qh = qh * lax.rsqrt(jnp.sum(qh * qh, axis=-1, keepdims=True) + EPS) * (GDN_DK ** -0.5)
            kh = kh * lax.rsqrt(jnp.sum(kh * kh, axis=-1, keepdims=True) + EPS)
            bc = beta_c[:, SMALL_BETA + h:SMALL_BETA + h + 1]
            gcc = gc_c[:, SMALL_ALPHA + h:SMALL_ALPHA + h + 1]
            gcr = gc_r[SMALL_ALPHA + h:SMALL_ALPHA + h + 1, :]
            e = jnp.exp(jnp.where(causal, gcc - gcr, 0.0))
            kb = kh * bc
            kt = kh.T
            qk_kk = _bdot(jnp.concatenate([qh, kb], axis=0), kt)
            eg = jnp.exp(gcc)
            qs.append(qh * eg)
            kts.append(kt)
            gccs.append(gcc)
            gcrs.append(gcr)
            a_qks.append(jnp.where(causal, qk_kk[:rows] * e, 0.0))
            a_kks.append(jnp.where(strict, qk_kk[rows:] * e, 0.0))
            rhss.append(jnp.concatenate([vh * bc, kb * eg], axis=1))

    tinv = [eye - jnp.where(level_masks[0], a_kks[u], 0.0) for u in idx]
    for lm in level_masks[1:]:
        tb = [tinv[u].astype(BF16) for u in idx]
        et = [_dot(jnp.where(lm, a_kks[u], 0.0).astype(BF16), tb[u]) for u in idx]
        tinv = [tinv[u] - _dot(tb[u], et[u].astype(BF16)) for u in idx]
    uws = [_bdot(tinv[u], rhss[u]) for u in idx]

    states = [s_scr[bb, h] for bb, h in units]
    outs = [[] for _ in idx]
    for c in range(n_chunks):
        lo = c * GDN_CHUNK
        hi = lo + GDN_CHUNK
        in_chunk = (col_row >= lo) & (col_row < hi)
        rs = [_bdot(jnp.concatenate([uws[u][lo:hi, GDN_DV:], qs[u][lo:hi]], axis=0), states[u])
              for u in idx]
        v_pads = []
        for u in idx:
            pads = [zeros_half] * n_chunks
            pads[c] = uws[u][lo:hi, :GDN_DV] - rs[u][:GDN_CHUNK]
            v_pads.append(jnp.concatenate(pads, axis=0))
        new_states = []
        for u in idx:
            g_last = gccs[u][hi - 1:hi, :]
            tail = jnp.where(in_chunk, jnp.exp(jnp.where(in_chunk, g_last - gcrs[u], 0.0)), 0.0)
            new_states.append(states[u] * jnp.exp(g_last) + _bdot(kts[u] * tail, v_pads[u]))
        states = new_states
        for u in idx:
            outs[u].append(rs[u][GDN_CHUNK:] + _bdot(a_qks[u][lo:hi, :], v_pads[u]))
    for u, (bb, h) in enumerate(units):
        s_scr[bb, h] = states[u]
        o = _rms(jnp.concatenate(outs[u], axis=0)) * nw_ref[...]
        zg = _silu(z_ref[bb, :, h * GDN_DV:(h + 1) * GDN_DV])
        o_ref[bb, :, h * GDN_DV:(h + 1) * GDN_DV] = (o * zg).astype(BF16)


def _gdn(big, small, small_t, conv_w, a_log, dt_bias, norm_w, bsz, seq):
    nb = GDN_BATCH_TILE
    nt = seq // TIME_TILE
    qkv_w = 2 * GDN_QK_W + GDN_V_W
    big3 = big.reshape(bsz, seq, big.shape[1])
    small3 = small.reshape(bsz, seq, SMALL_W)

    def pad_row(v):
        return jnp.zeros((1, SMALL_W), F32).at[0, SMALL_ALPHA:SMALL_ALPHA + GDN_HEADS].set(v)

    alr = pad_row(a_log)
    dtr = pad_row(dt_bias)
    col = lambda cb: pl.BlockSpec((nb, TIME_TILE, MIX_WIDTH), lambda g, t: (g, t, cb))
    const = lambda shape: pl.BlockSpec(shape, lambda g, t: (0, 0))
    small_t_spec = lambda bb: pl.BlockSpec(
        (SMALL_W, TIME_TILE), lambda g, t: (0, (g * nb + bb) * nt + t))
    out = pl.pallas_call(
        _gdn_kernel,
        out_shape=jax.ShapeDtypeStruct((bsz, seq, MIX_WIDTH), BF16),
        grid=(bsz // nb, nt),
        in_specs=[
            col(0), col(1), col(2), col(3),
            pl.BlockSpec((nb, TIME_TILE, SMALL_W), lambda g, t: (g, t, 0)),
            *[small_t_spec(bb) for bb in range(nb)],
            const((CONV_K, qkv_w)),
            const((1, SMALL_W)), const((SMALL_W, 1)),
            const((1, SMALL_W)), const((SMALL_W, 1)),
            const((1, GDN_DV)),
        ],
        out_specs=pl.BlockSpec((nb, TIME_TILE, MIX_WIDTH), lambda g, t: (g, t, 0)),
        scratch_shapes=[pltpu.VMEM((nb, HALO + TIME_TILE, qkv_w), F32),
                        pltpu.VMEM((nb, GDN_HEADS, GDN_DK, GDN_DV), F32)],
        compiler_params=_params("parallel", "arbitrary"),
        name="gdn",
    )(big3, big3, big3, big3, small3, *([small_t] * nb), conv_w, alr, alr.T, dtr, dtr.T,
      norm_w.reshape(1, GDN_DV))
    return out.reshape(bsz * seq, MIX_WIDTH)


def _lru_kernel(x_ref, y_ref, cw_ref, cb_ref, wa_ref, ba_ref, wi_ref, bi_ref, lam_ref,
                o_ref, xbuf, h_scr):
    rows = LRU_TIME_TILE
    first = pl.program_id(1) == 0

    @pl.when(first)
    def _():
        xbuf[0:HALO, :] = jnp.zeros((HALO, LRU_WIDTH), F32)
        h_scr[...] = jnp.zeros(h_scr.shape, F32)

    xbuf[HALO:HALO + rows, :] = x_ref[...]
    xl = _causal_conv(xbuf, cw_ref, rows) + cb_ref[...]
    xbuf[0:HALO, :] = xbuf[rows:rows + HALO, :]

    xlb = xl.astype(BF16)
    r_parts = []
    i_parts = []
    for k in range(LRU_BLOCKS):
        blk = xlb[:, k * LRU_BLOCK_W:(k + 1) * LRU_BLOCK_W]
        r_parts.append(_dot(blk, wa_ref[k]))
        i_parts.append(_dot(blk, wi_ref[k]))
    r = _sigmoid(jnp.concatenate(r_parts, axis=1) + ba_ref[...])
    ig = _sigmoid(jnp.concatenate(i_parts, axis=1) + bi_ref[...])
    log_a = -LRU_C * r * _softplus(-lam_ref[...])
    a = jnp.exp(log_a)
    mult = jnp.sqrt(-jnp.tanh(log_a) * (a * a + 1.0))
    row = lax.broadcasted_iota(jnp.int32, (rows, 1), 0)
    mult = jnp.where(first & (row == 0), 1.0, mult)
    b = mult * ig * xl

    row8 = row[:SUBLANES]
    shift = 1
    while shift < SUBLANES:
        a_roll = pltpu.roll(a, shift, 0)
        b_roll = pltpu.roll(b, shift, 0)
        keep = row8 >= shift
        a_prev = jnp.concatenate([jnp.where(keep, a_roll[:SUBLANES], 1.0), a_roll[SUBLANES:]], axis=0)
        b_prev = jnp.concatenate([jnp.where(keep, b_roll[:SUBLANES], 0.0), b_roll[SUBLANES:]], axis=0)
        b = a * b_prev + b
        a = a * a_prev
        shift *= 2
    while shift < rows:
        b = jnp.concatenate([b[:shift], a[shift:] * b[:rows - shift] + b[shift:]], axis=0)
        a = jnp.concatenate([a[:shift], a[shift:] * a[:rows - shift]], axis=0)
        shift *= 2
    hs = a * h_scr[0:1, :] + b
    h_scr[0:1, :] = hs[rows - 1:rows, :]
    o_ref[...] = (_gelu_tanh(y_ref[...]) * hs).astype(BF16)


def _lru(big, conv_w, conv_b, w_a, b_a, w_i, b_i, lam, bsz, seq):
    m = bsz * seq
    nt = seq // LRU_TIME_TILE
    base = BIG_LRU // LRU_WIDTH
    row1 = lambda v: v.reshape(1, LRU_WIDTH)
    const2 = lambda shape: pl.BlockSpec(shape, lambda b, t: (0, 0))
    const3 = lambda shape: pl.BlockSpec(shape, lambda b, t: (0, 0, 0))
    return pl.pallas_call(
        _lru_kernel,
        out_shape=jax.ShapeDtypeStruct((m, LRU_WIDTH), BF16),
        grid=(bsz, nt),
        in_specs=[
            pl.BlockSpec((LRU_TIME_TILE, LRU_WIDTH), lambda b, t: (b * nt + t, base)),
            pl.BlockSpec((LRU_TIME_TILE, LRU_WIDTH), lambda b, t: (b * nt + t, base + 1)),
            const2((CONV_K, LRU_WIDTH)), const2((1, LRU_WIDTH)),
            const3((LRU_BLOCKS, LRU_BLOCK_W, LRU_BLOCK_W)), const2((1, LRU_WIDTH)),
            const3((LRU_BLOCKS, LRU_BLOCK_W, LRU_BLOCK_W)), const2((1, LRU_WIDTH)),
            const2((1, LRU_WIDTH)),
        ],
        out_specs=pl.BlockSpec((LRU_TIME_TILE, LRU_WIDTH), lambda b, t: (b * nt + t, 0)),
        scratch_shapes=[pltpu.VMEM((HALO + LRU_TIME_TILE, LRU_WIDTH), F32),
                        pltpu.VMEM((SUBLANES, LRU_WIDTH), F32)],
        compiler_params=_params("parallel", "arbitrary"),
        name="rg_lru",
    )(big, big, conv_w, row1(conv_b), w_a.astype(BF16), row1(b_a), w_i.astype(BF16),
      row1(b_i), row1(lam))


def _gla_kernel(q_ref, k_ref, v_ref, z_ref, sm_ref, wg_ref, bg_ref, nw_ref, o_ref, st_scr):
    rows = TIME_TILE

    @pl.when(pl.program_id(1) == 0)
    def _():
        st_scr[...] = jnp.zeros(st_scr.shape, F32)

    nb = GLA_BATCH_TILE
    ri = lax.broadcasted_iota(jnp.int32, (rows, rows), 0)
    ci = lax.broadcasted_iota(jnp.int32, (rows, rows), 1)
    tri = (ci <= ri).astype(F32)
    bcums = []
    for bb in range(nb):
        log_a = _log_sigmoid(_dot(sm_ref[bb].astype(BF16), wg_ref[...]) + bg_ref[...]) / GLA_GATE_NORM
        bcums.append(_dot(tri, log_a, precision=HIGHEST))
    row = lax.broadcasted_iota(jnp.int32, (rows, 1), 0)

    def boundary_rows(bh, width, offset, first_zero):
        out = jnp.zeros_like(bh) if first_zero else None
        for g in range(rows // width):
            src = g * width + offset
            if src < 0:
                continue
            val = jnp.broadcast_to(bh[src:src + 1, :], bh.shape)
            out = val if out is None else jnp.where(row // width == g, val, out)
        return out

    units = [(bb, h) for bb in range(nb) for h in range(GLA_HEADS)]
    heads = range(len(units))
    bhs = [bcums[bb][:, h * GLA_DK:(h + 1) * GLA_DK] for bb, h in units]
    qh = [q_ref[bb, :, h * GLA_DK:(h + 1) * GLA_DK] * (GLA_DK ** -0.5) for bb, h in units]
    kh = [k_ref[bb, :, h * GLA_DK:(h + 1) * GLA_DK] for bb, h in units]
    vh = [v_ref[bb, :, h * GLA_DV:(h + 1) * GLA_DV] for bb, h in units]
    b_loc = [bhs[h] - boundary_rows(bhs[h], GLA_BASE_CHUNK, -1, True) for h in heads]
    base_mask = (ri // GLA_BASE_CHUNK == ci // GLA_BASE_CHUNK) & (ci <= ri)
    att = [jnp.where(base_mask,
                     _bdot_nt(qh[h] * jnp.exp(b_loc[h]), kh[h] * jnp.exp(-b_loc[h])), 0.0)
           for h in heads]
    s = GLA_BASE_CHUNK
    while s < rows:
        mask = ((ri // (2 * s) == ci // (2 * s)) & (ri % (2 * s) >= s) & (ci % (2 * s) < s))
        refs = [boundary_rows(bhs[h], 2 * s, s - 1, False) for h in heads]
        cross = [_bdot_nt(qh[h] * jnp.exp(jnp.minimum(bhs[h] - refs[h], 0.0)),
                          kh[h] * jnp.exp(jnp.minimum(refs[h] - bhs[h], 0.0))) for h in heads]
        att = [att[h] + jnp.where(mask, cross[h], 0.0) for h in heads]
        s *= 2
    sts = [st_scr[bb, h] for bb, h in units]
    o_inter = [_bdot_nt(qh[u] * jnp.exp(bhs[u]), sts[u]) for u in heads]
    o_intra = [_bdot(att[u], vh[u]) for u in heads]
    b_end = [bhs[u][rows - 1:rows, :] for u in heads]
    upd = [_bdot(vh[u].T, kh[u] * jnp.exp(b_end[u] - bhs[u])) for u in heads]
    for u, (bb, h) in enumerate(units):
        st_scr[bb, h] = sts[u] * jnp.exp(b_end[u]) + upd[u]
        o = _rms(o_intra[u] + o_inter[u]) * nw_ref[...]
        zg = _silu(z_ref[bb, :, h * GLA_DV:(h + 1) * GLA_DV])
        o_ref[bb, :, h * GLA_DV:(h + 1) * GLA_DV] = (o * zg).astype(BF16)


def _gla(big, small, w_gate, b_gate, norm_w, bsz, seq):
    nb = GLA_BATCH_TILE
    nt = seq // TIME_TILE
    big3 = big.reshape(bsz, seq, big.shape[1])
    small3 = small.reshape(bsz, seq, SMALL_W)
    wg = jnp.zeros((SMALL_W, GLA_QK_W), F32).at[SMALL_GK:SMALL_GK + GLA_GATE_RANK].set(w_gate)
    qk_base = BIG_GLA // GLA_QK_W
    v_base = (BIG_GLA + 2 * GLA_QK_W) // GLA_V_W
    const = lambda shape: pl.BlockSpec(shape, lambda g, t: (0, 0))
    tile = lambda width, cb: pl.BlockSpec((nb, TIME_TILE, width), lambda g, t: (g, t, cb))
    out = pl.pallas_call(
        _gla_kernel,
        out_shape=jax.ShapeDtypeStruct((bsz, seq, MIX_WIDTH), BF16),
        grid=(bsz // nb, nt),
        in_specs=[
            tile(GLA_QK_W, qk_base), tile(GLA_QK_W, qk_base + 1),
            tile(GLA_V_W, v_base), tile(GLA_V_W, v_base + 1),
            tile(SMALL_W, 0),
            const((SMALL_W, GLA_QK_W)), const((1, GLA_QK_W)), const((1, GLA_DV)),
        ],
        out_specs=tile(MIX_WIDTH, 0),
        scratch_shapes=[pltpu.VMEM((nb, GLA_HEADS, GLA_DV, GLA_DK), F32)],
        compiler_params=_params("parallel", "arbitrary"),
        name="gla",
    )(big3, big3, big3, big3, small3, wg.astype(BF16), b_gate.reshape(1, GLA_QK_W),
      norm_w.reshape(1, GLA_DV))
    return out.reshape(bsz * seq, MIX_WIDTH)


def _merge_kernel(h_ref, wga_ref, wgb_ref, wgc_ref, oa_ref, ob_ref, oc_ref, wb_ref, o_ref):
    h = h_ref[...]
    acc = _sigmoid(_dot(h, wga_ref[...])) * _dot(oa_ref[...], wb_ref[0])
    acc = acc + _sigmoid(_dot(h, wgb_ref[...])) * _dot(ob_ref[...], wb_ref[1])
    acc = acc + _sigmoid(_dot(h, wgc_ref[...])) * _dot(oc_ref[...], wb_ref[2])
    o_ref[...] = acc.astype(BF16)


def _merge(h, oa, ob, oc, w_big, w_branch, layer, seq):
    m = oa.shape[0]
    tm = min(1024, seq)
    tn = 512
    gate_w = lambda n: pl.BlockSpec(
        (None, D_MODEL, tn), lambda i, j: (layer, 0, (BIG_GATES + n * D_MODEL) // tn + j))
    branch = pl.BlockSpec((tm, MIX_WIDTH), lambda i, j: (i, 0))
    return pl.pallas_call(
        _merge_kernel,
        out_shape=jax.ShapeDtypeStruct((m, D_MODEL), BF16),
        grid=(m // tm, D_MODEL // tn),
        in_specs=[pl.BlockSpec((tm, D_MODEL), lambda i, j: (i, 0)),
                  gate_w(0), gate_w(1), gate_w(2), branch, branch, branch,
                  pl.BlockSpec((None, N_BRANCH, MIX_WIDTH, tn), lambda i, j: (layer, 0, 0, j))],
        out_specs=pl.BlockSpec((tm, tn), lambda i, j: (i, j)),
        compiler_params=_params("parallel", "arbitrary"),
        name="merge",
    )(h, w_big, w_big, w_big, oa, ob, oc, w_branch)


def _out_proj_kernel(m_ref, x_ref, mod_ref, g_ref, w_ref, o_ref):
    y = _dot(m_ref[...], w_ref[...])
    o_ref[...] = x_ref[...] + mod_ref[MOD_GT1:MOD_GT1 + 1, :] * (_rms(y) * g_ref[...])


def _out_proj(merged, x2, mod, g, w_out, layer, seq):
    m = x2.shape[0]
    tm = min(512, seq)
    tiles_per_seq = seq // tm
    return pl.pallas_call(
        _out_proj_kernel,
        out_shape=jax.ShapeDtypeStruct((m, D_MODEL), F32),
        grid=(m // tm,),
        in_specs=[
            pl.BlockSpec((tm, D_MODEL), lambda i: (i, 0)),
            pl.BlockSpec((tm, D_MODEL), lambda i: (i, 0)),
            pl.BlockSpec((None, N_MOD, D_MODEL), lambda i: (i // tiles_per_seq, 0, 0)),
            pl.BlockSpec((1, D_MODEL), lambda i: (0, 0)),
            pl.BlockSpec((None, D_MODEL, D_MODEL), lambda i: (layer, 0, 0)),
        ],
        out_specs=pl.BlockSpec((tm, D_MODEL), lambda i: (i, 0)),
        compiler_params=_params("parallel"),
        name="out_proj",
    )(merged, x2, mod, g, w_out)


def _mlp_kernel(xb_ref, xe_ref, modb_ref, mode_ref, gpre_ref, gpost_ref, w1_ref, w2_ref, o_ref,
                h_scr, acc_scr, *, n_tiles):
    r = pl.program_id(0)
    f = pl.program_id(1)
    chunk = xb_ref.shape[0]
    rows = pl.ds(pl.multiple_of(f * chunk, chunk), chunk)

    def build():
        h = _rms(xb_ref[...]) * gpre_ref[...]
        h = h * (1.0 + modb_ref[MOD_SC2:MOD_SC2 + 1, :]) + modb_ref[MOD_SH2:MOD_SH2 + 1, :]
        h_scr[r % 2, rows, :] = h.astype(BF16)

    def finish():
        y = _rms(acc_scr[r % 2, rows, :]) * gpost_ref[...]
        o_ref[...] = xe_ref[...] + mode_ref[MOD_GT2:MOD_GT2 + 1, :] * y

    @pl.when(r == 0)
    def _():
        @pl.when(f == 0)
        def _():
            acc_scr[...] = jnp.zeros(acc_scr.shape, F32)

        build()
        o_ref[...] = jnp.zeros(o_ref.shape, F32)

    @pl.when((r >= 1) & (r <= n_tiles))
    def _():
        prev = (r + 1) % 2

        @pl.when(f == 0)
        def _():
            acc_scr[prev] = jnp.zeros(acc_scr.shape[1:], F32)

        finish()
        a = jnp.maximum(_dot(h_scr[prev], w1_ref[...]), 0.0)
        acc_scr[prev] += _dot((a * a).astype(BF16), w2_ref[...])
        build()

    @pl.when(r == n_tiles + 1)
    def _():
        finish()


def _mlp(x2, mod, g_pre, g_post, w1, w2, layer, seq):
    m = x2.shape[0]
    tm = min(512, seq)
    tf = 1024
    n_f = D_FF // tf
    n_i = m // tm
    chunk = tm // n_f
    tiles_per_seq = seq // tm
    build = lambda r: jnp.minimum(r, n_i - 1)
    done = lambda r: jnp.clip(r - 2, 0, n_i - 1)
    wf = lambda r, f: jnp.where(r == 0, 0, jnp.where(r == n_i + 1, n_f - 1, f))
    return pl.pallas_call(
        functools.partial(_mlp_kernel, n_tiles=n_i),
        out_shape=jax.ShapeDtypeStruct((m, D_MODEL), F32),
        grid=(n_i + 2, n_f),
        in_specs=[
            pl.BlockSpec((chunk, D_MODEL), lambda r, f: (build(r) * n_f + f, 0)),
            pl.BlockSpec((chunk, D_MODEL), lambda r, f: (done(r) * n_f + f, 0)),
            pl.BlockSpec((None, N_MOD, D_MODEL), lambda r, f: (build(r) // tiles_per_seq, 0, 0)),
            pl.BlockSpec((None, N_MOD, D_MODEL), lambda r, f: (done(r) // tiles_per_seq, 0, 0)),
            pl.BlockSpec((1, D_MODEL), lambda r, f: (0, 0)),
            pl.BlockSpec((1, D_MODEL), lambda r, f: (0, 0)),
            pl.BlockSpec((None, D_MODEL, tf), lambda r, f: (layer, 0, wf(r, f))),
            pl.BlockSpec((None, tf, D_MODEL), lambda r, f: (layer, wf(r, f), 0)),
        ],
        out_specs=pl.BlockSpec((chunk, D_MODEL),
                               lambda r, f: (done(r) * n_f + jnp.where(r < 2, 0, f), 0)),
        scratch_shapes=[pltpu.VMEM((2, tm, D_MODEL), BF16), pltpu.VMEM((2, tm, D_MODEL), F32)],
        compiler_params=_params("arbitrary", "arbitrary"),
        name="mlp",
    )(x2, x2, mod, mod, g_pre, g_post, w1, w2)


def kernel(x, c, w_ada, b_ada, g_pre_mix, g_post_mix, g_pre_mlp, g_post_mlp, w_in, conv_gdn, gdn_a_log, gdn_dt_bias, gdn_norm, conv_lru, conv_lru_b, lru_w_a, lru_b_a, lru_w_i, lru_b_i, lru_lambda, gla_w_gate, gla_b_gate, gla_norm, w_branch, w_out, w_mlp1, w_mlp2):
    bsz, seq, _ = x.shape
    n_layers = w_in.shape[0]
    assert seq % LRU_TIME_TILE == 0 and x.shape[2] == D_MODEL and bsz % GDN_BATCH_TILE == 0
    row = lambda v: v.reshape(1, D_MODEL)

    mod_all = _ada_mod(c, w_ada, b_ada).reshape(n_layers, bsz, N_MOD, D_MODEL)
    x2 = x.reshape(bsz * seq, D_MODEL)
    w_big, w_small_t = _split_w_in(w_in)
    w_branch_b = w_branch.astype(BF16)
    w_out_b = w_out.astype(BF16)
    w_mlp1_b = w_mlp1.astype(BF16)
    w_mlp2_b = w_mlp2.astype(BF16)
    for l in range(n_layers):
        mod = mod_all[l]
        big, small, small_t, h = _in_proj(x2, mod, row(g_pre_mix[l]), w_big, w_small_t, l, seq)
        oa = _gdn(big, small, small_t, conv_gdn[l], gdn_a_log[l], gdn_dt_bias[l], gdn_norm[l],
                  bsz, seq)
        ob = _lru(big, conv_lru[l], conv_lru_b[l], lru_w_a[l], lru_b_a[l], lru_w_i[l],
                  lru_b_i[l], lru_lambda[l], bsz, seq)
        oc = _gla(big, small, gla_w_gate[l], gla_b_gate[l], gla_norm[l], bsz, seq)
        merged = _merge(h, oa, ob, oc, w_big, w_branch_b, l, seq)
        x2 = _out_proj(merged, x2, mod, row(g_post_mix[l]), w_out_b, l, seq)
        x2 = _mlp(x2, mod, row(g_pre_mlp[l]), row(g_post_mlp[l]), w_mlp1_b, w_mlp2_b, l, seq)
    return x2.reshape(bsz, seq, D_MODEL)
```

```python
import functools

import jax
import jax.numpy as jnp
from jax import lax
from jax.experimental import pallas as pl
from jax.experimental.pallas import tpu as pltpu

F32 = jnp.float32
BF16 = jnp.bfloat16
HIGHEST = lax.Precision.HIGHEST

D_MODEL = 2048
MIX_WIDTH = D_MODEL // 2
N_BRANCH = 3
GDN_DK = 128
GDN_DV = 128
GDN_HEADS = MIX_WIDTH // GDN_DV
GDN_CHUNK = 64
LRU_WIDTH = MIX_WIDTH
LRU_BLOCKS = 8
LRU_BLOCK_W = LRU_WIDTH // LRU_BLOCKS
LRU_C = 8.0
GLA_HEADS = 4
GLA_DV = MIX_WIDTH // GLA_HEADS
GLA_DK = GLA_DV // 2
GLA_GATE_RANK = 16
GLA_GATE_NORM = 16.0
GLA_BASE_CHUNK = 16
CONV_K = 4
D_FF = 4 * D_MODEL
N_MOD = 6
EPS = 1e-6

GDN_QK_W = GDN_HEADS * GDN_DK
GDN_V_W = GDN_HEADS * GDN_DV
GLA_QK_W = GLA_HEADS * GLA_DK
GLA_V_W = GLA_HEADS * GLA_DV
IN_SPLITS = (GDN_QK_W, GDN_QK_W, GDN_V_W, GDN_V_W, GDN_HEADS, GDN_HEADS,
             LRU_WIDTH, LRU_WIDTH,
             GLA_QK_W, GLA_QK_W, GLA_V_W, GLA_V_W, GLA_GATE_RANK,
             N_BRANCH * D_MODEL)
IN_OFFS = tuple(sum(IN_SPLITS[:i]) for i in range(len(IN_SPLITS) + 1))

LANES = 128
SUBLANES = 8
VMEM_LIMIT_BYTES = 56 * 1024 * 1024

BIG_GDN = 0
BIG_LRU = BIG_GDN + 4 * MIX_WIDTH
BIG_GLA = BIG_LRU + 2 * LRU_WIDTH
BIG_GATES = BIG_GLA + 2 * GLA_QK_W + 2 * GLA_V_W
BIG_W = BIG_GATES + N_BRANCH * D_MODEL
SMALL_BETA = 0
SMALL_ALPHA = GDN_HEADS
SMALL_GK = 2 * GDN_HEADS
SMALL_W = LANES

MOD_SH1, MOD_SC1, MOD_GT1, MOD_SH2, MOD_SC2, MOD_GT2 = range(N_MOD)

TIME_TILE = 128
GDN_BATCH_TILE = 2
GLA_BATCH_TILE = 4
LRU_TIME_TILE = 256
HALO = SUBLANES

W_PREP_TILE = 512
W_PREP_RUN1_TILE = IN_OFFS[4] // W_PREP_TILE
W_PREP_RUN2_TILE = (IN_OFFS[4] + IN_OFFS[12] - IN_OFFS[6]) // W_PREP_TILE
W_PREP_SHIFT1 = IN_OFFS[6] - IN_OFFS[4]
W_PREP_SHIFT2 = W_PREP_SHIFT1 + IN_OFFS[13] - IN_OFFS[12]
W_PREP_NEXT = W_PREP_SHIFT2
W_PREP_NARROW1 = slice(IN_OFFS[4] - W_PREP_RUN1_TILE * W_PREP_TILE,
                       IN_OFFS[6] - W_PREP_RUN1_TILE * W_PREP_TILE)
W_PREP_NARROW2 = slice(IN_OFFS[12] - W_PREP_RUN2_TILE * W_PREP_TILE,
                       IN_OFFS[13] - W_PREP_RUN2_TILE * W_PREP_TILE)
assert (W_PREP_NARROW1.start, W_PREP_NARROW1.stop) == (SMALL_BETA, SMALL_GK)
assert (W_PREP_NARROW2.start, W_PREP_NARROW2.stop) == (SMALL_GK, SMALL_GK + GLA_GATE_RANK)
assert W_PREP_TILE % W_PREP_NEXT == 0 and IN_OFFS[-1] % W_PREP_NEXT == 0


def _sigmoid(x):
    return 1.0 / (1.0 + jnp.exp(-x))


def _silu(x):
    return x * _sigmoid(x)


def _softplus(x):
    return jnp.maximum(x, 0.0) + jnp.log(1.0 + jnp.exp(-jnp.abs(x)))


def _log_sigmoid(x):
    return -_softplus(-x)


def _gelu_tanh(x):
    c = 0.7978845608028654
    return 0.5 * x * (1.0 + jnp.tanh(c * (x + 0.044715 * (x * x * x))))


def _rms(x):
    return x * lax.rsqrt(jnp.mean(x * x, axis=-1, keepdims=True) + EPS)


def _dot(a, b, precision=None):
    return jnp.dot(a, b, preferred_element_type=F32, precision=precision)


def _dot_nt(a, b):
    return lax.dot_general(a, b, (((1,), (1,)), ((), ())), preferred_element_type=F32)


def _bdot(a, b):
    return _dot(a.astype(BF16), b.astype(BF16))


def _bdot_nt(a, b):
    return _dot_nt(a.astype(BF16), b.astype(BF16))


def _params(*sem):
    return pltpu.CompilerParams(dimension_semantics=sem, vmem_limit_bytes=VMEM_LIMIT_BYTES)


def _ada_kernel(c_ref, w_ref, b_ref, o_ref):
    sc = _silu(c_ref[...]).astype(BF16)
    o_ref[...] = _dot(sc, w_ref[...].astype(BF16)) + b_ref[...]


def _ada_mod(c, w_ada, b_ada):
    n_layers, _, n_out = w_ada.shape
    bsz = c.shape[0]
    tn = 1024
    return pl.pallas_call(
        _ada_kernel,
        out_shape=jax.ShapeDtypeStruct((n_layers, bsz, n_out), F32),
        grid=(n_layers, n_out // tn),
        in_specs=[
            pl.BlockSpec((bsz, D_MODEL), lambda l, j: (0, 0)),
            pl.BlockSpec((None, D_MODEL, tn), lambda l, j: (l, 0, j)),
            pl.BlockSpec((None, 1, tn), lambda l, j: (l, 0, j)),
        ],
        out_specs=pl.BlockSpec((None, bsz, tn), lambda l, j: (l, 0, j)),
        compiler_params=_params("parallel", "parallel"),
        name="ada_mod",
    )(c, w_ada, b_ada.reshape(n_layers, 1, n_out))


def _w_prep_kernel(a_ref, b_ref, o_ref, wst_ref):
    j = pl.program_id(1)

    def emit(shift):
        src = a_ref[...] if shift == 0 else jnp.concatenate([a_ref[shift:, :], b_ref[:shift, :]],
                                                            axis=0)
        o_ref[...] = src.T.astype(BF16)

    @pl.when(j == 0)
    def _():
        wst_ref[...] = jnp.zeros(wst_ref.shape, BF16)

    @pl.when(j < W_PREP_RUN1_TILE)
    def _():
        emit(0)

    @pl.when((j >= W_PREP_RUN1_TILE) & (j < W_PREP_RUN2_TILE))
    def _():
        emit(W_PREP_SHIFT1)

    @pl.when(j >= W_PREP_RUN2_TILE)
    def _():
        emit(W_PREP_SHIFT2)

    @pl.when(j == W_PREP_RUN1_TILE)
    def _():
        wst_ref[W_PREP_NARROW1, :] = a_ref[W_PREP_NARROW1, :].astype(BF16)

    @pl.when(j == W_PREP_RUN2_TILE)
    def _():
        wst_ref[W_PREP_NARROW2, :] = a_ref[W_PREP_NARROW2, :].astype(BF16)


def _split_w_in(w_in):
    n_layers = w_in.shape[0]
    w_t = jnp.swapaxes(w_in, 1, 2)
    next_per_tile = W_PREP_TILE // W_PREP_NEXT
    w_big, w_small_t = pl.pallas_call(
        _w_prep_kernel,
        out_shape=(jax.ShapeDtypeStruct((n_layers, D_MODEL, BIG_W), BF16),
                   jax.ShapeDtypeStruct((n_layers, SMALL_W, D_MODEL), BF16)),
        grid=(n_layers, BIG_W // W_PREP_TILE),
        in_specs=[
            pl.BlockSpec((None, W_PREP_TILE, D_MODEL), lambda l, j: (l, j, 0)),
            pl.BlockSpec((None, W_PREP_NEXT, D_MODEL), lambda l, j: (l, (j + 1) * next_per_tile, 0)),
        ],
        out_specs=(pl.BlockSpec((None, D_MODEL, W_PREP_TILE), lambda l, j: (l, 0, j)),
                   pl.BlockSpec((None, SMALL_W, D_MODEL), lambda l, j: (l, 0, 0))),
        compiler_params=_params("parallel", "arbitrary"),
        name="w_prep",
    )(w_t, w_t)
    return w_big, w_small_t


def _in_proj_kernel(xn_ref, modn_ref, g_ref, w_ref, wst_ref,
                    big_ref, small_ref, smallt_ref, h_ref, h_scr, *, chunks_per_tile):
    r = pl.program_id(0)
    j = pl.program_id(1)
    chunk = xn_ref.shape[0]

    def build_chunk():
        h = _rms(xn_ref[...]) * g_ref[...]
        h = h * (1.0 + modn_ref[MOD_SC1:MOD_SC1 + 1, :]) + modn_ref[MOD_SH1:MOD_SH1 + 1, :]
        row0 = pl.multiple_of(jnp.minimum(j, chunks_per_tile - 1) * chunk, chunk)
        h_scr[r % 2, pl.ds(row0, chunk), :] = h.astype(BF16)

    @pl.when(r == 0)
    def _():
        build_chunk()

    @pl.when(r > 0)
    def _():
        prev = (r + 1) % 2

        @pl.when(j == 0)
        def _():
            hb = h_scr[prev]
            h_ref[...] = hb
            st = _dot_nt(wst_ref[...], hb)
            smallt_ref[...] = st
            small_ref[...] = st.T

        big_ref[...] = _dot(h_scr[prev], w_ref[...])
        build_chunk()


def _in_proj(x2, mod, g, w_big, w_small_t, layer, seq):
    m = x2.shape[0]
    tm = min(1024, seq)
    tn = 1536
    n_i = m // tm
    n_j = BIG_GATES // tn
    chunks_per_tile = 1 << (min(n_j, SUBLANES).bit_length() - 1)
    chunk = tm // chunks_per_tile
    tiles_per_seq = seq // tm
    build = lambda r: jnp.minimum(r, n_i - 1)
    out = lambda r: jnp.maximum(r - 1, 0)
    return pl.pallas_call(
        functools.partial(_in_proj_kernel, chunks_per_tile=chunks_per_tile),
        out_shape=(jax.ShapeDtypeStruct((m, BIG_GATES), F32),
                   jax.ShapeDtypeStruct((m, SMALL_W), F32),
                   jax.ShapeDtypeStruct((SMALL_W, m), F32),
                   jax.ShapeDtypeStruct((m, D_MODEL), BF16)),
        grid=(n_i + 1, n_j),
        in_specs=[
            pl.BlockSpec((chunk, D_MODEL), lambda r, j: (
                build(r) * chunks_per_tile + jnp.minimum(j, chunks_per_tile - 1), 0)),
            pl.BlockSpec((None, N_MOD, D_MODEL), lambda r, j: (build(r) // tiles_per_seq, 0, 0)),
            pl.BlockSpec((1, D_MODEL), lambda r, j: (0, 0)),
            pl.BlockSpec((None, D_MODEL, tn), lambda r, j: (layer, 0, j)),
            pl.BlockSpec((None, SMALL_W, D_MODEL), lambda r, j: (layer, 0, 0)),
        ],
        out_specs=(pl.BlockSpec((tm, tn), lambda r, j: (out(r), jnp.where(r == 0, 0, j))),
                   pl.BlockSpec((tm, SMALL_W), lambda r, j: (out(r), 0)),
                   pl.BlockSpec((SMALL_W, tm), lambda r, j: (0, out(r))),
                   pl.BlockSpec((tm, D_MODEL), lambda r, j: (out(r), 0))),
        scratch_shapes=[pltpu.VMEM((2, tm, D_MODEL), BF16)],
        compiler_params=_params("arbitrary", "arbitrary"),
        name="in_proj",
    )(x2, mod, g, w_big, w_small_t)


def _causal_conv(xbuf, cw_ref, rows):
    acc = cw_ref[CONV_K - 1:CONV_K, :] * xbuf[HALO:HALO + rows, :]
    for j in range(CONV_K - 1):
        off = HALO - (CONV_K - 1) + j
        acc = acc + cw_ref[j:j + 1, :] * xbuf[off:off + rows, :]
    return acc


def _gdn_kernel(*refs):
    nb = GDN_BATCH_TILE
    q_ref, k_ref, v_ref, z_ref, sm_ref = refs[:5]
    smt_refs = refs[5:5 + nb]
    cw_ref, alr_ref, alc_ref, dtr_ref, dtc_ref, nw_ref, o_ref, xbuf, s_scr = refs[5 + nb:]
    rows = TIME_TILE

    @pl.when(pl.program_id(1) == 0)
    def _():
        xbuf[:, 0:HALO, :] = jnp.zeros((nb, HALO, xbuf.shape[2]), F32)
        s_scr[...] = jnp.zeros(s_scr.shape, F32)

    ri = lax.broadcasted_iota(jnp.int32, (rows, rows), 0)
    ci = lax.broadcasted_iota(jnp.int32, (rows, rows), 1)
    same = (ri // GDN_CHUNK) == (ci // GDN_CHUNK)
    causal = same & (ci <= ri)
    strict = same & (ci < ri)
    cum_c = causal.astype(F32)
    cum_r = (same & (ri <= ci)).astype(F32)
    eye = (ri == ci).astype(F32)
    level_masks = []
    s = 1
    while s < GDN_CHUNK:
        level_masks.append((ri // (2 * s) == ci // (2 * s)) & (ri % (2 * s) >= s) & (ci % (2 * s) < s))
        s *= 2
    zeros_half = jnp.zeros((GDN_CHUNK, GDN_DV), F32)
    col_row = lax.broadcasted_iota(jnp.int32, (1, rows), 1)
    n_chunks = rows // GDN_CHUNK

    units = [(bb, h) for bb in range(nb) for h in range(GDN_HEADS)]
    idx = range(len(units))
    qs, kts, gccs, gcrs, a_qks, a_kks, rhss = [], [], [], [], [], [], []
    for bb in range(nb):
        xb = xbuf.at[bb]
        xb[HALO:HALO + rows, 0:GDN_QK_W] = q_ref[bb]
        xb[HALO:HALO + rows, GDN_QK_W:2 * GDN_QK_W] = k_ref[bb]
        xb[HALO:HALO + rows, 2 * GDN_QK_W:2 * GDN_QK_W + GDN_V_W] = v_ref[bb]
        qkv = _silu(_causal_conv(xb, cw_ref, rows))
        xb[0:HALO, :] = xb[rows:rows + HALO, :]

        sm = sm_ref[bb]
        smt = smt_refs[bb][...]
        beta_c = _sigmoid(sm)
        g_c = -jnp.exp(alr_ref[...]) * _softplus(sm + dtr_ref[...])
        g_r = -jnp.exp(alc_ref[...]) * _softplus(smt + dtc_ref[...])
        gc_c = _dot(cum_c, g_c, precision=HIGHEST)
        gc_r = _dot(g_r, cum_r, precision=HIGHEST)

        for h in range(GDN_HEADS):
            qh = qkv[:, h * GDN_DK:(h + 1) * GDN_DK]
            kh = qkv[:, GDN_QK_W + h * GDN_DK:GDN_QK_W + (h + 1) * GDN_DK]
            vh = qkv[:, 2 * GDN_QK_W + h * GDN_DV:2 * GDN_QK_W + (h + 1) * GDN_DV]
            qh = qh * lax.rsqrt(jnp.sum(qh * qh, axis=-1, keepdims=True) + EPS) * (GDN_DK ** -0.5)
            kh = kh * lax.rsqrt(jnp.sum(kh * kh, axis=-1, keepdims=True) + EPS)
            bc = beta_c[:, SMALL_BETA + h:SMALL_BETA + h + 1]
            gcc = gc_c[:, SMALL_ALPHA + h:SMALL_ALPHA + h + 1]
            gcr = gc_r[SMALL_ALPHA + h:SMALL_ALPHA + h + 1, :]
            e = jnp.exp(jnp.where(causal, gcc - gcr, 0.0))
            kb = kh * bc
            kt = kh.T
            qk_kk = _bdot(jnp.concatenate([qh, kb], axis=0), kt)
            eg = jnp.exp(gcc)
            qs.append(qh * eg)
            kts.append(kt)
            gccs.append(gcc)
            gcrs.append(gcr)
            a_qks.append(jnp.where(causal, qk_kk[:rows] * e, 0.0))
            a_kks.append(jnp.where(strict, qk_kk[rows:] * e, 0.0))
            rhss.append(jnp.concatenate([vh * bc, kb * eg], axis=1))

    tinv = [eye - jnp.where(level_masks[0], a_kks[u], 0.0) for u in idx]
    for lm in level_masks[1:]:
        tb = [tinv[u].astype(BF16) for u in idx]
        et = [_dot(jnp.where(lm, a_kks[u], 0.0).astype(BF16), tb[u]) for u in idx]
        tinv = [tinv[u] - _dot(tb[u], et[u].astype(BF16)) for u in idx]
    uws = [_bdot(tinv[u], rhss[u]) for u in idx]

    states = [s_scr[bb, h] for bb, h in units]
    outs = [[] for _ in idx]
    for c in range(n_chunks):
        lo = c * GDN_CHUNK
        hi = lo + GDN_CHUNK
        in_chunk = (col_row >= lo) & (col_row < hi)
        rs = [_bdot(jnp.concatenate([uws[u][lo:hi, GDN_DV:], qs[u][lo:hi]], axis=0), states[u])
              for u in idx]
        v_pads = []
        for u in idx:
            pads = [zeros_half] * n_chunks
            pads[c] = uws[u][lo:hi, :GDN_DV] - rs[u][:GDN_CHUNK]
            v_pads.append(jnp.concatenate(pads, axis=0))
        new_states = []
        for u in idx:
            g_last = gccs[u][hi - 1:hi, :]
            tail = jnp.where(in_chunk, jnp.exp(jnp.where(in_chunk, g_last - gcrs[u], 0.0)), 0.0)
            new_states.append(states[u] * jnp.exp(g_last) + _bdot(kts[u] * tail, v_pads[u]))
        states = new_states
        for u in idx:
            outs[u].append(rs[u][GDN_CHUNK:] + _bdot(a_qks[u][lo:hi, :], v_pads[u]))
    for u, (bb, h) in enumerate(units):
        s_scr[bb, h] = states[u]
        o = _rms(jnp.concatenate(outs[u], axis=0)) * nw_ref[...]
        zg = _silu(z_ref[bb, :, h * GDN_DV:(h + 1) * GDN_DV])
        o_ref[bb, :, h * GDN_DV:(h + 1) * GDN_DV] = (o * zg).astype(BF16)


def _gdn(big, small, small_t, conv_w, a_log, dt_bias, norm_w, bsz, seq):
    nb = GDN_BATCH_TILE
    nt = seq // TIME_TILE
    qkv_w = 2 * GDN_QK_W + GDN_V_W
    big3 = big.reshape(bsz, seq, big.shape[1])
    small3 = small.reshape(bsz, seq, SMALL_W)

    def pad_row(v):
        return jnp.zeros((1, SMALL_W), F32).at[0, SMALL_ALPHA:SMALL_ALPHA + GDN_HEADS].set(v)

    alr = pad_row(a_log)
    dtr = pad_row(dt_bias)
    col = lambda cb: pl.BlockSpec((nb, TIME_TILE, MIX_WIDTH), lambda g, t: (g, t, cb))
    const = lambda shape: pl.BlockSpec(shape, lambda g, t: (0, 0))
    small_t_spec = lambda bb: pl.BlockSpec(
        (SMALL_W, TIME_TILE), lambda g, t: (0, (g * nb + bb) * nt + t))
    out = pl.pallas_call(
        _gdn_kernel,
        out_shape=jax.ShapeDtypeStruct((bsz, seq, MIX_WIDTH), BF16),
        grid=(bsz // nb, nt),
        in_specs=[
            col(0), col(1), col(2), col(3),
            pl.BlockSpec((nb, TIME_TILE, SMALL_W), lambda g, t: (g, t, 0)),
            *[small_t_spec(bb) for bb in range(nb)],
            const((CONV_K, qkv_w)),
            const((1, SMALL_W)), const((SMALL_W, 1)),
            const((1, SMALL_W)), const((SMALL_W, 1)),
            const((1, GDN_DV)),
        ],
        out_specs=pl.BlockSpec((nb, TIME_TILE, MIX_WIDTH), lambda g, t: (g, t, 0)),
        scratch_shapes=[pltpu.VMEM((nb, HALO + TIME_TILE, qkv_w), F32),
                        pltpu.VMEM((nb, GDN_HEADS, GDN_DK, GDN_DV), F32)],
        compiler_params=_params("parallel", "arbitrary"),
        name="gdn",
    )(big3, big3, big3, big3, small3, *([small_t] * nb), conv_w, alr, alr.T, dtr, dtr.T,
      norm_w.reshape(1, GDN_DV))
    return out.reshape(bsz * seq, MIX_WIDTH)


def _lru_kernel(x_ref, y_ref, cw_ref, cb_ref, wa_ref, ba_ref, wi_ref, bi_ref, lam_ref,
                o_ref, xbuf, h_scr):
    rows = LRU_TIME_TILE
    first = pl.program_id(1) == 0

    @pl.when(first)
    def _():
        xbuf[0:HALO, :] = jnp.zeros((HALO, LRU_WIDTH), F32)
        h_scr[...] = jnp.zeros(h_scr.shape, F32)

    xbuf[HALO:HALO + rows, :] = x_ref[...]
    xl = _causal_conv(xbuf, cw_ref, rows) + cb_ref[...]
    xbuf[0:HALO, :] = xbuf[rows:rows + HALO, :]

    xlb = xl.astype(BF16)
    r_parts = []
    i_parts = []
    for k in range(LRU_BLOCKS):
        blk = xlb[:, k * LRU_BLOCK_W:(k + 1) * LRU_BLOCK_W]
        r_parts.append(_dot(blk, wa_ref[k]))
        i_parts.append(_dot(blk, wi_ref[k]))
    r = _sigmoid(jnp.concatenate(r_parts, axis=1) + ba_ref[...])
    ig = _sigmoid(jnp.concatenate(i_parts, axis=1) + bi_ref[...])
    log_a = -LRU_C * r * _softplus(-lam_ref[...])
    a = jnp.exp(log_a)
    mult = jnp.sqrt(-jnp.tanh(log_a) * (a * a + 1.0))
    row = lax.broadcasted_iota(jnp.int32, (rows, 1), 0)
    mult = jnp.where(first & (row == 0), 1.0, mult)
    b = mult * ig * xl

    row8 = row[:SUBLANES]
    shift = 1
    while shift < SUBLANES:
        a_roll = pltpu.roll(a, shift, 0)
        b_roll = pltpu.roll(b, shift, 0)
        keep = row8 >= shift
        a_prev = jnp.concatenate([jnp.where(keep, a_roll[:SUBLANES], 1.0), a_roll[SUBLANES:]], axis=0)
        b_prev = jnp.concatenate([jnp.where(keep, b_roll[:SUBLANES], 0.0), b_roll[SUBLANES:]], axis=0)
        b = a * b_prev + b
        a = a * a_prev
        shift *= 2
    while shift < rows:
        b = jnp.concatenate([b[:shift], a[shift:] * b[:rows - shift] + b[shift:]], axis=0)
        a = jnp.concatenate([a[:shift], a[shift:] * a[:rows - shift]], axis=0)
        shift *= 2
    hs = a * h_scr[0:1, :] + b
    h_scr[0:1, :] = hs[rows - 1:rows, :]
    o_ref[...] = (_gelu_tanh(y_ref[...]) * hs).astype(BF16)


def _lru(big, conv_w, conv_b, w_a, b_a, w_i, b_i, lam, bsz, seq):
    m = bsz * seq
    nt = seq // LRU_TIME_TILE
    base = BIG_LRU // LRU_WIDTH
    row1 = lambda v: v.reshape(1, LRU_WIDTH)
    const2 = lambda shape: pl.BlockSpec(shape, lambda b, t: (0, 0))
    const3 = lambda shape: pl.BlockSpec(shape, lambda b, t: (0, 0, 0))
    return pl.pallas_call(
        _lru_kernel,
        out_shape=jax.ShapeDtypeStruct((m, LRU_WIDTH), BF16),
        grid=(bsz, nt),
        in_specs=[
            pl.BlockSpec((LRU_TIME_TILE, LRU_WIDTH), lambda b, t: (b * nt + t, base)),
            pl.BlockSpec((LRU_TIME_TILE, LRU_WIDTH), lambda b, t: (b * nt + t, base + 1)),
            const2((CONV_K, LRU_WIDTH)), const2((1, LRU_WIDTH)),
            const3((LRU_BLOCKS, LRU_BLOCK_W, LRU_BLOCK_W)), const2((1, LRU_WIDTH)),
            const3((LRU_BLOCKS, LRU_BLOCK_W, LRU_BLOCK_W)), const2((1, LRU_WIDTH)),
            const2((1, LRU_WIDTH)),
        ],
        out_specs=pl.BlockSpec((LRU_TIME_TILE, LRU_WIDTH), lambda b, t: (b * nt + t, 0)),
        scratch_shapes=[pltpu.VMEM((HALO + LRU_TIME_TILE, LRU_WIDTH), F32),
                        pltpu.VMEM((SUBLANES, LRU_WIDTH), F32)],
        compiler_params=_params("parallel", "arbitrary"),
        name="rg_lru",
    )(big, big, conv_w, row1(conv_b), w_a.astype(BF16), row1(b_a), w_i.astype(BF16),
      row1(b_i), row1(lam))


def _gla_kernel(q_ref, k_ref, v_ref, z_ref, sm_ref, wg_ref, bg_ref, nw_ref, o_ref, st_scr):
    rows = TIME_TILE

    @pl.when(pl.program_id(1) == 0)
    def _():
        st_scr[...] = jnp.zeros(st_scr.shape, F32)

    nb = GLA_BATCH_TILE
    ri = lax.broadcasted_iota(jnp.int32, (rows, rows), 0)
    ci = lax.broadcasted_iota(jnp.int32, (rows, rows), 1)
    tri = (ci <= ri).astype(F32)
    bcums = []
    for bb in range(nb):
        log_a = _log_sigmoid(_dot(sm_ref[bb].astype(BF16), wg_ref[...]) + bg_ref[...]) / GLA_GATE_NORM
        bcums.append(_dot(tri, log_a, precision=HIGHEST))
    row = lax.broadcasted_iota(jnp.int32, (rows, 1), 0)

    def boundary_rows(bh, width, offset, first_zero):
        out = jnp.zeros_like(bh) if first_zero else None
        for g in range(rows // width):
            src = g * width + offset
            if src < 0:
                continue
            val = jnp.broadcast_to(bh[src:src + 1, :], bh.shape)
            out = val if out is None else jnp.where(row // width == g, val, out)
        return out

    units = [(bb, h) for bb in range(nb) for h in range(GLA_HEADS)]
    heads = range(len(units))
    bhs = [bcums[bb][:, h * GLA_DK:(h + 1) * GLA_DK] for bb, h in units]
    qh = [q_ref[bb, :, h * GLA_DK:(h + 1) * GLA_DK] * (GLA_DK ** -0.5) for bb, h in units]
    kh = [k_ref[bb, :, h * GLA_DK:(h + 1) * GLA_DK] for bb, h in units]
    vh = [v_ref[bb, :, h * GLA_DV:(h + 1) * GLA_DV] for bb, h in units]
    b_loc = [bhs[h] - boundary_rows(bhs[h], GLA_BASE_CHUNK, -1, True) for h in heads]
    base_mask = (ri // GLA_BASE_CHUNK == ci // GLA_BASE_CHUNK) & (ci <= ri)
    att = [jnp.where(base_mask,
                     _bdot_nt(qh[h] * jnp.exp(b_loc[h]), kh[h] * jnp.exp(-b_loc[h])), 0.0)
           for h in heads]
    s = GLA_BASE_CHUNK
    while s < rows:
        mask = ((ri // (2 * s) == ci // (2 * s)) & (ri % (2 * s) >= s) & (ci % (2 * s) < s))
        refs = [boundary_rows(bhs[h], 2 * s, s - 1, False) for h in heads]
        cross = [_bdot_nt(qh[h] * jnp.exp(jnp.minimum(bhs[h] - refs[h], 0.0)),
                          kh[h] * jnp.exp(jnp.minimum(refs[h] - bhs[h], 0.0))) for h in heads]
        att = [att[h] + jnp.where(mask, cross[h], 0.0) for h in heads]
        s *= 2
    sts = [st_scr[bb, h] for bb, h in units]
    o_inter = [_bdot_nt(qh[u] * jnp.exp(bhs[u]), sts[u]) for u in heads]
    o_intra = [_bdot(att[u], vh[u]) for u in heads]
    b_end = [bhs[u][rows - 1:rows, :] for u in heads]
    upd = [_bdot(vh[u].T, kh[u] * jnp.exp(b_end[u] - bhs[u])) for u in heads]
    for u, (bb, h) in enumerate(units):
        st_scr[bb, h] = sts[u] * jnp.exp(b_end[u]) + upd[u]
        o = _rms(o_intra[u] + o_inter[u]) * nw_ref[...]
        zg = _silu(z_ref[bb, :, h * GLA_DV:(h + 1) * GLA_DV])
        o_ref[bb, :, h * GLA_DV:(h + 1) * GLA_DV] = (o * zg).astype(BF16)


def _gla(big, small, w_gate, b_gate, norm_w, bsz, seq):
    nb = GLA_BATCH_TILE
    nt = seq // TIME_TILE
    big3 = big.reshape(bsz, seq, big.shape[1])
    small3 = small.reshape(bsz, seq, SMALL_W)
    wg = jnp.zeros((SMALL_W, GLA_QK_W), F32).at[SMALL_GK:SMALL_GK + GLA_GATE_RANK].set(w_gate)
    qk_base = BIG_GLA // GLA_QK_W
    v_base = (BIG_GLA + 2 * GLA_QK_W) // GLA_V_W
    const = lambda shape: pl.BlockSpec(shape, lambda g, t: (0, 0))
    tile = lambda width, cb: pl.BlockSpec((nb, TIME_TILE, width), lambda g, t: (g, t, cb))
    out = pl.pallas_call(
        _gla_kernel,
        out_shape=jax.ShapeDtypeStruct((bsz, seq, MIX_WIDTH), BF16),
        grid=(bsz // nb, nt),
        in_specs=[
            tile(GLA_QK_W, qk_base), tile(GLA_QK_W, qk_base + 1),
            tile(GLA_V_W, v_base), tile(GLA_V_W, v_base + 1),
            tile(SMALL_W, 0),
            const((SMALL_W, GLA_QK_W)), const((1, GLA_QK_W)), const((1, GLA_DV)),
        ],
        out_specs=tile(MIX_WIDTH, 0),
        scratch_shapes=[pltpu.VMEM((nb, GLA_HEADS, GLA_DV, GLA_DK), F32)],
        compiler_params=_params("parallel", "arbitrary"),
        name="gla",
    )(big3, big3, big3, big3, small3, wg.astype(BF16), b_gate.reshape(1, GLA_QK_W),
      norm_w.reshape(1, GLA_DV))
    return out.reshape(bsz * seq, MIX_WIDTH)


def _merge_kernel(h_ref, wga_ref, wgb_ref, wgc_ref, oa_ref, ob_ref, oc_ref, wb_ref, o_ref):
    h = h_ref[...]
    acc = _sigmoid(_dot(h, wga_ref[...])) * _dot(oa_ref[...], wb_ref[0])
    acc = acc + _sigmoid(_dot(h, wgb_ref[...])) * _dot(ob_ref[...], wb_ref[1])
    acc = acc + _sigmoid(_dot(h, wgc_ref[...])) * _dot(oc_ref[...], wb_ref[2])
    o_ref[...] = acc.astype(BF16)


def _merge(h, oa, ob, oc, w_big, w_branch, layer, seq):
    m = oa.shape[0]
    tm = min(1024, seq)
    tn = 512
    gate_w = lambda n: pl.BlockSpec(
        (None, D_MODEL, tn), lambda i, j: (layer, 0, (BIG_GATES + n * D_MODEL) // tn + j))
    branch = pl.BlockSpec((tm, MIX_WIDTH), lambda i, j: (i, 0))
    return pl.pallas_call(
        _merge_kernel,
        out_shape=jax.ShapeDtypeStruct((m, D_MODEL), BF16),
        grid=(m // tm, D_MODEL // tn),
        in_specs=[pl.BlockSpec((tm, D_MODEL), lambda i, j: (i, 0)),
                  gate_w(0), gate_w(1), gate_w(2), branch, branch, branch,
                  pl.BlockSpec((None, N_BRANCH, MIX_WIDTH, tn), lambda i, j: (layer, 0, 0, j))],
        out_specs=pl.BlockSpec((tm, tn), lambda i, j: (i, j)),
        compiler_params=_params("parallel", "arbitrary"),
        name="merge",
    )(h, w_big, w_big, w_big, oa, ob, oc, w_branch)


def _out_proj_kernel(m_ref, x_ref, mod_ref, g_ref, w_ref, o_ref):
    y = _dot(m_ref[...], w_ref[...])
    o_ref[...] = x_ref[...] + mod_ref[MOD_GT1:MOD_GT1 + 1, :] * (_rms(y) * g_ref[...])


def _out_proj(merged, x2, mod, g, w_out, layer, seq):
    m = x2.shape[0]
    tm = min(512, seq)
    tiles_per_seq = seq // tm
    return pl.pallas_call(
        _out_proj_kernel,
        out_shape=jax.ShapeDtypeStruct((m, D_MODEL), F32),
        grid=(m // tm,),
        in_specs=[
            pl.BlockSpec((tm, D_MODEL), lambda i: (i, 0)),
            pl.BlockSpec((tm, D_MODEL), lambda i: (i, 0)),
            pl.BlockSpec((None, N_MOD, D_MODEL), lambda i: (i // tiles_per_seq, 0, 0)),
            pl.BlockSpec((1, D_MODEL), lambda i: (0, 0)),
            pl.BlockSpec((None, D_MODEL, D_MODEL), lambda i: (layer, 0, 0)),
        ],
        out_specs=pl.BlockSpec((tm, D_MODEL), lambda i: (i, 0)),
        compiler_params=_params("parallel"),
        name="out_proj",
    )(merged, x2, mod, g, w_out)


def _mlp_kernel(xb_ref, xe_ref, modb_ref, mode_ref, gpre_ref, gpost_ref, w1_ref, w2_ref, o_ref,
                h_scr, acc_scr, *, n_tiles):
    r = pl.program_id(0)
    f = pl.program_id(1)
    chunk = xb_ref.shape[0]
    rows = pl.ds(pl.multiple_of(f * chunk, chunk), chunk)

    def build():
        h = _rms(xb_ref[...]) * gpre_ref[...]
        h = h * (1.0 + modb_ref[MOD_SC2:MOD_SC2 + 1, :]) + modb_ref[MOD_SH2:MOD_SH2 + 1, :]
        h_scr[r % 2, rows, :] = h.astype(BF16)

    def finish():
        y = _rms(acc_scr[r % 2, rows, :]) * gpost_ref[...]
        o_ref[...] = xe_ref[...] + mode_ref[MOD_GT2:MOD_GT2 + 1, :] * y

    @pl.when(r == 0)
    def _():
        @pl.when(f == 0)
        def _():
            acc_scr[...] = jnp.zeros(acc_scr.shape, F32)

        build()
        o_ref[...] = jnp.zeros(o_ref.shape, F32)

    @pl.when((r >= 1) & (r <= n_tiles))
    def _():
        prev = (r + 1) % 2

        @pl.when(f == 0)
        def _():
            acc_scr[prev] = jnp.zeros(acc_scr.shape[1:], F32)

        finish()
        a = jnp.maximum(_dot(h_scr[prev], w1_ref[...]), 0.0)
        acc_scr[prev] += _dot((a * a).astype(BF16), w2_ref[...])
        build()

    @pl.when(r == n_tiles + 1)
    def _():
        finish()


def _mlp(x2, mod, g_pre, g_post, w1, w2, layer, seq):
    m = x2.shape[0]
    tm = min(1024, seq)
    tf = 1024
    n_f = D_FF // tf
    n_i = m // tm
    chunk = tm // n_f
    tiles_per_seq = seq // tm
    build = lambda r: jnp.minimum(r, n_i - 1)
    done = lambda r: jnp.clip(r - 2, 0, n_i - 1)
    wf = lambda r, f: jnp.where(r == 0, 0, jnp.where(r == n_i + 1, n_f - 1, f))
    return pl.pallas_call(
        functools.partial(_mlp_kernel, n_tiles=n_i),
        out_shape=jax.ShapeDtypeStruct((m, D_MODEL), F32),
        grid=(n_i + 2, n_f),
        in_specs=[
            pl.BlockSpec((chunk, D_MODEL), lambda r, f: (build(r) * n_f + f, 0)),
            pl.BlockSpec((chunk, D_MODEL), lambda r, f: (done(r) * n_f + f, 0)),
            pl.BlockSpec((None, N_MOD, D_MODEL), lambda r, f: (build(r) // tiles_per_seq, 0, 0)),
            pl.BlockSpec((None, N_MOD, D_MODEL), lambda r, f: (done(r) // tiles_per_seq, 0, 0)),
            pl.BlockSpec((1, D_MODEL), lambda r, f: (0, 0)),
            pl.BlockSpec((1, D_MODEL), lambda r, f: (0, 0)),
            pl.BlockSpec((None, D_MODEL, tf), lambda r, f: (layer, 0, wf(r, f))),
            pl.BlockSpec((None, tf, D_MODEL), lambda r, f: (layer, wf(r, f), 0)),
        ],
        out_specs=pl.BlockSpec((chunk, D_MODEL),
                               lambda r, f: (done(r) * n_f + jnp.where(r < 2, 0, f), 0)),
        scratch_shapes=[pltpu.VMEM((2, tm, D_MODEL), BF16), pltpu.VMEM((2, tm, D_MODEL), F32)],
        compiler_params=_params("arbitrary", "arbitrary"),
        name="mlp",
    )(x2, x2, mod, mod, g_pre, g_post, w1, w2)


def kernel(x, c, w_ada, b_ada, g_pre_mix, g_post_mix, g_pre_mlp, g_post_mlp, w_in, conv_gdn, gdn_a_log, gdn_dt_bias, gdn_norm, conv_lru, conv_lru_b, lru_w_a, lru_b_a, lru_w_i, lru_b_i, lru_lambda, gla_w_gate, gla_b_gate, gla_norm, w_branch, w_out, w_mlp1, w_mlp2):
    bsz, seq, _ = x.shape
    n_layers = w_in.shape[0]
    assert seq % LRU_TIME_TILE == 0 and x.shape[2] == D_MODEL and bsz % GDN_BATCH_TILE == 0
    row = lambda v: v.reshape(1, D_MODEL)

    mod_all = _ada_mod(c, w_ada, b_ada).reshape(n_layers, bsz, N_MOD, D_MODEL)
    x2 = x.reshape(bsz * seq, D_MODEL)
    w_big, w_small_t = _split_w_in(w_in)
    w_branch_b = w_branch.astype(BF16)
    w_out_b = w_out.astype(BF16)
    w_mlp1_b = w_mlp1.astype(BF16)
    w_mlp2_b = w_mlp2.astype(BF16)
    for l in range(n_layers):
        mod = mod_all[l]
        big, small, small_t, h = _in_proj(x2, mod, row(g_pre_mix[l]), w_big, w_small_t, l, seq)
        oa = _gdn(big, small, small_t, conv_gdn[l], gdn_a_log[l], gdn_dt_bias[l], gdn_norm[l],
                  bsz, seq)
        ob = _lru(big, conv_lru[l], conv_lru_b[l], lru_w_a[l], lru_b_a[l], lru_w_i[l],
                  lru_b_i[l], lru_lambda[l], bsz, seq)
        oc = _gla(big, small, gla_w_gate[l], gla_b_gate[l], gla_norm[l], bsz, seq)
        merged = _merge(h, oa, ob, oc, w_big, w_branch_b, l, seq)
        x2 = _out_proj(merged, x2, mod, row(g_post_mix[l]), w_out_b, l, seq)
        x2 = _mlp(x2, mod, row(g_pre_mlp[l]), row(g_post_mlp[l]), w_mlp1_b, w_mlp2_b, l, seq)
    return x2.reshape(bsz, seq, D_MODEL)
```

```python
import functools

import jax
import jax.numpy as jnp
from jax import lax
from jax.experimental import pallas as pl
from jax.experimental.pallas import tpu as pltpu

F32 = jnp.float32
BF16 = jnp.bfloat16
HIGHEST = lax.Precision.HIGHEST

D_MODEL = 2048
MIX_WIDTH = D_MODEL // 2
N_BRANCH = 3
GDN_DK = 128
GDN_DV = 128
GDN_HEADS = MIX_WIDTH // GDN_DV
GDN_CHUNK = 64
LRU_WIDTH = MIX_WIDTH
LRU_BLOCKS = 8
LRU_BLOCK_W = LRU_WIDTH // LRU_BLOCKS
LRU_C = 8.0
GLA_HEADS = 4
GLA_DV = MIX_WIDTH // GLA_HEADS
GLA_DK = GLA_DV // 2
GLA_GATE_RANK = 16
GLA_GATE_NORM = 16.0
GLA_BASE_CHUNK = 16
CONV_K = 4
D_FF = 4 * D_MODEL
N_MOD = 6
EPS = 1e-6

GDN_QK_W = GDN_HEADS * GDN_DK
GDN_V_W = GDN_HEADS * GDN_DV
GLA_QK_W = GLA_HEADS * GLA_DK
GLA_V_W = GLA_HEADS * GLA_DV
IN_SPLITS = (GDN_QK_W, GDN_QK_W, GDN_V_W, GDN_V_W, GDN_HEADS, GDN_HEADS,
             LRU_WIDTH, LRU_WIDTH,
             GLA_QK_W, GLA_QK_W, GLA_V_W, GLA_V_W, GLA_GATE_RANK,
             N_BRANCH * D_MODEL)
IN_OFFS = tuple(sum(IN_SPLITS[:i]) for i in range(len(IN_SPLITS) + 1))

LANES = 128
SUBLANES = 8
VMEM_LIMIT_BYTES = 56 * 1024 * 1024

BIG_GDN = 0
BIG_LRU = BIG_GDN + 4 * MIX_WIDTH
BIG_GLA = BIG_LRU + 2 * LRU_WIDTH
BIG_GATES = BIG_GLA + 2 * GLA_QK_W + 2 * GLA_V_W
BIG_W = BIG_GATES + N_BRANCH * D_MODEL
SMALL_BETA = 0
SMALL_ALPHA = GDN_HEADS
SMALL_GK = 2 * GDN_HEADS
SMALL_W = LANES

MOD_SH1, MOD_SC1, MOD_GT1, MOD_SH2, MOD_SC2, MOD_GT2 = range(N_MOD)

ADA_COL_TILE = 1024
PROJ_ROW_TILE = 1024
PROJ_COL_TILE = 1536
MERGE_ROW_TILE = 1024
MERGE_COL_TILE = 512
OUT_PROJ_ROW_TILE = 512
MLP_ROW_TILE = 1024
MLP_FF_TILE = 1024
TIME_TILE = 128
GDN_BATCH_TILE = 2
GLA_BATCH_TILE = 4
LRU_TIME_TILE = 256
HALO = SUBLANES

W_PREP_TILE = 512
W_PREP_RUN1_TILE = IN_OFFS[4] // W_PREP_TILE
W_PREP_RUN2_TILE = (IN_OFFS[4] + IN_OFFS[12] - IN_OFFS[6]) // W_PREP_TILE
W_PREP_SHIFT1 = IN_OFFS[6] - IN_OFFS[4]
W_PREP_SHIFT2 = W_PREP_SHIFT1 + IN_OFFS[13] - IN_OFFS[12]
W_PREP_NEXT = W_PREP_SHIFT2
W_PREP_NARROW1 = slice(IN_OFFS[4] - W_PREP_RUN1_TILE * W_PREP_TILE,
                       IN_OFFS[6] - W_PREP_RUN1_TILE * W_PREP_TILE)
W_PREP_NARROW2 = slice(IN_OFFS[12] - W_PREP_RUN2_TILE * W_PREP_TILE,
                       IN_OFFS[13] - W_PREP_RUN2_TILE * W_PREP_TILE)
assert (W_PREP_NARROW1.start, W_PREP_NARROW1.stop) == (SMALL_BETA, SMALL_GK)
assert (W_PREP_NARROW2.start, W_PREP_NARROW2.stop) == (SMALL_GK, SMALL_GK + GLA_GATE_RANK)
assert W_PREP_TILE % W_PREP_NEXT == 0 and IN_OFFS[-1] % W_PREP_NEXT == 0


def _sigmoid(x):
    return 1.0 / (1.0 + jnp.exp(-x))


def _silu(x):
    return x * _sigmoid(x)


def _softplus(x):
    return jnp.maximum(x, 0.0) + jnp.log(1.0 + jnp.exp(-jnp.abs(x)))


def _log_sigmoid(x):
    return -_softplus(-x)


def _gelu_tanh(x):
    c = 0.7978845608028654
    return 0.5 * x * (1.0 + jnp.tanh(c * (x + 0.044715 * (x * x * x))))


def _rms(x):
    return x * lax.rsqrt(jnp.mean(x * x, axis=-1, keepdims=True) + EPS)


def _dot(a, b, precision=None):
    return jnp.dot(a, b, preferred_element_type=F32, precision=precision)


def _dot_nt(a, b):
    return lax.dot_general(a, b, (((1,), (1,)), ((), ())), preferred_element_type=F32)


def _bdot(a, b):
    return _dot(a.astype(BF16), b.astype(BF16))


def _bdot_nt(a, b):
    return _dot_nt(a.astype(BF16), b.astype(BF16))


def _params(*sem):
    return pltpu.CompilerParams(dimension_semantics=sem, vmem_limit_bytes=VMEM_LIMIT_BYTES)


def _ada_kernel(c_ref, w_ref, b_ref, o_ref):
    sc = _silu(c_ref[...]).astype(BF16)
    o_ref[...] = _dot(sc, w_ref[...].astype(BF16)) + b_ref[...]


def _ada_mod(c, w_ada, b_ada):
    n_layers, _, n_out = w_ada.shape
    bsz = c.shape[0]
    tn = ADA_COL_TILE
    return pl.pallas_call(
        _ada_kernel,
        out_shape=jax.ShapeDtypeStruct((n_layers, bsz, n_out), F32),
        grid=(n_layers, n_out // tn),
        in_specs=[
            pl.BlockSpec((bsz, D_MODEL), lambda l, j: (0, 0)),
            pl.BlockSpec((None, D_MODEL, tn), lambda l, j: (l, 0, j)),
            pl.BlockSpec((None, 1, tn), lambda l, j: (l, 0, j)),
        ],
        out_specs=pl.BlockSpec((None, bsz, tn), lambda l, j: (l, 0, j)),
        compiler_params=_params("parallel", "parallel"),
        name="ada_mod",
    )(c, w_ada, b_ada.reshape(n_layers, 1, n_out))


def _w_prep_kernel(a_ref, b_ref, o_ref, wst_ref):
    j = pl.program_id(1)

    def emit(shift):
        src = a_ref[...] if shift == 0 else jnp.concatenate([a_ref[shift:, :], b_ref[:shift, :]],
                                                            axis=0)
        o_ref[...] = src.T.astype(BF16)

    @pl.when(j == 0)
    def _():
        wst_ref[...] = jnp.zeros(wst_ref.shape, BF16)

    @pl.when(j < W_PREP_RUN1_TILE)
    def _():
        emit(0)

    @pl.when((j >= W_PREP_RUN1_TILE) & (j < W_PREP_RUN2_TILE))
    def _():
        emit(W_PREP_SHIFT1)

    @pl.when(j >= W_PREP_RUN2_TILE)
    def _():
        emit(W_PREP_SHIFT2)

    @pl.when(j == W_PREP_RUN1_TILE)
    def _():
        wst_ref[W_PREP_NARROW1, :] = a_ref[W_PREP_NARROW1, :].astype(BF16)

    @pl.when(j == W_PREP_RUN2_TILE)
    def _():
        wst_ref[W_PREP_NARROW2, :] = a_ref[W_PREP_NARROW2, :].astype(BF16)


def _split_w_in(w_in):
    n_layers = w_in.shape[0]
    w_t = jnp.swapaxes(w_in, 1, 2)
    next_per_tile = W_PREP_TILE // W_PREP_NEXT
    w_big, w_small_t = pl.pallas_call(
        _w_prep_kernel,
        out_shape=(jax.ShapeDtypeStruct((n_layers, D_MODEL, BIG_W), BF16),
                   jax.ShapeDtypeStruct((n_layers, SMALL_W, D_MODEL), BF16)),
        grid=(n_layers, BIG_W // W_PREP_TILE),
        in_specs=[
            pl.BlockSpec((None, W_PREP_TILE, D_MODEL), lambda l, j: (l, j, 0)),
            pl.BlockSpec((None, W_PREP_NEXT, D_MODEL), lambda l, j: (l, (j + 1) * next_per_tile, 0)),
        ],
        out_specs=(pl.BlockSpec((None, D_MODEL, W_PREP_TILE), lambda l, j: (l, 0, j)),
                   pl.BlockSpec((None, SMALL_W, D_MODEL), lambda l, j: (l, 0, 0))),
        compiler_params=_params("parallel", "arbitrary"),
        name="w_prep",
    )(w_t, w_t)
    return w_big, w_small_t


def _in_proj_kernel(xn_ref, modn_ref, g_ref, w_ref, wst_ref,
                    big_ref, small_ref, smallt_ref, h_ref, h_scr, *, chunks_per_tile):
    r = pl.program_id(0)
    j = pl.program_id(1)
    chunk = xn_ref.shape[0]

    def build_chunk():
        h = _rms(xn_ref[...]) * g_ref[...]
        h = h * (1.0 + modn_ref[MOD_SC1:MOD_SC1 + 1, :]) + modn_ref[MOD_SH1:MOD_SH1 + 1, :]
        row0 = pl.multiple_of(jnp.minimum(j, chunks_per_tile - 1) * chunk, chunk)
        h_scr[r % 2, pl.ds(row0, chunk), :] = h.astype(BF16)

    @pl.when(r == 0)
    def _():
        build_chunk()

    @pl.when(r > 0)
    def _():
        prev = (r + 1) % 2

        @pl.when(j == 0)
        def _():
            hb = h_scr[prev]
            h_ref[...] = hb
            st = _dot_nt(wst_ref[...], hb)
            smallt_ref[...] = st
            small_ref[...] = st.T

        big_ref[...] = _dot(h_scr[prev], w_ref[...])
        build_chunk()


def _in_proj(x2, mod, g, w_big, w_small_t, layer, seq):
    m = x2.shape[0]
    tm = min(PROJ_ROW_TILE, seq)
    tn = PROJ_COL_TILE
    n_i = m // tm
    n_j = BIG_GATES // tn
    chunks_per_tile = 1 << (min(n_j, SUBLANES).bit_length() - 1)
    chunk = tm // chunks_per_tile
    tiles_per_seq = seq // tm
    build = lambda r: jnp.minimum(r, n_i - 1)
    out = lambda r: jnp.maximum(r - 1, 0)
    return pl.pallas_call(
        functools.partial(_in_proj_kernel, chunks_per_tile=chunks_per_tile),
        out_shape=(jax.ShapeDtypeStruct((m, BIG_GATES), F32),
                   jax.ShapeDtypeStruct((m, SMALL_W), F32),
                   jax.ShapeDtypeStruct((SMALL_W, m), F32),
                   jax.ShapeDtypeStruct((m, D_MODEL), BF16)),
        grid=(n_i + 1, n_j),
        in_specs=[
            pl.BlockSpec((chunk, D_MODEL), lambda r, j: (
                build(r) * chunks_per_tile + jnp.minimum(j, chunks_per_tile - 1), 0)),
            pl.BlockSpec((None, N_MOD, D_MODEL), lambda r, j: (build(r) // tiles_per_seq, 0, 0)),
            pl.BlockSpec((1, D_MODEL), lambda r, j: (0, 0)),
            pl.BlockSpec((None, D_MODEL, tn), lambda r, j: (layer, 0, j)),
            pl.BlockSpec((None, SMALL_W, D_MODEL), lambda r, j: (layer, 0, 0)),
        ],
        out_specs=(pl.BlockSpec((tm, tn), lambda r, j: (out(r), jnp.where(r == 0, 0, j))),
                   pl.BlockSpec((tm, SMALL_W), lambda r, j: (out(r), 0)),
                   pl.BlockSpec((SMALL_W, tm), lambda r, j: (0, out(r))),
                   pl.BlockSpec((tm, D_MODEL), lambda r, j: (out(r), 0))),
        scratch_shapes=[pltpu.VMEM((2, tm, D_MODEL), BF16)],
        compiler_params=_params("arbitrary", "arbitrary"),
        name="in_proj",
    )(x2, mod, g, w_big, w_small_t)


def _causal_conv(xbuf, cw_ref, rows):
    acc = cw_ref[CONV_K - 1:CONV_K, :] * xbuf[HALO:HALO + rows, :]
    for j in range(CONV_K - 1):
        off = HALO - (CONV_K - 1) + j
        acc = acc + cw_ref[j:j + 1, :] * xbuf[off:off + rows, :]
    return acc


def _gdn_kernel(*refs):
    nb = GDN_BATCH_TILE
    q_ref, k_ref, v_ref, z_ref, sm_ref = refs[:5]
    smt_refs = refs[5:5 + nb]
    cw_ref, alr_ref, alc_ref, dtr_ref, dtc_ref, nw_ref, o_ref, xbuf, s_scr = refs[5 + nb:]
    rows = TIME_TILE

    @pl.when(pl.program_id(1) == 0)
    def _():
        xbuf[:, 0:HALO, :] = jnp.zeros((nb, HALO, xbuf.shape[2]), F32)
        s_scr[...] = jnp.zeros(s_scr.shape, F32)

    ri = lax.broadcasted_iota(jnp.int32, (rows, rows), 0)
    ci = lax.broadcasted_iota(jnp.int32, (rows, rows), 1)
    same = (ri // GDN_CHUNK) == (ci // GDN_CHUNK)
    causal = same & (ci <= ri)
    strict = same & (ci < ri)
    cum_c = causal.astype(F32)
    cum_r = (same & (ri <= ci)).astype(F32)
    eye = (ri == ci).astype(F32)
    level_masks = []
    s = 1
    while s < GDN_CHUNK:
        level_masks.append((ri // (2 * s) == ci // (2 * s)) & (ri % (2 * s) >= s) & (ci % (2 * s) < s))
        s *= 2
    zeros_half = jnp.zeros((GDN_CHUNK, GDN_DV), F32)
    col_row = lax.broadcasted_iota(jnp.int32, (1, rows), 1)
    n_chunks = rows // GDN_CHUNK

    units = [(bb, h) for bb in range(nb) for h in range(GDN_HEADS)]
    idx = range(len(units))
    qs, kts, gccs, gcrs, a_qks, a_kks, rhss = [], [], [], [], [], [], []
    for bb in range(nb):
        xb = xbuf.at[bb]
        xb[HALO:HALO + rows, 0:GDN_QK_W] = q_ref[bb]
        xb[HALO:HALO + rows, GDN_QK_W:2 * GDN_QK_W] = k_ref[bb]
        xb[HALO:HALO + rows, 2 * GDN_QK_W:2 * GDN_QK_W + GDN_V_W] = v_ref[bb]
        qkv = _silu(_causal_conv(xb, cw_ref, rows))
        xb[0:HALO, :] = xb[rows:rows + HALO, :]

        sm = sm_ref[bb]
        smt = smt_refs[bb][...]
        beta_c = _sigmoid(sm)
        g_c = -jnp.exp(alr_ref[...]) * _softplus(sm + dtr_ref[...])
        g_r = -jnp.exp(alc_ref[...]) * _softplus(smt + dtc_ref[...])
        gc_c = _dot(cum_c, g_c, precision=HIGHEST)
        gc_r = _dot(g_r, cum_r, precision=HIGHEST)

        for h in range(GDN_HEADS):
            qh = qkv[:, h * GDN_DK:(h + 1) * GDN_DK]
            kh = qkv[:, GDN_QK_W + h * GDN_DK:GDN_QK_W + (h + 1) * GDN_DK]
            vh = qkv[:, 2 * GDN_QK_W + h * GDN_DV:2 * GDN_QK_W + (h + 1) * GDN_DV]
            qh = qh * lax.rsqrt(jnp.sum(qh * qh, axis=-1, keepdims=True) + EPS) * (GDN_DK ** -0.5)
            kh = kh * lax.rsqrt(jnp.sum(kh * kh, axis=-1, keepdims=True) + EPS)
            bc = beta_c[:, SMALL_BETA + h:SMALL_BETA + h + 1]
            gcc = gc_c[:, SMALL_ALPHA + h:SMALL_ALPHA + h + 1]
            gcr = gc_r[SMALL_ALPHA + h:SMALL_ALPHA + h + 1, :]
            e = jnp.exp(jnp.where(causal, gcc - gcr, 0.0))
            kb = kh * bc
            kt = kh.T
            qk_kk = _bdot(jnp.concatenate([qh, kb], axis=0), kt)
            eg = jnp.exp(gcc)
            qs.append(qh * eg)
            kts.append(kt)
            gccs.append(gcc)
            gcrs.append(gcr)
            a_qks.append(jnp.where(causal, qk_kk[:rows] * e, 0.0))
            a_kks.append(jnp.where(strict, qk_kk[rows:] * e, 0.0))
            rhss.append(jnp.concatenate([vh * bc, kb * eg], axis=1))

    tinv = [eye - jnp.where(level_masks[0], a_kks[u], 0.0) for u in idx]
    for lm in level_masks[1:]:
        tb = [tinv[u].astype(BF16) for u in idx]
        et = [_dot(jnp.where(lm, a_kks[u], 0.0).astype(BF16), tb[u]) for u in idx]
        tinv = [tinv[u] - _dot(tb[u], et[u].astype(BF16)) for u in idx]
    uws = [_bdot(tinv[u], rhss[u]) for u in idx]

    states = [s_scr[bb, h] for bb, h in units]
    outs = [[] for _ in idx]
    for c in range(n_chunks):
        lo = c * GDN_CHUNK
        hi = lo + GDN_CHUNK
        in_chunk = (col_row >= lo) & (col_row < hi)
        rs = [_bdot(jnp.concatenate([uws[u][lo:hi, GDN_DV:], qs[u][lo:hi]], axis=0), states[u])
              for u in idx]
        v_pads = []
        for u in idx:
            pads = [zeros_half] * n_chunks
            pads[c] = uws[u][lo:hi, :GDN_DV] - rs[u][:GDN_CHUNK]
            v_pads.append(jnp.concatenate(pads, axis=0))
        new_states = []
        for u in idx:
            g_last = gccs[u][hi - 1:hi, :]
            tail = jnp.where(in_chunk, jnp.exp(jnp.where(in_chunk, g_last - gcrs[u], 0.0)), 0.0)
            new_states.append(states[u] * jnp.exp(g_last) + _bdot(kts[u] * tail, v_pads[u]))
        states = new_states
        for u in idx:
            outs[u].append(rs[u][GDN_CHUNK:] + _bdot(a_qks[u][lo:hi, :], v_pads[u]))
    for u, (bb, h) in enumerate(units):
        s_scr[bb, h] = states[u]
        o = _rms(jnp.concatenate(outs[u], axis=0)) * nw_ref[...]
        zg = _silu(z_ref[bb, :, h * GDN_DV:(h + 1) * GDN_DV])
        o_ref[bb, :, h * GDN_DV:(h + 1) * GDN_DV] = (o * zg).astype(BF16)


def _gdn(big, small, small_t, conv_w, a_log, dt_bias, norm_w, bsz, seq):
    nb = GDN_BATCH_TILE
    nt = seq // TIME_TILE
    qkv_w = 2 * GDN_QK_W + GDN_V_W
    big3 = big.reshape(bsz, seq, big.shape[1])
    small3 = small.reshape(bsz, seq, SMALL_W)

    def pad_row(v):
        return jnp.zeros((1, SMALL_W), F32).at[0, SMALL_ALPHA:SMALL_ALPHA + GDN_HEADS].set(v)

    alr = pad_row(a_log)
    dtr = pad_row(dt_bias)
    col = lambda cb: pl.BlockSpec((nb, TIME_TILE, MIX_WIDTH), lambda g, t: (g, t, cb))
    const = lambda shape: pl.BlockSpec(shape, lambda g, t: (0, 0))
    small_t_spec = lambda bb: pl.BlockSpec(
        (SMALL_W, TIME_TILE), lambda g, t: (0, (g * nb + bb) * nt + t))
    out = pl.pallas_call(
        _gdn_kernel,
        out_shape=jax.ShapeDtypeStruct((bsz, seq, MIX_WIDTH), BF16),
        grid=(bsz // nb, nt),
        in_specs=[
            col(0), col(1), col(2), col(3),
            pl.BlockSpec((nb, TIME_TILE, SMALL_W), lambda g, t: (g, t, 0)),
            *[small_t_spec(bb) for bb in range(nb)],
            const((CONV_K, qkv_w)),
            const((1, SMALL_W)), const((SMALL_W, 1)),
            const((1, SMALL_W)), const((SMALL_W, 1)),
            const((1, GDN_DV)),
        ],
        out_specs=pl.BlockSpec((nb, TIME_TILE, MIX_WIDTH), lambda g, t: (g, t, 0)),
        scratch_shapes=[pltpu.VMEM((nb, HALO + TIME_TILE, qkv_w), F32),
                        pltpu.VMEM((nb, GDN_HEADS, GDN_DK, GDN_DV), F32)],
        compiler_params=_params("parallel", "arbitrary"),
        name="gdn",
    )(big3, big3, big3, big3, small3, *([small_t] * nb), conv_w, alr, alr.T, dtr, dtr.T,
      norm_w.reshape(1, GDN_DV))
    return out.reshape(bsz * seq, MIX_WIDTH)


def _lru_kernel(x_ref, y_ref, cw_ref, cb_ref, wa_ref, ba_ref, wi_ref, bi_ref, lam_ref,
                o_ref, xbuf, h_scr):
    rows = LRU_TIME_TILE
    first = pl.program_id(1) == 0

    @pl.when(first)
    def _():
        xbuf[0:HALO, :] = jnp.zeros((HALO, LRU_WIDTH), F32)
        h_scr[...] = jnp.zeros(h_scr.shape, F32)

    xbuf[HALO:HALO + rows, :] = x_ref[...]
    xl = _causal_conv(xbuf, cw_ref, rows) + cb_ref[...]
    xbuf[0:HALO, :] = xbuf[rows:rows + HALO, :]

    xlb = xl.astype(BF16)
    r_parts = []
    i_parts = []
    for k in range(LRU_BLOCKS):
        blk = xlb[:, k * LRU_BLOCK_W:(k + 1) * LRU_BLOCK_W]
        r_parts.append(_dot(blk, wa_ref[k]))
        i_parts.append(_dot(blk, wi_ref[k]))
    r = _sigmoid(jnp.concatenate(r_parts, axis=1) + ba_ref[...])
    ig = _sigmoid(jnp.concatenate(i_parts, axis=1) + bi_ref[...])
    log_a = -LRU_C * r * _softplus(-lam_ref[...])
    a = jnp.exp(log_a)
    mult = jnp.sqrt(-jnp.tanh(log_a) * (a * a + 1.0))
    row = lax.broadcasted_iota(jnp.int32, (rows, 1), 0)
    mult = jnp.where(first & (row == 0), 1.0, mult)
    b = mult * ig * xl

    row8 = row[:SUBLANES]
    shift = 1
    while shift < SUBLANES:
        a_roll = pltpu.roll(a, shift, 0)
        b_roll = pltpu.roll(b, shift, 0)
        keep = row8 >= shift
        a_prev = jnp.concatenate([jnp.where(keep, a_roll[:SUBLANES], 1.0), a_roll[SUBLANES:]], axis=0)
        b_prev = jnp.concatenate([jnp.where(keep, b_roll[:SUBLANES], 0.0), b_roll[SUBLANES:]], axis=0)
        b = a * b_prev + b
        a = a * a_prev
        shift *= 2
    while shift < rows:
        b = jnp.concatenate([b[:shift], a[shift:] * b[:rows - shift] + b[shift:]], axis=0)
        a = jnp.concatenate([a[:shift], a[shift:] * a[:rows - shift]], axis=0)
        shift *= 2
    hs = a * h_scr[0:1, :] + b
    h_scr[0:1, :] = hs[rows - 1:rows, :]
    o_ref[...] = (_gelu_tanh(y_ref[...]) * hs).astype(BF16)


def _lru(big, conv_w, conv_b, w_a, b_a, w_i, b_i, lam, bsz, seq):
    m = bsz * seq
    nt = seq // LRU_TIME_TILE
    base = BIG_LRU // LRU_WIDTH
    row1 = lambda v: v.reshape(1, LRU_WIDTH)
    const2 = lambda shape: pl.BlockSpec(shape, lambda b, t: (0, 0))
    const3 = lambda shape: pl.BlockSpec(shape, lambda b, t: (0, 0, 0))
    return pl.pallas_call(
        _lru_kernel,
        out_shape=jax.ShapeDtypeStruct((m, LRU_WIDTH), BF16),
        grid=(bsz, nt),
        in_specs=[
            pl.BlockSpec((LRU_TIME_TILE, LRU_WIDTH), lambda b, t: (b * nt + t, base)),
            pl.BlockSpec((LRU_TIME_TILE, LRU_WIDTH), lambda b, t: (b * nt + t, base + 1)),
            const2((CONV_K, LRU_WIDTH)), const2((1, LRU_WIDTH)),
            const3((LRU_BLOCKS, LRU_BLOCK_W, LRU_BLOCK_W)), const2((1, LRU_WIDTH)),
            const3((LRU_BLOCKS, LRU_BLOCK_W, LRU_BLOCK_W)), const2((1, LRU_WIDTH)),
            const2((1, LRU_WIDTH)),
        ],
        out_specs=pl.BlockSpec((LRU_TIME_TILE, LRU_WIDTH), lambda b, t: (b * nt + t, 0)),
        scratch_shapes=[pltpu.VMEM((HALO + LRU_TIME_TILE, LRU_WIDTH), F32),
                        pltpu.VMEM((SUBLANES, LRU_WIDTH), F32)],
        compiler_params=_params("parallel", "arbitrary"),
        name="rg_lru",
    )(big, big, conv_w, row1(conv_b), w_a.astype(BF16), row1(b_a), w_i.astype(BF16),
      row1(b_i), row1(lam))


def _gla_kernel(q_ref, k_ref, v_ref, z_ref, sm_ref, wg_ref, bg_ref, nw_ref, o_ref, st_scr):
    rows = TIME_TILE

    @pl.when(pl.program_id(1) == 0)
    def _():
        st_scr[...] = jnp.zeros(st_scr.shape, F32)

    nb = GLA_BATCH_TILE
    ri = lax.broadcasted_iota(jnp.int32, (rows, rows), 0)
    ci = lax.broadcasted_iota(jnp.int32, (rows, rows), 1)
    tri = (ci <= ri).astype(F32)
    bcums = []
    for bb in range(nb):
        log_a = _log_sigmoid(_dot(sm_ref[bb].astype(BF16), wg_ref[...]) + bg_ref[...]) / GLA_GATE_NORM
        bcums.append(_dot(tri, log_a, precision=HIGHEST))
    row = lax.broadcasted_iota(jnp.int32, (rows, 1), 0)

    def boundary_rows(bh, width, offset, first_zero):
        out = jnp.zeros_like(bh) if first_zero else None
        for g in range(rows // width):
            src = g * width + offset
            if src < 0:
                continue
            val = jnp.broadcast_to(bh[src:src + 1, :], bh.shape)
            out = val if out is None else jnp.where(row // width == g, val, out)
        return out

    units = [(bb, h) for bb in range(nb) for h in range(GLA_HEADS)]
    heads = range(len(units))
    bhs = [bcums[bb][:, h * GLA_DK:(h + 1) * GLA_DK] for bb, h in units]
    qh = [q_ref[bb, :, h * GLA_DK:(h + 1) * GLA_DK] * (GLA_DK ** -0.5) for bb, h in units]
    kh = [k_ref[bb, :, h * GLA_DK:(h + 1) * GLA_DK] for bb, h in units]
    vh = [v_ref[bb, :, h * GLA_DV:(h + 1) * GLA_DV] for bb, h in units]
    b_loc = [bhs[h] - boundary_rows(bhs[h], GLA_BASE_CHUNK, -1, True) for h in heads]
    base_mask = (ri // GLA_BASE_CHUNK == ci // GLA_BASE_CHUNK) & (ci <= ri)
    att = [jnp.where(base_mask,
                     _bdot_nt(qh[h] * jnp.exp(b_loc[h]), kh[h] * jnp.exp(-b_loc[h])), 0.0)
           for h in heads]
    s = GLA_BASE_CHUNK
    while s < rows:
        mask = ((ri // (2 * s) == ci // (2 * s)) & (ri % (2 * s) >= s) & (ci % (2 * s) < s))
        refs = [boundary_rows(bhs[h], 2 * s, s - 1, False) for h in heads]
        cross = [_bdot_nt(qh[h] * jnp.exp(jnp.minimum(bhs[h] - refs[h], 0.0)),
                          kh[h] * jnp.exp(jnp.minimum(refs[h] - bhs[h], 0.0))) for h in heads]
        att = [att[h] + jnp.where(mask, cross[h], 0.0) for h in heads]
        s *= 2
    sts = [st_scr[bb, h] for bb, h in units]
    o_inter = [_bdot_nt(qh[u] * jnp.exp(bhs[u]), sts[u]) for u in heads]
    o_intra = [_bdot(att[u], vh[u]) for u in heads]
    b_end = [bhs[u][rows - 1:rows, :] for u in heads]
    upd = [_bdot(vh[u].T, kh[u] * jnp.exp(b_end[u] - bhs[u])) for u in heads]
    for u, (bb, h) in enumerate(units):
        st_scr[bb, h] = sts[u] * jnp.exp(b_end[u]) + upd[u]
        o = _rms(o_intra[u] + o_inter[u]) * nw_ref[...]
        zg = _silu(z_ref[bb, :, h * GLA_DV:(h + 1) * GLA_DV])
        o_ref[bb, :, h * GLA_DV:(h + 1) * GLA_DV] = (o * zg).astype(BF16)


def _gla(big, small, w_gate, b_gate, norm_w, bsz, seq):
    nb = GLA_BATCH_TILE
    nt = seq // TIME_TILE
    big3 = big.reshape(bsz, seq, big.shape[1])
    small3 = small.reshape(bsz, seq, SMALL_W)
    wg = jnp.zeros((SMALL_W, GLA_QK_W), F32).at[SMALL_GK:SMALL_GK + GLA_GATE_RANK].set(w_gate)
    qk_base = BIG_GLA // GLA_QK_W
    v_base = (BIG_GLA + 2 * GLA_QK_W) // GLA_V_W
    const = lambda shape: pl.BlockSpec(shape, lambda g, t: (0, 0))
    tile = lambda width, cb: pl.BlockSpec((nb, TIME_TILE, width), lambda g, t: (g, t, cb))
    out = pl.pallas_call(
        _gla_kernel,
        out_shape=jax.ShapeDtypeStruct((bsz, seq, MIX_WIDTH), BF16),
        grid=(bsz // nb, nt),
        in_specs=[
            tile(GLA_QK_W, qk_base), tile(GLA_QK_W, qk_base + 1),
            tile(GLA_V_W, v_base), tile(GLA_V_W, v_base + 1),
            tile(SMALL_W, 0),
            const((SMALL_W, GLA_QK_W)), const((1, GLA_QK_W)), const((1, GLA_DV)),
        ],
        out_specs=tile(MIX_WIDTH, 0),
        scratch_shapes=[pltpu.VMEM((nb, GLA_HEADS, GLA_DV, GLA_DK), F32)],
        compiler_params=_params("parallel", "arbitrary"),
        name="gla",
    )(big3, big3, big3, big3, small3, wg.astype(BF16), b_gate.reshape(1, GLA_QK_W),
      norm_w.reshape(1, GLA_DV))
    return out.reshape(bsz * seq, MIX_WIDTH)


def _merge_kernel(h_ref, wga_ref, wgb_ref, wgc_ref, oa_ref, ob_ref, oc_ref, wb_ref, o_ref):
    h = h_ref[...]
    acc = _sigmoid(_dot(h, wga_ref[...])) * _dot(oa_ref[...], wb_ref[0])
    acc = acc + _sigmoid(_dot(h, wgb_ref[...])) * _dot(ob_ref[...], wb_ref[1])
    acc = acc + _sigmoid(_dot(h, wgc_ref[...])) * _dot(oc_ref[...], wb_ref[2])
    o_ref[...] = acc.astype(BF16)


def _merge(h, oa, ob, oc, w_big, w_branch, layer, seq):
    m = oa.shape[0]
    tm = min(MERGE_ROW_TILE, seq)
    tn = MERGE_COL_TILE
    gate_w = lambda n: pl.BlockSpec(
        (None, D_MODEL, tn), lambda i, j: (layer, 0, (BIG_GATES + n * D_MODEL) // tn + j))
    branch = pl.BlockSpec((tm, MIX_WIDTH), lambda i, j: (i, 0))
    return pl.pallas_call(
        _merge_kernel,
        out_shape=jax.ShapeDtypeStruct((m, D_MODEL), BF16),
        grid=(m // tm, D_MODEL // tn),
        in_specs=[pl.BlockSpec((tm, D_MODEL), lambda i, j: (i, 0)),
                  gate_w(0), gate_w(1), gate_w(2), branch, branch, branch,
                  pl.BlockSpec((None, N_BRANCH, MIX_WIDTH, tn), lambda i, j: (layer, 0, 0, j))],
        out_specs=pl.BlockSpec((tm, tn), lambda i, j: (i, j)),
        compiler_params=_params("parallel", "arbitrary"),
        name="merge",
    )(h, w_big, w_big, w_big, oa, ob, oc, w_branch)


def _out_proj_kernel(m_ref, x_ref, mod_ref, g_ref, w_ref, o_ref):
    y = _dot(m_ref[...], w_ref[...])
    o_ref[...] = x_ref[...] + mod_ref[MOD_GT1:MOD_GT1 + 1, :] * (_rms(y) * g_ref[...])


def _out_proj(merged, x2, mod, g, w_out, layer, seq):
    m = x2.shape[0]
    tm = min(OUT_PROJ_ROW_TILE, seq)
    tiles_per_seq = seq // tm
    return pl.pallas_call(
        _out_proj_kernel,
        out_shape=jax.ShapeDtypeStruct((m, D_MODEL), F32),
        grid=(m // tm,),
        in_specs=[
            pl.BlockSpec((tm, D_MODEL), lambda i: (i, 0)),
            pl.BlockSpec((tm, D_MODEL), lambda i: (i, 0)),
            pl.BlockSpec((None, N_MOD, D_MODEL), lambda i: (i // tiles_per_seq, 0, 0)),
            pl.BlockSpec((1, D_MODEL), lambda i: (0, 0)),
            pl.BlockSpec((None, D_MODEL, D_MODEL), lambda i: (layer, 0, 0)),
        ],
        out_specs=pl.BlockSpec((tm, D_MODEL), lambda i: (i, 0)),
        compiler_params=_params("parallel"),
        name="out_proj",
    )(merged, x2, mod, g, w_out)


def _mlp_kernel(xb_ref, xe_ref, modb_ref, mode_ref, gpre_ref, gpost_ref, w1_ref, w2_ref, o_ref,
                h_scr, acc_scr, *, n_tiles):
    r = pl.program_id(0)
    f = pl.program_id(1)
    chunk = xb_ref.shape[0]
    rows = pl.ds(pl.multiple_of(f * chunk, chunk), chunk)

    def build():
        h = _rms(xb_ref[...]) * gpre_ref[...]
        h = h * (1.0 + modb_ref[MOD_SC2:MOD_SC2 + 1, :]) + modb_ref[MOD_SH2:MOD_SH2 + 1, :]
        h_scr[r % 2, rows, :] = h.astype(BF16)

    def finish():
        y = _rms(acc_scr[r % 2, rows, :]) * gpost_ref[...]
        o_ref[...] = xe_ref[...] + mode_ref[MOD_GT2:MOD_GT2 + 1, :] * y
        acc_scr[r % 2, rows, :] = jnp.zeros((chunk, acc_scr.shape[2]), F32)

    @pl.when(r == 0)
    def _():
        @pl.when(f == 0)
        def _():
            acc_scr[...] = jnp.zeros(acc_scr.shape, F32)

        build()
        o_ref[...] = jnp.zeros(o_ref.shape, F32)

    @pl.when((r >= 1) & (r <= n_tiles))
    def _():
        prev = (r + 1) % 2
        finish()
        a = jnp.maximum(_dot(h_scr[prev], w1_ref[...]), 0.0)
        acc_scr[prev] += _dot((a * a).astype(BF16), w2_ref[...])
        build()

    @pl.when(r == n_tiles + 1)
    def _():
        finish()


def _mlp(x2, mod, g_pre, g_post, w1, w2, layer, seq):
    m = x2.shape[0]
    tm = min(MLP_ROW_TILE, seq)
    tf = MLP_FF_TILE
    n_f = D_FF // tf
    n_i = m // tm
    chunk = tm // n_f
    tiles_per_seq = seq // tm
    build = lambda r: jnp.minimum(r, n_i - 1)
    done = lambda r: jnp.clip(r - 2, 0, n_i - 1)
    wf = lambda r, f: jnp.where(r == 0, 0, jnp.where(r == n_i + 1, n_f - 1, f))
    return pl.pallas_call(
        functools.partial(_mlp_kernel, n_tiles=n_i),
        out_shape=jax.ShapeDtypeStruct((m, D_MODEL), F32),
        grid=(n_i + 2, n_f),
        in_specs=[
            pl.BlockSpec((chunk, D_MODEL), lambda r, f: (build(r) * n_f + f, 0)),
            pl.BlockSpec((chunk, D_MODEL), lambda r, f: (done(r) * n_f + f, 0)),
            pl.BlockSpec((None, N_MOD, D_MODEL), lambda r, f: (build(r) // tiles_per_seq, 0, 0)),
            pl.BlockSpec((None, N_MOD, D_MODEL), lambda r, f: (done(r) // tiles_per_seq, 0, 0)),
            pl.BlockSpec((1, D_MODEL), lambda r, f: (0, 0)),
            pl.BlockSpec((1, D_MODEL), lambda r, f: (0, 0)),
            pl.BlockSpec((None, D_MODEL, tf), lambda r, f: (layer, 0, wf(r, f))),
            pl.BlockSpec((None, tf, D_MODEL), lambda r, f: (layer, wf(r, f), 0)),
        ],
        out_specs=pl.BlockSpec((chunk, D_MODEL),
                               lambda r, f: (done(r) * n_f + jnp.where(r < 2, 0, f), 0)),
        scratch_shapes=[pltpu.VMEM((2, tm, D_MODEL), BF16), pltpu.VMEM((2, tm, D_MODEL), F32)],
        compiler_params=_params("arbitrary", "arbitrary"),
        name="mlp",
    )(x2, x2, mod, mod, g_pre, g_post, w1, w2)


def kernel(x, c, w_ada, b_ada, g_pre_mix, g_post_mix, g_pre_mlp, g_post_mlp, w_in, conv_gdn, gdn_a_log, gdn_dt_bias, gdn_norm, conv_lru, conv_lru_b, lru_w_a, lru_b_a, lru_w_i, lru_b_i, lru_lambda, gla_w_gate, gla_b_gate, gla_norm, w_branch, w_out, w_mlp1, w_mlp2):
    bsz, seq, _ = x.shape
    n_layers = w_in.shape[0]
    assert seq % LRU_TIME_TILE == 0 and x.shape[2] == D_MODEL and bsz % GDN_BATCH_TILE == 0
    row = lambda v: v.reshape(1, D_MODEL)

    mod_all = _ada_mod(c, w_ada, b_ada).reshape(n_layers, bsz, N_MOD, D_MODEL)
    x2 = x.reshape(bsz * seq, D_MODEL)
    w_big, w_small_t = _split_w_in(w_in)
    w_branch_b = w_branch.astype(BF16)
    w_out_b = w_out.astype(BF16)
    w_mlp1_b = w_mlp1.astype(BF16)
    w_mlp2_b = w_mlp2.astype(BF16)
    for l in range(n_layers):
        mod = mod_all[l]
        big, small, small_t, h = _in_proj(x2, mod, row(g_pre_mix[l]), w_big, w_small_t, l, seq)
        oa = _gdn(big, small, small_t, conv_gdn[l], gdn_a_log[l], gdn_dt_bias[l], gdn_norm[l],
                  bsz, seq)
        ob = _lru(big, conv_lru[l], conv_lru_b[l], lru_w_a[l], lru_b_a[l], lru_w_i[l],
                  lru_b_i[l], lru_lambda[l], bsz, seq)
        oc = _gla(big, small, gla_w_gate[l], gla_b_gate[l], gla_norm[l], bsz, seq)
        merged = _merge(h, oa, ob, oc, w_big, w_branch_b, l, seq)
        x2 = _out_proj(merged, x2, mod, row(g_post_mix[l]), w_out_b, l, seq)
        x2 = _mlp(x2, mod, row(g_pre_mlp[l]), row(g_post_mlp[l]), w_mlp1_b, w_mlp2_b, l, seq)
    return x2.reshape(bsz, seq, D_MODEL)
```

```python
import functools

import jax
import jax.numpy as jnp
from jax import lax
from jax.experimental import pallas as pl
from jax.experimental.pallas import tpu as pltpu

F32 = jnp.float32
BF16 = jnp.bfloat16
HIGHEST = lax.Precision.HIGHEST

D_MODEL = 2048
MIX_WIDTH = D_MODEL // 2
N_BRANCH = 3
GDN_DK = 128
GDN_DV = 128
GDN_HEADS = MIX_WIDTH // GDN_DV
GDN_CHUNK = 64
LRU_WIDTH = MIX_WIDTH
LRU_BLOCKS = 8
LRU_BLOCK_W = LRU_WIDTH // LRU_BLOCKS
LRU_C = 8.0
GLA_HEADS = 4
GLA_DV = MIX_WIDTH // GLA_HEADS
GLA_DK = GLA_DV // 2
GLA_GATE_RANK = 16
GLA_GATE_NORM = 16.0
GLA_BASE_CHUNK = 16
CONV_K = 4
D_FF = 4 * D_MODEL
N_MOD = 6
EPS = 1e-6

GDN_QK_W = GDN_HEADS * GDN_DK
GDN_V_W = GDN_HEADS * GDN_DV
GLA_QK_W = GLA_HEADS * GLA_DK
GLA_V_W = GLA_HEADS * GLA_DV
IN_SPLITS = (GDN_QK_W, GDN_QK_W, GDN_V_W, GDN_V_W, GDN_HEADS, GDN_HEADS,
             LRU_WIDTH, LRU_WIDTH,
             GLA_QK_W, GLA_QK_W, GLA_V_W, GLA_V_W, GLA_GATE_RANK,
             N_BRANCH * D_MODEL)
IN_OFFS = tuple(sum(IN_SPLITS[:i]) for i in range(len(IN_SPLITS) + 1))

LANES = 128
SUBLANES = 8
VMEM_LIMIT_BYTES = 56 * 1024 * 1024

BIG_GDN = 0
BIG_LRU = BIG_GDN + 4 * MIX_WIDTH
BIG_GLA = BIG_LRU + 2 * LRU_WIDTH
BIG_GATES = BIG_GLA + 2 * GLA_QK_W + 2 * GLA_V_W
BIG_W = BIG_GATES + N_BRANCH * D_MODEL
SMALL_BETA = 0
SMALL_ALPHA = GDN_HEADS
SMALL_GK = 2 * GDN_HEADS
SMALL_W = LANES

MOD_SH1, MOD_SC1, MOD_GT1, MOD_SH2, MOD_SC2, MOD_GT2 = range(N_MOD)

ADA_COL_TILE = 1024
PROJ_ROW_TILE = 1024
PROJ_COL_TILE = 1536
MERGE_ROW_TILE = 1024
MERGE_COL_TILE = 512
OUT_PROJ_ROW_TILE = 512
MLP_ROW_TILE = 1024
MLP_FF_TILE = 1024
TIME_TILE = 128
GDN_BATCH_TILE = 2
GLA_BATCH_TILE = 4
LRU_TIME_TILE = 512
HALO = SUBLANES

W_PREP_TILE = 512
W_PREP_RUN1_TILE = IN_OFFS[4] // W_PREP_TILE
W_PREP_RUN2_TILE = (IN_OFFS[4] + IN_OFFS[12] - IN_OFFS[6]) // W_PREP_TILE
W_PREP_SHIFT1 = IN_OFFS[6] - IN_OFFS[4]
W_PREP_SHIFT2 = W_PREP_SHIFT1 + IN_OFFS[13] - IN_OFFS[12]
W_PREP_NEXT = W_PREP_SHIFT2
W_PREP_NARROW1 = slice(IN_OFFS[4] - W_PREP_RUN1_TILE * W_PREP_TILE,
                       IN_OFFS[6] - W_PREP_RUN1_TILE * W_PREP_TILE)
W_PREP_NARROW2 = slice(IN_OFFS[12] - W_PREP_RUN2_TILE * W_PREP_TILE,
                       IN_OFFS[13] - W_PREP_RUN2_TILE * W_PREP_TILE)
assert (W_PREP_NARROW1.start, W_PREP_NARROW1.stop) == (SMALL_BETA, SMALL_GK)
assert (W_PREP_NARROW2.start, W_PREP_NARROW2.stop) == (SMALL_GK, SMALL_GK + GLA_GATE_RANK)
assert W_PREP_TILE % W_PREP_NEXT == 0 and IN_OFFS[-1] % W_PREP_NEXT == 0


def _sigmoid(x):
    return 1.0 / (1.0 + jnp.exp(-x))


def _silu(x):
    return x * _sigmoid(x)


def _softplus(x):
    return jnp.maximum(x, 0.0) + jnp.log(1.0 + jnp.exp(-jnp.abs(x)))


def _log_sigmoid(x):
    return -_softplus(-x)


def _gelu_tanh(x):
    c = 0.7978845608028654
    return 0.5 * x * (1.0 + jnp.tanh(c * (x + 0.044715 * (x * x * x))))


def _rms(x):
    return x * lax.rsqrt(jnp.mean(x * x, axis=-1, keepdims=True) + EPS)


def _dot(a, b, precision=None):
    return jnp.dot(a, b, preferred_element_type=F32, precision=precision)


def _dot_nt(a, b):
    return lax.dot_general(a, b, (((1,), (1,)), ((), ())), preferred_element_type=F32)


def _bdot(a, b):
    return _dot(a.astype(BF16), b.astype(BF16))


def _bdot_nt(a, b):
    return _dot_nt(a.astype(BF16), b.astype(BF16))


def _params(*sem):
    return pltpu.CompilerParams(dimension_semantics=sem, vmem_limit_bytes=VMEM_LIMIT_BYTES)


def _ada_kernel(c_ref, w_ref, b_ref, o_ref):
    sc = _silu(c_ref[...]).astype(BF16)
    o_ref[...] = _dot(sc, w_ref[...].astype(BF16)) + b_ref[...]


def _ada_mod(c, w_ada, b_ada):
    n_layers, _, n_out = w_ada.shape
    bsz = c.shape[0]
    tn = ADA_COL_TILE
    return pl.pallas_call(
        _ada_kernel,
        out_shape=jax.ShapeDtypeStruct((n_layers, bsz, n_out), F32),
        grid=(n_layers, n_out // tn),
        in_specs=[
            pl.BlockSpec((bsz, D_MODEL), lambda l, j: (0, 0)),
            pl.BlockSpec((None, D_MODEL, tn), lambda l, j: (l, 0, j)),
            pl.BlockSpec((None, 1, tn), lambda l, j: (l, 0, j)),
        ],
        out_specs=pl.BlockSpec((None, bsz, tn), lambda l, j: (l, 0, j)),
        compiler_params=_params("parallel", "parallel"),
        name="ada_mod",
    )(c, w_ada, b_ada.reshape(n_layers, 1, n_out))


def _w_prep_kernel(a_ref, b_ref, o_ref, wst_ref):
    j = pl.program_id(1)

    def emit(shift):
        src = a_ref[...] if shift == 0 else jnp.concatenate([a_ref[shift:, :], b_ref[:shift, :]],
                                                            axis=0)
        o_ref[...] = src.T.astype(BF16)

    @pl.when(j == 0)
    def _():
        wst_ref[...] = jnp.zeros(wst_ref.shape, BF16)

    @pl.when(j < W_PREP_RUN1_TILE)
    def _():
        emit(0)

    @pl.when((j >= W_PREP_RUN1_TILE) & (j < W_PREP_RUN2_TILE))
    def _():
        emit(W_PREP_SHIFT1)

    @pl.when(j >= W_PREP_RUN2_TILE)
    def _():
        emit(W_PREP_SHIFT2)

    @pl.when(j == W_PREP_RUN1_TILE)
    def _():
        wst_ref[W_PREP_NARROW1, :] = a_ref[W_PREP_NARROW1, :].astype(BF16)

    @pl.when(j == W_PREP_RUN2_TILE)
    def _():
        wst_ref[W_PREP_NARROW2, :] = a_ref[W_PREP_NARROW2, :].astype(BF16)


def _split_w_in(w_in):
    n_layers = w_in.shape[0]
    w_t = jnp.swapaxes(w_in, 1, 2)
    next_per_tile = W_PREP_TILE // W_PREP_NEXT
    w_big, w_small_t = pl.pallas_call(
        _w_prep_kernel,
        out_shape=(jax.ShapeDtypeStruct((n_layers, D_MODEL, BIG_W), BF16),
                   jax.ShapeDtypeStruct((n_layers, SMALL_W, D_MODEL), BF16)),
        grid=(n_layers, BIG_W // W_PREP_TILE),
        in_specs=[
            pl.BlockSpec((None, W_PREP_TILE, D_MODEL), lambda l, j: (l, j, 0)),
            pl.BlockSpec((None, W_PREP_NEXT, D_MODEL), lambda l, j: (l, (j + 1) * next_per_tile, 0)),
        ],
        out_specs=(pl.BlockSpec((None, D_MODEL, W_PREP_TILE), lambda l, j: (l, 0, j)),
                   pl.BlockSpec((None, SMALL_W, D_MODEL), lambda l, j: (l, 0, 0))),
        compiler_params=_params("parallel", "arbitrary"),
        name="w_prep",
    )(w_t, w_t)
    return w_big, w_small_t


def _in_proj_kernel(xn_ref, modn_ref, g_ref, w_ref, wst_ref,
                    big_ref, small_ref, smallt_ref, h_ref, h_scr, *, chunks_per_tile):
    r = pl.program_id(0)
    j = pl.program_id(1)
    chunk = xn_ref.shape[0]

    def build_chunk():
        h = _rms(xn_ref[...]) * g_ref[...]
        h = h * (1.0 + modn_ref[MOD_SC1:MOD_SC1 + 1, :]) + modn_ref[MOD_SH1:MOD_SH1 + 1, :]
        row0 = pl.multiple_of(jnp.minimum(j, chunks_per_tile - 1) * chunk, chunk)
        h_scr[r % 2, pl.ds(row0, chunk), :] = h.astype(BF16)

    @pl.when(r == 0)
    def _():
        build_chunk()

    @pl.when(r > 0)
    def _():
        prev = (r + 1) % 2

        @pl.when(j == 0)
        def _():
            hb = h_scr[prev]
            h_ref[...] = hb
            st = _dot_nt(wst_ref[...], hb)
            smallt_ref[...] = st
            small_ref[...] = st.T

        big_ref[...] = _dot(h_scr[prev], w_ref[...])
        build_chunk()


def _in_proj(x2, mod, g, w_big, w_small_t, layer, seq):
    m = x2.shape[0]
    tm = min(PROJ_ROW_TILE, seq)
    tn = PROJ_COL_TILE
    n_i = m // tm
    n_j = BIG_GATES // tn
    chunks_per_tile = 1 << (min(n_j, SUBLANES).bit_length() - 1)
    chunk = tm // chunks_per_tile
    tiles_per_seq = seq // tm
    build = lambda r: jnp.minimum(r, n_i - 1)
    out = lambda r: jnp.maximum(r - 1, 0)
    return pl.pallas_call(
        functools.partial(_in_proj_kernel, chunks_per_tile=chunks_per_tile),
        out_shape=(jax.ShapeDtypeStruct((m, BIG_GATES), F32),
                   jax.ShapeDtypeStruct((m, SMALL_W), F32),
                   jax.ShapeDtypeStruct((SMALL_W, m), F32),
                   jax.ShapeDtypeStruct((m, D_MODEL), BF16)),
        grid=(n_i + 1, n_j),
        in_specs=[
            pl.BlockSpec((chunk, D_MODEL), lambda r, j: (
                build(r) * chunks_per_tile + jnp.minimum(j, chunks_per_tile - 1), 0)),
            pl.BlockSpec((None, N_MOD, D_MODEL), lambda r, j: (build(r) // tiles_per_seq, 0, 0)),
            pl.BlockSpec((1, D_MODEL), lambda r, j: (0, 0)),
            pl.BlockSpec((None, D_MODEL, tn), lambda r, j: (layer, 0, j)),
            pl.BlockSpec((None, SMALL_W, D_MODEL), lambda r, j: (layer, 0, 0)),
        ],
        out_specs=(pl.BlockSpec((tm, tn), lambda r, j: (out(r), jnp.where(r == 0, 0, j))),
                   pl.BlockSpec((tm, SMALL_W), lambda r, j: (out(r), 0)),
                   pl.BlockSpec((SMALL_W, tm), lambda r, j: (0, out(r))),
                   pl.BlockSpec((tm, D_MODEL), lambda r, j: (out(r), 0))),
        scratch_shapes=[pltpu.VMEM((2, tm, D_MODEL), BF16)],
        compiler_params=_params("arbitrary", "arbitrary"),
        name="in_proj",
    )(x2, mod, g, w_big, w_small_t)


def _causal_conv(xbuf, cw_ref, rows):
    acc = cw_ref[CONV_K - 1:CONV_K, :] * xbuf[HALO:HALO + rows, :]
    for j in range(CONV_K - 1):
        off = HALO - (CONV_K - 1) + j
        acc = acc + cw_ref[j:j + 1, :] * xbuf[off:off + rows, :]
    return acc


def _gdn_kernel(*refs):
    nb = GDN_BATCH_TILE
    q_ref, k_ref, v_ref, z_ref, sm_ref = refs[:5]
    smt_refs = refs[5:5 + nb]
    cw_ref, alr_ref, alc_ref, dtr_ref, dtc_ref, nw_ref, o_ref, xbuf, s_scr = refs[5 + nb:]
    rows = TIME_TILE

    @pl.when(pl.program_id(1) == 0)
    def _():
        xbuf[:, 0:HALO, :] = jnp.zeros((nb, HALO, xbuf.shape[2]), F32)
        s_scr[...] = jnp.zeros(s_scr.shape, F32)

    ri = lax.broadcasted_iota(jnp.int32, (rows, rows), 0)
    ci = lax.broadcasted_iota(jnp.int32, (rows, rows), 1)
    same = (ri // GDN_CHUNK) == (ci // GDN_CHUNK)
    causal = same & (ci <= ri)
    strict = same & (ci < ri)
    cum_c = causal.astype(F32)
    cum_r = (same & (ri <= ci)).astype(F32)
    eye = (ri == ci).astype(F32)
    level_masks = []
    s = 1
    while s < GDN_CHUNK:
        level_masks.append((ri // (2 * s) == ci // (2 * s)) & (ri % (2 * s) >= s) & (ci % (2 * s) < s))
        s *= 2
    zeros_half = jnp.zeros((GDN_CHUNK, GDN_DV), F32)
    col_row = lax.broadcasted_iota(jnp.int32, (1, rows), 1)
    n_chunks = rows // GDN_CHUNK

    units = [(bb, h) for bb in range(nb) for h in range(GDN_HEADS)]
    idx = range(len(units))
    qs, kts, gccs, gcrs, a_qks, a_kks, rhss = [], [], [], [], [], [], []
    for bb in range(nb):
        xb = xbuf.at[bb]
        xb[HALO:HALO + rows, 0:GDN_QK_W] = q_ref[bb]
        xb[HALO:HALO + rows, GDN_QK_W:2 * GDN_QK_W] = k_ref[bb]
        xb[HALO:HALO + rows, 2 * GDN_QK_W:2 * GDN_QK_W + GDN_V_W] = v_ref[bb]
        qkv = _silu(_causal_conv(xb, cw_ref, rows))
        xb[0:HALO, :] = xb[rows:rows + HALO, :]

        sm = sm_ref[bb]
        smt = smt_refs[bb][...]
        beta_c = _sigmoid(sm)
        g_c = -jnp.exp(alr_ref[...]) * _softplus(sm + dtr_ref[...])
        g_r = -jnp.exp(alc_ref[...]) * _softplus(smt + dtc_ref[...])
        gc_c = _dot(cum_c, g_c, precision=HIGHEST)
        gc_r = _dot(g_r, cum_r, precision=HIGHEST)

        for h in range(GDN_HEADS):
            qh = qkv[:, h * GDN_DK:(h + 1) * GDN_DK]
            kh = qkv[:, GDN_QK_W + h * GDN_DK:GDN_QK_W + (h + 1) * GDN_DK]
            vh = qkv[:, 2 * GDN_QK_W + h * GDN_DV:2 * GDN_QK_W + (h + 1) * GDN_DV]
            qh = qh * lax.rsqrt(jnp.sum(qh * qh, axis=-1, keepdims=True) + EPS) * (GDN_DK ** -0.5)
            kh = kh * lax.rsqrt(jnp.sum(kh * kh, axis=-1, keepdims=True) + EPS)
            bc = beta_c[:, SMALL_BETA + h:SMALL_BETA + h + 1]
            gcc = gc_c[:, SMALL_ALPHA + h:SMALL_ALPHA + h + 1]
            gcr = gc_r[SMALL_ALPHA + h:SMALL_ALPHA + h + 1, :]
            e = jnp.exp(jnp.where(causal, gcc - gcr, 0.0))
            kb = kh * bc
            kt = kh.T
            qk_kk = _bdot(jnp.concatenate([qh, kb], axis=0), kt)
            eg = jnp.exp(gcc)
            qs.append(qh * eg)
            kts.append(kt)
            gccs.append(gcc)
            gcrs.append(gcr)
            a_qks.append(jnp.where(causal, qk_kk[:rows] * e, 0.0))
            a_kks.append(jnp.where(strict, qk_kk[rows:] * e, 0.0))
            rhss.append(jnp.concatenate([vh * bc, kb * eg], axis=1))

    tinv = [eye - jnp.where(level_masks[0], a_kks[u], 0.0) for u in idx]
    for lm in level_masks[1:]:
        tb = [tinv[u].astype(BF16) for u in idx]
        et = [_dot(jnp.where(lm, a_kks[u], 0.0).astype(BF16), tb[u]) for u in idx]
        tinv = [tinv[u] - _dot(tb[u], et[u].astype(BF16)) for u in idx]
    uws = [_bdot(tinv[u], rhss[u]) for u in idx]

    states = [s_scr[bb, h] for bb, h in units]
    outs = [[] for _ in idx]
    for c in range(n_chunks):
        lo = c * GDN_CHUNK
        hi = lo + GDN_CHUNK
        in_chunk = (col_row >= lo) & (col_row < hi)
        rs = [_bdot(jnp.concatenate([uws[u][lo:hi, GDN_DV:], qs[u][lo:hi]], axis=0), states[u])
              for u in idx]
        v_pads = []
        for u in idx:
            pads = [zeros_half] * n_chunks
            pads[c] = uws[u][lo:hi, :GDN_DV] - rs[u][:GDN_CHUNK]
            v_pads.append(jnp.concatenate(pads, axis=0))
        new_states = []
        for u in idx:
            g_last = gccs[u][hi - 1:hi, :]
            tail = jnp.where(in_chunk, jnp.exp(jnp.where(in_chunk, g_last - gcrs[u], 0.0)), 0.0)
            new_states.append(states[u] * jnp.exp(g_last) + _bdot(kts[u] * tail, v_pads[u]))
        states = new_states
        for u in idx:
            outs[u].append(rs[u][GDN_CHUNK:] + _bdot(a_qks[u][lo:hi, :], v_pads[u]))
    for u, (bb, h) in enumerate(units):
        s_scr[bb, h] = states[u]
        o = _rms(jnp.concatenate(outs[u], axis=0)) * nw_ref[...]
        zg = _silu(z_ref[bb, :, h * GDN_DV:(h + 1) * GDN_DV])
        o_ref[bb, :, h * GDN_DV:(h + 1) * GDN_DV] = (o * zg).astype(BF16)


def _gdn(big, small, small_t, conv_w, a_log, dt_bias, norm_w, bsz, seq):
    nb = GDN_BATCH_TILE
    nt = seq // TIME_TILE
    qkv_w = 2 * GDN_QK_W + GDN_V_W
    big3 = big.reshape(bsz, seq, big.shape[1])
    small3 = small.reshape(bsz, seq, SMALL_W)

    def pad_row(v):
        return jnp.zeros((1, SMALL_W), F32).at[0, SMALL_ALPHA:SMALL_ALPHA + GDN_HEADS].set(v)

    alr = pad_row(a_log)
    dtr = pad_row(dt_bias)
    col = lambda cb: pl.BlockSpec((nb, TIME_TILE, MIX_WIDTH), lambda g, t: (g, t, cb))
    const = lambda shape: pl.BlockSpec(shape, lambda g, t: (0, 0))
    small_t_spec = lambda bb: pl.BlockSpec(
        (SMALL_W, TIME_TILE), lambda g, t: (0, (g * nb + bb) * nt + t))
    out = pl.pallas_call(
        _gdn_kernel,
        out_shape=jax.ShapeDtypeStruct((bsz, seq, MIX_WIDTH), BF16),
        grid=(bsz // nb, nt),
        in_specs=[
            col(0), col(1), col(2), col(3),
            pl.BlockSpec((nb, TIME_TILE, SMALL_W), lambda g, t: (g, t, 0)),
            *[small_t_spec(bb) for bb in range(nb)],
            const((CONV_K, qkv_w)),
            const((1, SMALL_W)), const((SMALL_W, 1)),
            const((1, SMALL_W)), const((SMALL_W, 1)),
            const((1, GDN_DV)),
        ],
        out_specs=pl.BlockSpec((nb, TIME_TILE, MIX_WIDTH), lambda g, t: (g, t, 0)),
        scratch_shapes=[pltpu.VMEM((nb, HALO + TIME_TILE, qkv_w), F32),
                        pltpu.VMEM((nb, GDN_HEADS, GDN_DK, GDN_DV), F32)],
        compiler_params=_params("parallel", "arbitrary"),
        name="gdn",
    )(big3, big3, big3, big3, small3, *([small_t] * nb), conv_w, alr, alr.T, dtr, dtr.T,
      norm_w.reshape(1, GDN_DV))
    return out.reshape(bsz * seq, MIX_WIDTH)


def _lru_kernel(x_ref, y_ref, cw_ref, cb_ref, wa_ref, ba_ref, wi_ref, bi_ref, lam_ref,
                o_ref, xbuf, h_scr):
    rows = LRU_TIME_TILE
    first = pl.program_id(1) == 0

    @pl.when(first)
    def _():
        xbuf[0:HALO, :] = jnp.zeros((HALO, LRU_WIDTH), F32)
        h_scr[...] = jnp.zeros(h_scr.shape, F32)

    xbuf[HALO:HALO + rows, :] = x_ref[...]
    xl = _causal_conv(xbuf, cw_ref, rows) + cb_ref[...]
    xbuf[0:HALO, :] = xbuf[rows:rows + HALO, :]

    xlb = xl.astype(BF16)
    r_parts = []
    i_parts = []
    for k in range(LRU_BLOCKS):
        blk = xlb[:, k * LRU_BLOCK_W:(k + 1) * LRU_BLOCK_W]
        r_parts.append(_dot(blk, wa_ref[k]))
        i_parts.append(_dot(blk, wi_ref[k]))
    r = _sigmoid(jnp.concatenate(r_parts, axis=1) + ba_ref[...])
    ig = _sigmoid(jnp.concatenate(i_parts, axis=1) + bi_ref[...])
    log_a = -LRU_C * r * _softplus(-lam_ref[...])
    a = jnp.exp(log_a)
    mult = jnp.sqrt(-jnp.tanh(log_a) * (a * a + 1.0))
    row = lax.broadcasted_iota(jnp.int32, (rows, 1), 0)
    mult = jnp.where(first & (row == 0), 1.0, mult)
    b = mult * ig * xl

    row8 = row[:SUBLANES]
    shift = 1
    while shift < SUBLANES:
        a_roll = pltpu.roll(a, shift, 0)
        b_roll = pltpu.roll(b, shift, 0)
        keep = row8 >= shift
        a_prev = jnp.concatenate([jnp.where(keep, a_roll[:SUBLANES], 1.0), a_roll[SUBLANES:]], axis=0)
        b_prev = jnp.concatenate([jnp.where(keep, b_roll[:SUBLANES], 0.0), b_roll[SUBLANES:]], axis=0)
        b = a * b_prev + b
        a = a * a_prev
        shift *= 2
    hg = a[:SUBLANES] * h_scr[0:1, :] + b[:SUBLANES]
    groups = [hg]
    for lo in range(SUBLANES, rows, SUBLANES):
        hg = a[lo:lo + SUBLANES] * hg + b[lo:lo + SUBLANES]
        groups.append(hg)
    hs = jnp.concatenate(groups, axis=0)
    h_scr[0:1, :] = hs[rows - 1:rows, :]
    o_ref[...] = (_gelu_tanh(y_ref[...]) * hs).astype(BF16)


def _lru(big, conv_w, conv_b, w_a, b_a, w_i, b_i, lam, bsz, seq):
    m = bsz * seq
    nt = seq // LRU_TIME_TILE
    base = BIG_LRU // LRU_WIDTH
    row1 = lambda v: v.reshape(1, LRU_WIDTH)
    const2 = lambda shape: pl.BlockSpec(shape, lambda b, t: (0, 0))
    const3 = lambda shape: pl.BlockSpec(shape, lambda b, t: (0, 0, 0))
    return pl.pallas_call(
        _lru_kernel,
        out_shape=jax.ShapeDtypeStruct((m, LRU_WIDTH), BF16),
        grid=(bsz, nt),
        in_specs=[
            pl.BlockSpec((LRU_TIME_TILE, LRU_WIDTH), lambda b, t: (b * nt + t, base)),
            pl.BlockSpec((LRU_TIME_TILE, LRU_WIDTH), lambda b, t: (b * nt + t, base + 1)),
            const2((CONV_K, LRU_WIDTH)), const2((1, LRU_WIDTH)),
            const3((LRU_BLOCKS, LRU_BLOCK_W, LRU_BLOCK_W)), const2((1, LRU_WIDTH)),
            const3((LRU_BLOCKS, LRU_BLOCK_W, LRU_BLOCK_W)), const2((1, LRU_WIDTH)),
            const2((1, LRU_WIDTH)),
        ],
        out_specs=pl.BlockSpec((LRU_TIME_TILE, LRU_WIDTH), lambda b, t: (b * nt + t, 0)),
        scratch_shapes=[pltpu.VMEM((HALO + LRU_TIME_TILE, LRU_WIDTH), F32),
                        pltpu.VMEM((SUBLANES, LRU_WIDTH), F32)],
        compiler_params=_params("parallel", "arbitrary"),
        name="rg_lru",
    )(big, big, conv_w, row1(conv_b), w_a.astype(BF16), row1(b_a), w_i.astype(BF16),
      row1(b_i), row1(lam))


def _gla_kernel(q_ref, k_ref, v_ref, z_ref, sm_ref, wg_ref, bg_ref, nw_ref, o_ref, st_scr):
    rows = TIME_TILE

    @pl.when(pl.program_id(1) == 0)
    def _():
        st_scr[...] = jnp.zeros(st_scr.shape, F32)

    nb = GLA_BATCH_TILE
    ri = lax.broadcasted_iota(jnp.int32, (rows, rows), 0)
    ci = lax.broadcasted_iota(jnp.int32, (rows, rows), 1)
    tri = (ci <= ri).astype(F32)
    bcums = []
    for bb in range(nb):
        log_a = _log_sigmoid(_dot(sm_ref[bb].astype(BF16), wg_ref[...]) + bg_ref[...]) / GLA_GATE_NORM
        bcums.append(_dot(tri, log_a, precision=HIGHEST))
    row = lax.broadcasted_iota(jnp.int32, (rows, 1), 0)

    def boundary_rows(bh, width, offset, first_zero):
        out = jnp.zeros_like(bh) if first_zero else None
        for g in range(rows // width):
            src = g * width + offset
            if src < 0:
                continue
            val = jnp.broadcast_to(bh[src:src + 1, :], bh.shape)
            out = val if out is None else jnp.where(row // width == g, val, out)
        return out

    units = [(bb, h) for bb in range(nb) for h in range(GLA_HEADS)]
    heads = range(len(units))
    bhs = [bcums[bb][:, h * GLA_DK:(h + 1) * GLA_DK] for bb, h in units]
    qh = [q_ref[bb, :, h * GLA_DK:(h + 1) * GLA_DK] * (GLA_DK ** -0.5) for bb, h in units]
    kh = [k_ref[bb, :, h * GLA_DK:(h + 1) * GLA_DK] for bb, h in units]
    vh = [v_ref[bb, :, h * GLA_DV:(h + 1) * GLA_DV] for bb, h in units]
    b_loc = [bhs[h] - boundary_rows(bhs[h], GLA_BASE_CHUNK, -1, True) for h in heads]
    base_mask = (ri // GLA_BASE_CHUNK == ci // GLA_BASE_CHUNK) & (ci <= ri)
    att = [jnp.where(base_mask,
                     _bdot_nt(qh[h] * jnp.exp(b_loc[h]), kh[h] * jnp.exp(-b_loc[h])), 0.0)
           for h in heads]
    s = GLA_BASE_CHUNK
    while s < rows:
        mask = ((ri // (2 * s) == ci // (2 * s)) & (ri % (2 * s) >= s) & (ci % (2 * s) < s))
        refs = [boundary_rows(bhs[h], 2 * s, s - 1, False) for h in heads]
        cross = [_bdot_nt(qh[h] * jnp.exp(jnp.minimum(bhs[h] - refs[h], 0.0)),
                          kh[h] * jnp.exp(jnp.minimum(refs[h] - bhs[h], 0.0))) for h in heads]
        att = [att[h] + jnp.where(mask, cross[h], 0.0) for h in heads]
        s *= 2
    sts = [st_scr[bb, h] for bb, h in units]
    o_inter = [_bdot_nt(qh[u] * jnp.exp(bhs[u]), sts[u]) for u in heads]
    o_intra = [_bdot(att[u], vh[u]) for u in heads]
    b_end = [bhs[u][rows - 1:rows, :] for u in heads]
    upd = [_bdot(vh[u].T, kh[u] * jnp.exp(b_end[u] - bhs[u])) for u in heads]
    for u, (bb, h) in enumerate(units):
        st_scr[bb, h] = sts[u] * jnp.exp(b_end[u]) + upd[u]
        o = _rms(o_intra[u] + o_inter[u]) * nw_ref[...]
        zg = _silu(z_ref[bb, :, h * GLA_DV:(h + 1) * GLA_DV])
        o_ref[bb, :, h * GLA_DV:(h + 1) * GLA_DV] = (o * zg).astype(BF16)


def _gla(big, small, w_gate, b_gate, norm_w, bsz, seq):
    nb = GLA_BATCH_TILE
    nt = seq // TIME_TILE
    big3 = big.reshape(bsz, seq, big.shape[1])
    small3 = small.reshape(bsz, seq, SMALL_W)
    wg = jnp.zeros((SMALL_W, GLA_QK_W), F32).at[SMALL_GK:SMALL_GK + GLA_GATE_RANK].set(w_gate)
    qk_base = BIG_GLA // GLA_QK_W
    v_base = (BIG_GLA + 2 * GLA_QK_W) // GLA_V_W
    const = lambda shape: pl.BlockSpec(shape, lambda g, t: (0, 0))
    tile = lambda width, cb: pl.BlockSpec((nb, TIME_TILE, width), lambda g, t: (g, t, cb))
    out = pl.pallas_call(
        _gla_kernel,
        out_shape=jax.ShapeDtypeStruct((bsz, seq, MIX_WIDTH), BF16),
        grid=(bsz // nb, nt),
        in_specs=[
            tile(GLA_QK_W, qk_base), tile(GLA_QK_W, qk_base + 1),
            tile(GLA_V_W, v_base), tile(GLA_V_W, v_base + 1),
            tile(SMALL_W, 0),
            const((SMALL_W, GLA_QK_W)), const((1, GLA_QK_W)), const((1, GLA_DV)),
        ],
        out_specs=tile(MIX_WIDTH, 0),
        scratch_shapes=[pltpu.VMEM((nb, GLA_HEADS, GLA_DV, GLA_DK), F32)],
        compiler_params=_params("parallel", "arbitrary"),
        name="gla",
    )(big3, big3, big3, big3, small3, wg.astype(BF16), b_gate.reshape(1, GLA_QK_W),
      norm_w.reshape(1, GLA_DV))
    return out.reshape(bsz * seq, MIX_WIDTH)


def _merge_kernel(h_ref, wga_ref, wgb_ref, wgc_ref, oa_ref, ob_ref, oc_ref, wb_ref, o_ref):
    h = h_ref[...]
    acc = _sigmoid(_dot(h, wga_ref[...])) * _dot(oa_ref[...], wb_ref[0])
    acc = acc + _sigmoid(_dot(h, wgb_ref[...])) * _dot(ob_ref[...], wb_ref[1])
    acc = acc + _sigmoid(_dot(h, wgc_ref[...])) * _dot(oc_ref[...], wb_ref[2])
    o_ref[...] = acc.astype(BF16)


def _merge(h, oa, ob, oc, w_big, w_branch, layer, seq):
    m = oa.shape[0]
    tm = min(MERGE_ROW_TILE, seq)
    tn = MERGE_COL_TILE
    gate_w = lambda n: pl.BlockSpec(
        (None, D_MODEL, tn), lambda i, j: (layer, 0, (BIG_GATES + n * D_MODEL) // tn + j))
    branch = pl.BlockSpec((tm, MIX_WIDTH), lambda i, j: (i, 0))
    return pl.pallas_call(
        _merge_kernel,
        out_shape=jax.ShapeDtypeStruct((m, D_MODEL), BF16),
        grid=(m // tm, D_MODEL // tn),
        in_specs=[pl.BlockSpec((tm, D_MODEL), lambda i, j: (i, 0)),
                  gate_w(0), gate_w(1), gate_w(2), branch, branch, branch,
                  pl.BlockSpec((None, N_BRANCH, MIX_WIDTH, tn), lambda i, j: (layer, 0, 0, j))],
        out_specs=pl.BlockSpec((tm, tn), lambda i, j: (i, j)),
        compiler_params=_params("parallel", "arbitrary"),
        name="merge",
    )(h, w_big, w_big, w_big, oa, ob, oc, w_branch)


def _out_proj_kernel(m_ref, x_ref, mod_ref, g_ref, w_ref, o_ref):
    y = _dot(m_ref[...], w_ref[...])
    o_ref[...] = x_ref[...] + mod_ref[MOD_GT1:MOD_GT1 + 1, :] * (_rms(y) * g_ref[...])


def _out_proj(merged, x2, mod, g, w_out, layer, seq):
    m = x2.shape[0]
    tm = min(OUT_PROJ_ROW_TILE, seq)
    tiles_per_seq = seq // tm
    return pl.pallas_call(
        _out_proj_kernel,
        out_shape=jax.ShapeDtypeStruct((m, D_MODEL), F32),
        grid=(m // tm,),
        in_specs=[
            pl.BlockSpec((tm, D_MODEL), lambda i: (i, 0)),
            pl.BlockSpec((tm, D_MODEL), lambda i: (i, 0)),
            pl.BlockSpec((None, N_MOD, D_MODEL), lambda i: (i // tiles_per_seq, 0, 0)),
            pl.BlockSpec((1, D_MODEL), lambda i: (0, 0)),
            pl.BlockSpec((None, D_MODEL, D_MODEL), lambda i: (layer, 0, 0)),
        ],
        out_specs=pl.BlockSpec((tm, D_MODEL), lambda i: (i, 0)),
        compiler_params=_params("parallel"),
        name="out_proj",
    )(merged, x2, mod, g, w_out)


def _mlp_kernel(xb_ref, xe_ref, modb_ref, mode_ref, gpre_ref, gpost_ref, w1_ref, w2_ref, o_ref,
                h_scr, acc_scr, *, n_tiles):
    r = pl.program_id(0)
    f = pl.program_id(1)
    chunk = xb_ref.shape[0]
    rows = pl.ds(pl.multiple_of(f * chunk, chunk), chunk)

    def build():
        h = _rms(xb_ref[...]) * gpre_ref[...]
        h = h * (1.0 + modb_ref[MOD_SC2:MOD_SC2 + 1, :]) + modb_ref[MOD_SH2:MOD_SH2 + 1, :]
        h_scr[r % 2, rows, :] = h.astype(BF16)

    def finish():
        y = _rms(acc_scr[r % 2, rows, :]) * gpost_ref[...]
        o_ref[...] = xe_ref[...] + mode_ref[MOD_GT2:MOD_GT2 + 1, :] * y
        acc_scr[r % 2, rows, :] = jnp.zeros((chunk, acc_scr.shape[2]), F32)

    @pl.when(r == 0)
    def _():
        @pl.when(f == 0)
        def _():
            acc_scr[...] = jnp.zeros(acc_scr.shape, F32)

        build()
        o_ref[...] = jnp.zeros(o_ref.shape, F32)

    @pl.when((r >= 1) & (r <= n_tiles))
    def _():
        prev = (r + 1) % 2
        finish()
        a = jnp.maximum(_dot(h_scr[prev], w1_ref[...]), 0.0)
        acc_scr[prev] += _dot((a * a).astype(BF16), w2_ref[...])
        build()

    @pl.when(r == n_tiles + 1)
    def _():
        finish()


def _mlp(x2, mod, g_pre, g_post, w1, w2, layer, seq):
    m = x2.shape[0]
    tm = min(MLP_ROW_TILE, seq)
    tf = MLP_FF_TILE
    n_f = D_FF // tf
    n_i = m // tm
    chunk = tm // n_f
    tiles_per_seq = seq // tm
    build = lambda r: jnp.minimum(r, n_i - 1)
    done = lambda r: jnp.clip(r - 2, 0, n_i - 1)
    wf = lambda r, f: jnp.where(r == 0, 0, jnp.where(r == n_i + 1, n_f - 1, f))
    return pl.pallas_call(
        functools.partial(_mlp_kernel, n_tiles=n_i),
        out_shape=jax.ShapeDtypeStruct((m, D_MODEL), F32),
        grid=(n_i + 2, n_f),
        in_specs=[
            pl.BlockSpec((chunk, D_MODEL), lambda r, f: (build(r) * n_f + f, 0)),
            pl.BlockSpec((chunk, D_MODEL), lambda r, f: (done(r) * n_f + f, 0)),
            pl.BlockSpec((None, N_MOD, D_MODEL), lambda r, f: (build(r) // tiles_per_seq, 0, 0)),
            pl.BlockSpec((None, N_MOD, D_MODEL), lambda r, f: (done(r) // tiles_per_seq, 0, 0)),
            pl.BlockSpec((1, D_MODEL), lambda r, f: (0, 0)),
            pl.BlockSpec((1, D_MODEL), lambda r, f: (0, 0)),
            pl.BlockSpec((None, D_MODEL, tf), lambda r, f: (layer, 0, wf(r, f))),
            pl.BlockSpec((None, tf, D_MODEL), lambda r, f: (layer, wf(r, f), 0)),
        ],
        out_specs=pl.BlockSpec((chunk, D_MODEL),
                               lambda r, f: (done(r) * n_f + jnp.where(r < 2, 0, f), 0)),
        scratch_shapes=[pltpu.VMEM((2, tm, D_MODEL), BF16), pltpu.VMEM((2, tm, D_MODEL), F32)],
        compiler_params=_params("arbitrary", "arbitrary"),
        name="mlp",
    )(x2, x2, mod, mod, g_pre, g_post, w1, w2)


def kernel(x, c, w_ada, b_ada, g_pre_mix, g_post_mix, g_pre_mlp, g_post_mlp, w_in, conv_gdn, gdn_a_log, gdn_dt_bias, gdn_norm, conv_lru, conv_lru_b, lru_w_a, lru_b_a, lru_w_i, lru_b_i, lru_lambda, gla_w_gate, gla_b_gate, gla_norm, w_branch, w_out, w_mlp1, w_mlp2):
    bsz, seq, _ = x.shape
    n_layers = w_in.shape[0]
    assert seq % LRU_TIME_TILE == 0 and x.shape[2] == D_MODEL and bsz % GDN_BATCH_TILE == 0
    row = lambda v: v.reshape(1, D_MODEL)

    mod_all = _ada_mod(c, w_ada, b_ada).reshape(n_layers, bsz, N_MOD, D_MODEL)
    x2 = x.reshape(bsz * seq, D_MODEL)
    w_big, w_small_t = _split_w_in(w_in)
    w_branch_b = w_branch.astype(BF16)
    w_out_b = w_out.astype(BF16)
    w_mlp1_b = w_mlp1.astype(BF16)
    w_mlp2_b = w_mlp2.astype(BF16)
    for l in range(n_layers):
        mod = mod_all[l]
        big, small, small_t, h = _in_proj(x2, mod, row(g_pre_mix[l]), w_big, w_small_t, l, seq)
        oa = _gdn(big, small, small_t, conv_gdn[l], gdn_a_log[l], gdn_dt_bias[l], gdn_norm[l],
                  bsz, seq)
        ob = _lru(big, conv_lru[l], conv_lru_b[l], lru_w_a[l], lru_b_a[l], lru_w_i[l],
                  lru_b_i[l], lru_lambda[l], bsz, seq)
        oc = _gla(big, small, gla_w_gate[l], gla_b_gate[l], gla_norm[l], bsz, seq)
        merged = _merge(h, oa, ob, oc, w_big, w_branch_b, l, seq)
        x2 = _out_proj(merged, x2, mod, row(g_post_mix[l]), w_out_b, l, seq)
        x2 = _mlp(x2, mod, row(g_pre_mlp[l]), row(g_post_mlp[l]), w_mlp1_b, w_mlp2_b, l, seq)
    return x2.reshape(bsz, seq, D_MODEL)
```

```python
import functools

import jax
import jax.numpy as jnp
from jax import lax
from jax.experimental import pallas as pl
from jax.experimental.pallas import tpu as pltpu

F32 = jnp.float32
BF16 = jnp.bfloat16
HIGHEST = lax.Precision.HIGHEST

D_MODEL = 2048
MIX_WIDTH = D_MODEL // 2
N_BRANCH = 3
GDN_DK = 128
GDN_DV = 128
GDN_HEADS = MIX_WIDTH // GDN_DV
GDN_CHUNK = 64
LRU_WIDTH = MIX_WIDTH
LRU_BLOCKS = 8
LRU_BLOCK_W = LRU_WIDTH // LRU_BLOCKS
LRU_C = 8.0
GLA_HEADS = 4
GLA_DV = MIX_WIDTH // GLA_HEADS
GLA_DK = GLA_DV // 2
GLA_GATE_RANK = 16
GLA_GATE_NORM = 16.0
GLA_BASE_CHUNK = 16
CONV_K = 4
D_FF = 4 * D_MODEL
N_MOD = 6
EPS = 1e-6

GDN_QK_W = GDN_HEADS * GDN_DK
GDN_V_W = GDN_HEADS * GDN_DV
GLA_QK_W = GLA_HEADS * GLA_DK
GLA_V_W = GLA_HEADS * GLA_DV
IN_SPLITS = (GDN_QK_W, GDN_QK_W, GDN_V_W, GDN_V_W, GDN_HEADS, GDN_HEADS,
             LRU_WIDTH, LRU_WIDTH,
             GLA_QK_W, GLA_QK_W, GLA_V_W, GLA_V_W, GLA_GATE_RANK,
             N_BRANCH * D_MODEL)
IN_OFFS = tuple(sum(IN_SPLITS[:i]) for i in range(len(IN_SPLITS) + 1))

LANES = 128
SUBLANES = 8
V7X_VMEM_BYTES = 64 * 1024 * 1024
VMEM_LIMIT_BYTES = V7X_VMEM_BYTES - 8 * 1024 * 1024

BIG_GDN = 0
BIG_LRU = BIG_GDN + 4 * MIX_WIDTH
BIG_GLA = BIG_LRU + 2 * LRU_WIDTH
BIG_GATES = BIG_GLA + 2 * GLA_QK_W + 2 * GLA_V_W
BIG_W = BIG_GATES + N_BRANCH * D_MODEL
SMALL_BETA = 0
SMALL_ALPHA = GDN_HEADS
SMALL_GK = 2 * GDN_HEADS
SMALL_W = LANES

MOD_SH1, MOD_SC1, MOD_GT1, MOD_SH2, MOD_SC2, MOD_GT2 = range(N_MOD)

ADA_COL_TILE = 1024
PROJ_ROW_TILE = 1024
PROJ_COL_TILE = 1536
MERGE_ROW_TILE = 1024
MERGE_COL_TILE = 512
OUT_PROJ_ROW_TILE = 512
MLP_ROW_TILE = 1024
MLP_FF_TILE = 1024
TIME_TILE = 128
GDN_BATCH_TILE = 2
GLA_BATCH_TILE = 4
LRU_TIME_TILE = 512
HALO = SUBLANES

W_PREP_TILE = 512
W_PREP_RUN1_TILE = IN_OFFS[4] // W_PREP_TILE
W_PREP_RUN2_TILE = (IN_OFFS[4] + IN_OFFS[12] - IN_OFFS[6]) // W_PREP_TILE
W_PREP_SHIFT1 = IN_OFFS[6] - IN_OFFS[4]
W_PREP_SHIFT2 = W_PREP_SHIFT1 + IN_OFFS[13] - IN_OFFS[12]
W_PREP_NEXT = W_PREP_SHIFT2
W_PREP_NARROW1 = slice(IN_OFFS[4] - W_PREP_RUN1_TILE * W_PREP_TILE,
                       IN_OFFS[6] - W_PREP_RUN1_TILE * W_PREP_TILE)
W_PREP_NARROW2 = slice(IN_OFFS[12] - W_PREP_RUN2_TILE * W_PREP_TILE,
                       IN_OFFS[13] - W_PREP_RUN2_TILE * W_PREP_TILE)
assert (W_PREP_NARROW1.start, W_PREP_NARROW1.stop) == (SMALL_BETA, SMALL_GK)
assert (W_PREP_NARROW2.start, W_PREP_NARROW2.stop) == (SMALL_GK, SMALL_GK + GLA_GATE_RANK)
assert W_PREP_TILE % W_PREP_NEXT == 0 and IN_OFFS[-1] % W_PREP_NEXT == 0


def _sigmoid(x):
    return 1.0 / (1.0 + jnp.exp(-x))


def _silu(x):
    return x * _sigmoid(x)


def _softplus(x):
    return jnp.maximum(x, 0.0) + jnp.log(1.0 + jnp.exp(-jnp.abs(x)))


def _log_sigmoid(x):
    return -_softplus(-x)


def _gelu_tanh(x):
    c = 0.7978845608028654
    return 0.5 * x * (1.0 + jnp.tanh(c * (x + 0.044715 * (x * x * x))))


def _rms(x):
    return x * lax.rsqrt(jnp.mean(x * x, axis=-1, keepdims=True) + EPS)


def _dot(a, b, precision=None):
    return jnp.dot(a, b, preferred_element_type=F32, precision=precision)


def _dot_nt(a, b):
    return lax.dot_general(a, b, (((1,), (1,)), ((), ())), preferred_element_type=F32)


def _bdot(a, b):
    return _dot(a.astype(BF16), b.astype(BF16))


def _bdot_nt(a, b):
    return _dot_nt(a.astype(BF16), b.astype(BF16))


def _params(*sem):
    return pltpu.CompilerParams(dimension_semantics=sem, vmem_limit_bytes=VMEM_LIMIT_BYTES)


def _ada_kernel(c_ref, w_ref, b_ref, o_ref):
    sc = _silu(c_ref[...]).astype(BF16)
    o_ref[...] = _dot(sc, w_ref[...].astype(BF16)) + b_ref[...]


def _ada_mod(c, w_ada, b_ada):
    n_layers, _, n_out = w_ada.shape
    bsz = c.shape[0]
    tn = ADA_COL_TILE
    return pl.pallas_call(
        _ada_kernel,
        out_shape=jax.ShapeDtypeStruct((n_layers, bsz, n_out), F32),
        grid=(n_layers, n_out // tn),
        in_specs=[
            pl.BlockSpec((bsz, D_MODEL), lambda l, j: (0, 0)),
            pl.BlockSpec((None, D_MODEL, tn), lambda l, j: (l, 0, j)),
            pl.BlockSpec((None, 1, tn), lambda l, j: (l, 0, j)),
        ],
        out_specs=pl.BlockSpec((None, bsz, tn), lambda l, j: (l, 0, j)),
        compiler_params=_params("parallel", "parallel"),
        name="ada_mod",
    )(c, w_ada, b_ada.reshape(n_layers, 1, n_out))


def _w_prep_kernel(a_ref, b_ref, o_ref, wst_ref):
    j = pl.program_id(1)

    def emit(shift):
        src = a_ref[...] if shift == 0 else jnp.concatenate([a_ref[shift:, :], b_ref[:shift, :]],
                                                            axis=0)
        o_ref[...] = src.T.astype(BF16)

    @pl.when(j == 0)
    def _():
        wst_ref[...] = jnp.zeros(wst_ref.shape, BF16)

    @pl.when(j < W_PREP_RUN1_TILE)
    def _():
        emit(0)

    @pl.when((j >= W_PREP_RUN1_TILE) & (j < W_PREP_RUN2_TILE))
    def _():
        emit(W_PREP_SHIFT1)

    @pl.when(j >= W_PREP_RUN2_TILE)
    def _():
        emit(W_PREP_SHIFT2)

    @pl.when(j == W_PREP_RUN1_TILE)
    def _():
        wst_ref[W_PREP_NARROW1, :] = a_ref[W_PREP_NARROW1, :].astype(BF16)

    @pl.when(j == W_PREP_RUN2_TILE)
    def _():
        wst_ref[W_PREP_NARROW2, :] = a_ref[W_PREP_NARROW2, :].astype(BF16)


def _split_w_in(w_in):
    n_layers = w_in.shape[0]
    w_t = jnp.swapaxes(w_in, 1, 2)
    next_per_tile = W_PREP_TILE // W_PREP_NEXT
    w_big, w_small_t = pl.pallas_call(
        _w_prep_kernel,
        out_shape=(jax.ShapeDtypeStruct((n_layers, D_MODEL, BIG_W), BF16),
                   jax.ShapeDtypeStruct((n_layers, SMALL_W, D_MODEL), BF16)),
        grid=(n_layers, BIG_W // W_PREP_TILE),
        in_specs=[
            pl.BlockSpec((None, W_PREP_TILE, D_MODEL), lambda l, j: (l, j, 0)),
            pl.BlockSpec((None, W_PREP_NEXT, D_MODEL), lambda l, j: (l, (j + 1) * next_per_tile, 0)),
        ],
        out_specs=(pl.BlockSpec((None, D_MODEL, W_PREP_TILE), lambda l, j: (l, 0, j)),
                   pl.BlockSpec((None, SMALL_W, D_MODEL), lambda l, j: (l, 0, 0))),
        compiler_params=_params("parallel", "arbitrary"),
        name="w_prep",
    )(w_t, w_t)
    return w_big, w_small_t


def _in_proj_kernel(xn_ref, modn_ref, g_ref, w_ref, wst_ref,
                    big_ref, small_ref, smallt_ref, h_ref, h_scr, *, chunks_per_tile):
    r = pl.program_id(0)
    j = pl.program_id(1)
    chunk = xn_ref.shape[0]

    def build_chunk():
        h = _rms(xn_ref[...]) * g_ref[...]
        h = h * (1.0 + modn_ref[MOD_SC1:MOD_SC1 + 1, :]) + modn_ref[MOD_SH1:MOD_SH1 + 1, :]
        row0 = pl.multiple_of(jnp.minimum(j, chunks_per_tile - 1) * chunk, chunk)
        h_scr[r % 2, pl.ds(row0, chunk), :] = h.astype(BF16)

    @pl.when(r == 0)
    def _():
        build_chunk()

    @pl.when(r > 0)
    def _():
        prev = (r + 1) % 2

        @pl.when(j == 0)
        def _():
            hb = h_scr[prev]
            h_ref[...] = hb
            st = _dot_nt(wst_ref[...], hb)
            smallt_ref[...] = st
            small_ref[...] = st.T

        big_ref[...] = _dot(h_scr[prev], w_ref[...])
        build_chunk()


def _in_proj(x2, mod, g, w_big, w_small_t, layer, seq):
    m = x2.shape[0]
    tm = min(PROJ_ROW_TILE, seq)
    tn = PROJ_COL_TILE
    n_i = m // tm
    n_j = BIG_GATES // tn
    chunks_per_tile = 1 << (min(n_j, SUBLANES).bit_length() - 1)
    chunk = tm // chunks_per_tile
    tiles_per_seq = seq // tm
    build = lambda r: jnp.minimum(r, n_i - 1)
    out = lambda r: jnp.maximum(r - 1, 0)
    return pl.pallas_call(
        functools.partial(_in_proj_kernel, chunks_per_tile=chunks_per_tile),
        out_shape=(jax.ShapeDtypeStruct((m, BIG_GATES), F32),
                   jax.ShapeDtypeStruct((m, SMALL_W), F32),
                   jax.ShapeDtypeStruct((SMALL_W, m), F32),
                   jax.ShapeDtypeStruct((m, D_MODEL), BF16)),
        grid=(n_i + 1, n_j),
        in_specs=[
            pl.BlockSpec((chunk, D_MODEL), lambda r, j: (
                build(r) * chunks_per_tile + jnp.minimum(j, chunks_per_tile - 1), 0)),
            pl.BlockSpec((None, N_MOD, D_MODEL), lambda r, j: (build(r) // tiles_per_seq, 0, 0)),
            pl.BlockSpec((1, D_MODEL), lambda r, j: (0, 0)),
            pl.BlockSpec((None, D_MODEL, tn), lambda r, j: (layer, 0, j)),
            pl.BlockSpec((None, SMALL_W, D_MODEL), lambda r, j: (layer, 0, 0)),
        ],
        out_specs=(pl.BlockSpec((tm, tn), lambda r, j: (out(r), jnp.where(r == 0, 0, j))),
                   pl.BlockSpec((tm, SMALL_W), lambda r, j: (out(r), 0)),
                   pl.BlockSpec((SMALL_W, tm), lambda r, j: (0, out(r))),
                   pl.BlockSpec((tm, D_MODEL), lambda r, j: (out(r), 0))),
        scratch_shapes=[pltpu.VMEM((2, tm, D_MODEL), BF16)],
        compiler_params=_params("arbitrary", "arbitrary"),
        name="in_proj",
    )(x2, mod, g, w_big, w_small_t)


def _causal_conv(xbuf, cw_ref, rows):
    acc = cw_ref[CONV_K - 1:CONV_K, :] * xbuf[HALO:HALO + rows, :]
    for j in range(CONV_K - 1):
        off = HALO - (CONV_K - 1) + j
        acc = acc + cw_ref[j:j + 1, :] * xbuf[off:off + rows, :]
    return acc


def _gdn_kernel(*refs):
    nb = GDN_BATCH_TILE
    q_ref, k_ref, v_ref, z_ref, sm_ref = refs[:5]
    smt_refs = refs[5:5 + nb]
    cw_ref, alr_ref, alc_ref, dtr_ref, dtc_ref, nw_ref, o_ref, xbuf, s_scr = refs[5 + nb:]
    rows = TIME_TILE

    @pl.when(pl.program_id(1) == 0)
    def _():
        xbuf[:, 0:HALO, :] = jnp.zeros((nb, HALO, xbuf.shape[2]), F32)
        s_scr[...] = jnp.zeros(s_scr.shape, F32)

    ri = lax.broadcasted_iota(jnp.int32, (rows, rows), 0)
    ci = lax.broadcasted_iota(jnp.int32, (rows, rows), 1)
    same = (ri // GDN_CHUNK) == (ci // GDN_CHUNK)
    causal = same & (ci <= ri)
    strict = same & (ci < ri)
    cum_c = causal.astype(F32)
    cum_r = (same & (ri <= ci)).astype(F32)
    eye = (ri == ci).astype(F32)
    level_masks = []
    s = 1
    while s < GDN_CHUNK:
        level_masks.append((ri // (2 * s) == ci // (2 * s)) & (ri % (2 * s) >= s) & (ci % (2 * s) < s))
        s *= 2
    zeros_half = jnp.zeros((GDN_CHUNK, GDN_DV), F32)
    col_row = lax.broadcasted_iota(jnp.int32, (1, rows), 1)
    n_chunks = rows // GDN_CHUNK

    units = [(bb, h) for bb in range(nb) for h in range(GDN_HEADS)]
    idx = range(len(units))
    qs, kts, gccs, gcrs, a_qks, a_kks, rhss = [], [], [], [], [], [], []
    for bb in range(nb):
        xb = xbuf.at[bb]
        xb[HALO:HALO + rows, 0:GDN_QK_W] = q_ref[bb]
        xb[HALO:HALO + rows, GDN_QK_W:2 * GDN_QK_W] = k_ref[bb]
        xb[HALO:HALO + rows, 2 * GDN_QK_W:2 * GDN_QK_W + GDN_V_W] = v_ref[bb]
        qkv = _silu(_causal_conv(xb, cw_ref, rows))
        xb[0:HALO, :] = xb[rows:rows + HALO, :]

        sm = sm_ref[bb]
        smt = smt_refs[bb][...]
        beta_c = _sigmoid(sm)
        g_c = -jnp.exp(alr_ref[...]) * _softplus(sm + dtr_ref[...])
        g_r = -jnp.exp(alc_ref[...]) * _softplus(smt + dtc_ref[...])
        gc_c = _dot(cum_c, g_c, precision=HIGHEST)
        gc_r = _dot(g_r, cum_r, precision=HIGHEST)

        for h in range(GDN_HEADS):
            qh = qkv[:, h * GDN_DK:(h + 1) * GDN_DK]
            kh = qkv[:, GDN_QK_W + h * GDN_DK:GDN_QK_W + (h + 1) * GDN_DK]
            vh = qkv[:, 2 * GDN_QK_W + h * GDN_DV:2 * GDN_QK_W + (h + 1) * GDN_DV]
            qh = qh * lax.rsqrt(jnp.sum(qh * qh, axis=-1, keepdims=True) + EPS) * (GDN_DK ** -0.5)
            kh = kh * lax.rsqrt(jnp.sum(kh * kh, axis=-1, keepdims=True) + EPS)
            bc = beta_c[:, SMALL_BETA + h:SMALL_BETA + h + 1]
            gcc = gc_c[:, SMALL_ALPHA + h:SMALL_ALPHA + h + 1]
            gcr = gc_r[SMALL_ALPHA + h:SMALL_ALPHA + h + 1, :]
            e = jnp.exp(jnp.where(causal, gcc - gcr, 0.0))
            kb = kh * bc
            kt = kh.T
            qk_kk = _bdot(jnp.concatenate([qh, kb], axis=0), kt)
            eg = jnp.exp(gcc)
            qs.append(qh * eg)
            kts.append(kt)
            gccs.append(gcc)
            gcrs.append(gcr)
            a_qks.append(jnp.where(causal, qk_kk[:rows] * e, 0.0))
            a_kks.append(jnp.where(strict, qk_kk[rows:] * e, 0.0))
            rhss.append(jnp.concatenate([vh * bc, kb * eg], axis=1))

    tinv = [eye - jnp.where(level_masks[0], a_kks[u], 0.0) for u in idx]
    for lm in level_masks[1:]:
        tb = [tinv[u].astype(BF16) for u in idx]
        et = [_dot(jnp.where(lm, a_kks[u], 0.0).astype(BF16), tb[u]) for u in idx]
        tinv = [tinv[u] - _dot(tb[u], et[u].astype(BF16)) for u in idx]
    uws = [_bdot(tinv[u], rhss[u]) for u in idx]

    states = [s_scr[bb, h] for bb, h in units]
    outs = [[] for _ in idx]
    for c in range(n_chunks):
        lo = c * GDN_CHUNK
        hi = lo + GDN_CHUNK
        in_chunk = (col_row >= lo) & (col_row < hi)
        rs = [_bdot(jnp.concatenate([uws[u][lo:hi, GDN_DV:], qs[u][lo:hi]], axis=0), states[u])
              for u in idx]
        v_pads = []
        for u in idx:
            pads = [zeros_half] * n_chunks
            pads[c] = uws[u][lo:hi, :GDN_DV] - rs[u][:GDN_CHUNK]
            v_pads.append(jnp.concatenate(pads, axis=0))
        new_states = []
        for u in idx:
            g_last = gccs[u][hi - 1:hi, :]
            tail = jnp.where(in_chunk, jnp.exp(jnp.where(in_chunk, g_last - gcrs[u], 0.0)), 0.0)
            new_states.append(states[u] * jnp.exp(g_last) + _bdot(kts[u] * tail, v_pads[u]))
        states = new_states
        for u in idx:
            outs[u].append(rs[u][GDN_CHUNK:] + _bdot(a_qks[u][lo:hi, :], v_pads[u]))
    for u, (bb, h) in enumerate(units):
        s_scr[bb, h] = states[u]
        o = _rms(jnp.concatenate(outs[u], axis=0)) * nw_ref[...]
        zg = _silu(z_ref[bb, :, h * GDN_DV:(h + 1) * GDN_DV])
        o_ref[bb, :, h * GDN_DV:(h + 1) * GDN_DV] = (o * zg).astype(BF16)


def _gdn(big, small, small_t, conv_w, a_log, dt_bias, norm_w, bsz, seq):
    nb = GDN_BATCH_TILE
    nt = seq // TIME_TILE
    qkv_w = 2 * GDN_QK_W + GDN_V_W
    big3 = big.reshape(bsz, seq, big.shape[1])
    small3 = small.reshape(bsz, seq, SMALL_W)

    def pad_row(v):
        return jnp.zeros((1, SMALL_W), F32).at[0, SMALL_ALPHA:SMALL_ALPHA + GDN_HEADS].set(v)

    alr = pad_row(a_log)
    dtr = pad_row(dt_bias)
    col = lambda cb: pl.BlockSpec((nb, TIME_TILE, MIX_WIDTH), lambda g, t: (g, t, cb))
    const = lambda shape: pl.BlockSpec(shape, lambda g, t: (0, 0))
    small_t_spec = lambda bb: pl.BlockSpec(
        (SMALL_W, TIME_TILE), lambda g, t: (0, (g * nb + bb) * nt + t))
    out = pl.pallas_call(
        _gdn_kernel,
        out_shape=jax.ShapeDtypeStruct((bsz, seq, MIX_WIDTH), BF16),
        grid=(bsz // nb, nt),
        in_specs=[
            col(0), col(1), col(2), col(3),
            pl.BlockSpec((nb, TIME_TILE, SMALL_W), lambda g, t: (g, t, 0)),
            *[small_t_spec(bb) for bb in range(nb)],
            const((CONV_K, qkv_w)),
            const((1, SMALL_W)), const((SMALL_W, 1)),
            const((1, SMALL_W)), const((SMALL_W, 1)),
            const((1, GDN_DV)),
        ],
        out_specs=pl.BlockSpec((nb, TIME_TILE, MIX_WIDTH), lambda g, t: (g, t, 0)),
        scratch_shapes=[pltpu.VMEM((nb, HALO + TIME_TILE, qkv_w), F32),
                        pltpu.VMEM((nb, GDN_HEADS, GDN_DK, GDN_DV), F32)],
        compiler_params=_params("parallel", "arbitrary"),
        name="gdn",
    )(big3, big3, big3, big3, small3, *([small_t] * nb), conv_w, alr, alr.T, dtr, dtr.T,
      norm_w.reshape(1, GDN_DV))
    return out.reshape(bsz * seq, MIX_WIDTH)


def _lru_kernel(x_ref, y_ref, cw_ref, cb_ref, wa_ref, ba_ref, wi_ref, bi_ref, lam_ref,
                o_ref, xbuf, h_scr):
    rows = LRU_TIME_TILE
    first = pl.program_id(1) == 0

    @pl.when(first)
    def _():
        xbuf[0:HALO, :] = jnp.zeros((HALO, LRU_WIDTH), F32)
        h_scr[...] = jnp.zeros(h_scr.shape, F32)

    xbuf[HALO:HALO + rows, :] = x_ref[...]
    xl = _causal_conv(xbuf, cw_ref, rows) + cb_ref[...]
    xbuf[0:HALO, :] = xbuf[rows:rows + HALO, :]

    xlb = xl.astype(BF16)
    r_parts = []
    i_parts = []
    for k in range(LRU_BLOCKS):
        blk = xlb[:, k * LRU_BLOCK_W:(k + 1) * LRU_BLOCK_W]
        r_parts.append(_dot(blk, wa_ref[k]))
        i_parts.append(_dot(blk, wi_ref[k]))
    r = _sigmoid(jnp.concatenate(r_parts, axis=1) + ba_ref[...])
    ig = _sigmoid(jnp.concatenate(i_parts, axis=1) + bi_ref[...])
    log_a = -LRU_C * r * _softplus(-lam_ref[...])
    a = jnp.exp(log_a)
    mult = jnp.sqrt(-jnp.tanh(log_a) * (a * a + 1.0))
    row = lax.broadcasted_iota(jnp.int32, (rows, 1), 0)
    mult = jnp.where(first & (row == 0), 1.0, mult)
    b = mult * ig * xl

    row8 = row[:SUBLANES]
    shift = 1
    while shift < SUBLANES:
        a_roll = pltpu.roll(a, shift, 0)
        b_roll = pltpu.roll(b, shift, 0)
        keep = row8 >= shift
        a_prev = jnp.concatenate([jnp.where(keep, a_roll[:SUBLANES], 1.0), a_roll[SUBLANES:]], axis=0)
        b_prev = jnp.concatenate([jnp.where(keep, b_roll[:SUBLANES], 0.0), b_roll[SUBLANES:]], axis=0)
        b = a * b_prev + b
        a = a * a_prev
        shift *= 2
    hg = a[:SUBLANES] * h_scr[0:1, :] + b[:SUBLANES]
    groups = [hg]
    for lo in range(SUBLANES, rows, SUBLANES):
        hg = a[lo:lo + SUBLANES] * hg + b[lo:lo + SUBLANES]
        groups.append(hg)
    hs = jnp.concatenate(groups, axis=0)
    h_scr[0:1, :] = hs[rows - 1:rows, :]
    o_ref[...] = (_gelu_tanh(y_ref[...]) * hs).astype(BF16)


def _lru(big, conv_w, conv_b, w_a, b_a, w_i, b_i, lam, bsz, seq):
    m = bsz * seq
    nt = seq // LRU_TIME_TILE
    base = BIG_LRU // LRU_WIDTH
    row1 = lambda v: v.reshape(1, LRU_WIDTH)
    const2 = lambda shape: pl.BlockSpec(shape, lambda b, t: (0, 0))
    const3 = lambda shape: pl.BlockSpec(shape, lambda b, t: (0, 0, 0))
    return pl.pallas_call(
        _lru_kernel,
        out_shape=jax.ShapeDtypeStruct((m, LRU_WIDTH), BF16),
        grid=(bsz, nt),
        in_specs=[
            pl.BlockSpec((LRU_TIME_TILE, LRU_WIDTH), lambda b, t: (b * nt + t, base)),
            pl.BlockSpec((LRU_TIME_TILE, LRU_WIDTH), lambda b, t: (b * nt + t, base + 1)),
            const2((CONV_K, LRU_WIDTH)), const2((1, LRU_WIDTH)),
            const3((LRU_BLOCKS, LRU_BLOCK_W, LRU_BLOCK_W)), const2((1, LRU_WIDTH)),
            const3((LRU_BLOCKS, LRU_BLOCK_W, LRU_BLOCK_W)), const2((1, LRU_WIDTH)),
            const2((1, LRU_WIDTH)),
        ],
        out_specs=pl.BlockSpec((LRU_TIME_TILE, LRU_WIDTH), lambda b, t: (b * nt + t, 0)),
        scratch_shapes=[pltpu.VMEM((HALO + LRU_TIME_TILE, LRU_WIDTH), F32),
                        pltpu.VMEM((SUBLANES, LRU_WIDTH), F32)],
        compiler_params=_params("parallel", "arbitrary"),
        name="rg_lru",
    )(big, big, conv_w, row1(conv_b), w_a.astype(BF16), row1(b_a), w_i.astype(BF16),
      row1(b_i), row1(lam))


def _gla_kernel(q_ref, k_ref, v_ref, z_ref, sm_ref, wg_ref, bg_ref, nw_ref, o_ref, st_scr):
    rows = TIME_TILE

    @pl.when(pl.program_id(1) == 0)
    def _():
        st_scr[...] = jnp.zeros(st_scr.shape, F32)

    nb = GLA_BATCH_TILE
    ri = lax.broadcasted_iota(jnp.int32, (rows, rows), 0)
    ci = lax.broadcasted_iota(jnp.int32, (rows, rows), 1)
    tri = (ci <= ri).astype(F32)
    bcums = []
    for bb in range(nb):
        log_a = _log_sigmoid(_dot(sm_ref[bb].astype(BF16), wg_ref[...]) + bg_ref[...]) / GLA_GATE_NORM
        bcums.append(_dot(tri, log_a, precision=HIGHEST))
    row = lax.broadcasted_iota(jnp.int32, (rows, 1), 0)

    def boundary_rows(bh, width, offset, first_zero):
        out = jnp.zeros_like(bh) if first_zero else None
        for g in range(rows // width):
            src = g * width + offset
            if src < 0:
                continue
            val = jnp.broadcast_to(bh[src:src + 1, :], bh.shape)
            out = val if out is None else jnp.where(row // width == g, val, out)
        return out

    units = [(bb, h) for bb in range(nb) for h in range(GLA_HEADS)]
    heads = range(len(units))
    bhs = [bcums[bb][:, h * GLA_DK:(h + 1) * GLA_DK] for bb, h in units]
    qh = [q_ref[bb, :, h * GLA_DK:(h + 1) * GLA_DK] * (GLA_DK ** -0.5) for bb, h in units]
    kh = [k_ref[bb, :, h * GLA_DK:(h + 1) * GLA_DK] for bb, h in units]
    vh = [v_ref[bb, :, h * GLA_DV:(h + 1) * GLA_DV] for bb, h in units]
    b_loc = [bhs[h] - boundary_rows(bhs[h], GLA_BASE_CHUNK, -1, True) for h in heads]
    base_mask = (ri // GLA_BASE_CHUNK == ci // GLA_BASE_CHUNK) & (ci <= ri)
    att = [jnp.where(base_mask,
                     _bdot_nt(qh[h] * jnp.exp(b_loc[h]), kh[h] * jnp.exp(-b_loc[h])), 0.0)
           for h in heads]
    s = GLA_BASE_CHUNK
    while s < rows:
        mask = ((ri // (2 * s) == ci // (2 * s)) & (ri % (2 * s) >= s) & (ci % (2 * s) < s))
        refs = [boundary_rows(bhs[h], 2 * s, s - 1, False) for h in heads]
        cross = [_bdot_nt(qh[h] * jnp.exp(jnp.minimum(bhs[h] - refs[h], 0.0)),
                          kh[h] * jnp.exp(jnp.minimum(refs[h] - bhs[h], 0.0))) for h in heads]
        att = [att[h] + jnp.where(mask, cross[h], 0.0) for h in heads]
        s *= 2
    sts = [st_scr[bb, h] for bb, h in units]
    o_inter = [_bdot_nt(qh[u] * jnp.exp(bhs[u]), sts[u]) for u in heads]
    o_intra = [_bdot(att[u], vh[u]) for u in heads]
    b_end = [bhs[u][rows - 1:rows, :] for u in heads]
    upd = [_bdot(vh[u].T, kh[u] * jnp.exp(b_end[u] - bhs[u])) for u in heads]
    for u, (bb, h) in enumerate(units):
        st_scr[bb, h] = sts[u] * jnp.exp(b_end[u]) + upd[u]
        o = _rms(o_intra[u] + o_inter[u]) * nw_ref[...]
        zg = _silu(z_ref[bb, :, h * GLA_DV:(h + 1) * GLA_DV])
        o_ref[bb, :, h * GLA_DV:(h + 1) * GLA_DV] = (o * zg).astype(BF16)


def _gla(big, small, w_gate, b_gate, norm_w, bsz, seq):
    nb = GLA_BATCH_TILE
    nt = seq // TIME_TILE
    big3 = big.reshape(bsz, seq, big.shape[1])
    small3 = small.reshape(bsz, seq, SMALL_W)
    wg = jnp.zeros((SMALL_W, GLA_QK_W), F32).at[SMALL_GK:SMALL_GK + GLA_GATE_RANK].set(w_gate)
    qk_base = BIG_GLA // GLA_QK_W
    v_base = (BIG_GLA + 2 * GLA_QK_W) // GLA_V_W
    const = lambda shape: pl.BlockSpec(shape, lambda g, t: (0, 0))
    tile = lambda width, cb: pl.BlockSpec((nb, TIME_TILE, width), lambda g, t: (g, t, cb))
    out = pl.pallas_call(
        _gla_kernel,
        out_shape=jax.ShapeDtypeStruct((bsz, seq, MIX_WIDTH), BF16),
        grid=(bsz // nb, nt),
        in_specs=[
            tile(GLA_QK_W, qk_base), tile(GLA_QK_W, qk_base + 1),
            tile(GLA_V_W, v_base), tile(GLA_V_W, v_base + 1),
            tile(SMALL_W, 0),
            const((SMALL_W, GLA_QK_W)), const((1, GLA_QK_W)), const((1, GLA_DV)),
        ],
        out_specs=tile(MIX_WIDTH, 0),
        scratch_shapes=[pltpu.VMEM((nb, GLA_HEADS, GLA_DV, GLA_DK), F32)],
        compiler_params=_params("parallel", "arbitrary"),
        name="gla",
    )(big3, big3, big3, big3, small3, wg.astype(BF16), b_gate.reshape(1, GLA_QK_W),
      norm_w.reshape(1, GLA_DV))
    return out.reshape(bsz * seq, MIX_WIDTH)


def _merge_kernel(h_ref, wga_ref, wgb_ref, wgc_ref, oa_ref, ob_ref, oc_ref, wb_ref, o_ref):
    h = h_ref[...]
    acc = _sigmoid(_dot(h, wga_ref[...])) * _dot(oa_ref[...], wb_ref[0])
    acc = acc + _sigmoid(_dot(h, wgb_ref[...])) * _dot(ob_ref[...], wb_ref[1])
    acc = acc + _sigmoid(_dot(h, wgc_ref[...])) * _dot(oc_ref[...], wb_ref[2])
    o_ref[...] = acc.astype(BF16)


def _merge(h, oa, ob, oc, w_big, w_branch, layer, seq):
    m = oa.shape[0]
    tm = min(MERGE_ROW_TILE, seq)
    tn = MERGE_COL_TILE
    gate_w = lambda n: pl.BlockSpec(
        (None, D_MODEL, tn), lambda i, j: (layer, 0, (BIG_GATES + n * D_MODEL) // tn + j))
    branch = pl.BlockSpec((tm, MIX_WIDTH), lambda i, j: (i, 0))
    return pl.pallas_call(
        _merge_kernel,
        out_shape=jax.ShapeDtypeStruct((m, D_MODEL), BF16),
        grid=(m // tm, D_MODEL // tn),
        in_specs=[pl.BlockSpec((tm, D_MODEL), lambda i, j: (i, 0)),
                  gate_w(0), gate_w(1), gate_w(2), branch, branch, branch,
                  pl.BlockSpec((None, N_BRANCH, MIX_WIDTH, tn), lambda i, j: (layer, 0, 0, j))],
        out_specs=pl.BlockSpec((tm, tn), lambda i, j: (i, j)),
        compiler_params=_params("parallel", "arbitrary"),
        name="merge",
    )(h, w_big, w_big, w_big, oa, ob, oc, w_branch)


def _out_proj_kernel(m_ref, x_ref, mod_ref, g_ref, w_ref, o_ref):
    y = _dot(m_ref[...], w_ref[...])
    o_ref[...] = x_ref[...] + mod_ref[MOD_GT1:MOD_GT1 + 1, :] * (_rms(y) * g_ref[...])


def _out_proj(merged, x2, mod, g, w_out, layer, seq):
    m = x2.shape[0]
    tm = min(OUT_PROJ_ROW_TILE, seq)
    tiles_per_seq = seq // tm
    return pl.pallas_call(
        _out_proj_kernel,
        out_shape=jax.ShapeDtypeStruct((m, D_MODEL), F32),
        grid=(m // tm,),
        in_specs=[
            pl.BlockSpec((tm, D_MODEL), lambda i: (i, 0)),
            pl.BlockSpec((tm, D_MODEL), lambda i: (i, 0)),
            pl.BlockSpec((None, N_MOD, D_MODEL), lambda i: (i // tiles_per_seq, 0, 0)),
            pl.BlockSpec((1, D_MODEL), lambda i: (0, 0)),
            pl.BlockSpec((None, D_MODEL, D_MODEL), lambda i: (layer, 0, 0)),
        ],
        out_specs=pl.BlockSpec((tm, D_MODEL), lambda i: (i, 0)),
        compiler_params=_params("parallel"),
        name="out_proj",
    )(merged, x2, mod, g, w_out)


def _mlp_kernel(xb_ref, xe_ref, modb_ref, mode_ref, gpre_ref, gpost_ref, w1_ref, w2_ref, o_ref,
                h_scr, acc_scr, *, n_tiles):
    r = pl.program_id(0)
    f = pl.program_id(1)
    chunk = xb_ref.shape[0]
    rows = pl.ds(pl.multiple_of(f * chunk, chunk), chunk)

    def build():
        h = _rms(xb_ref[...]) * gpre_ref[...]
        h = h * (1.0 + modb_ref[MOD_SC2:MOD_SC2 + 1, :]) + modb_ref[MOD_SH2:MOD_SH2 + 1, :]
        h_scr[r % 2, rows, :] = h.astype(BF16)

    def finish():
        y = _rms(acc_scr[r % 2, rows, :]) * gpost_ref[...]
        o_ref[...] = xe_ref[...] + mode_ref[MOD_GT2:MOD_GT2 + 1, :] * y
        acc_scr[r % 2, rows, :] = jnp.zeros((chunk, acc_scr.shape[2]), F32)

    @pl.when(r == 0)
    def _():
        @pl.when(f == 0)
        def _():
            acc_scr[...] = jnp.zeros(acc_scr.shape, F32)

        build()
        o_ref[...] = jnp.zeros(o_ref.shape, F32)

    @pl.when((r >= 1) & (r <= n_tiles))
    def _():
        prev = (r + 1) % 2
        finish()
        a = jnp.maximum(_dot(h_scr[prev], w1_ref[...]), 0.0)
        acc_scr[prev] += _dot((a * a).astype(BF16), w2_ref[...])
        build()

    @pl.when(r == n_tiles + 1)
    def _():
        finish()


def _mlp(x2, mod, g_pre, g_post, w1, w2, layer, seq):
    m = x2.shape[0]
    tm = min(MLP_ROW_TILE, seq)
    tf = MLP_FF_TILE
    n_f = D_FF // tf
    n_i = m // tm
    chunk = tm // n_f
    tiles_per_seq = seq // tm
    build = lambda r: jnp.minimum(r, n_i - 1)
    done = lambda r: jnp.clip(r - 2, 0, n_i - 1)
    wf = lambda r, f: jnp.where(r == 0, 0, jnp.where(r == n_i + 1, n_f - 1, f))
    return pl.pallas_call(
        functools.partial(_mlp_kernel, n_tiles=n_i),
        out_shape=jax.ShapeDtypeStruct((m, D_MODEL), F32),
        grid=(n_i + 2, n_f),
        in_specs=[
            pl.BlockSpec((chunk, D_MODEL), lambda r, f: (build(r) * n_f + f, 0)),
            pl.BlockSpec((chunk, D_MODEL), lambda r, f: (done(r) * n_f + f, 0)),
            pl.BlockSpec((None, N_MOD, D_MODEL), lambda r, f: (build(r) // tiles_per_seq, 0, 0)),
            pl.BlockSpec((None, N_MOD, D_MODEL), lambda r, f: (done(r) // tiles_per_seq, 0, 0)),
            pl.BlockSpec((1, D_MODEL), lambda r, f: (0, 0)),
            pl.BlockSpec((1, D_MODEL), lambda r, f: (0, 0)),
            pl.BlockSpec((None, D_MODEL, tf), lambda r, f: (layer, 0, wf(r, f))),
            pl.BlockSpec((None, tf, D_MODEL), lambda r, f: (layer, wf(r, f), 0)),
        ],
        out_specs=pl.BlockSpec((chunk, D_MODEL),
                               lambda r, f: (done(r) * n_f + jnp.where(r < 2, 0, f), 0)),
        scratch_shapes=[pltpu.VMEM((2, tm, D_MODEL), BF16), pltpu.VMEM((2, tm, D_MODEL), F32)],
        compiler_params=_params("arbitrary", "arbitrary"),
        name="mlp",
    )(x2, x2, mod, mod, g_pre, g_post, w1, w2)


def kernel(x, c, w_ada, b_ada, g_pre_mix, g_post_mix, g_pre_mlp, g_post_mlp, w_in, conv_gdn, gdn_a_log, gdn_dt_bias, gdn_norm, conv_lru, conv_lru_b, lru_w_a, lru_b_a, lru_w_i, lru_b_i, lru_lambda, gla_w_gate, gla_b_gate, gla_norm, w_branch, w_out, w_mlp1, w_mlp2):
    bsz, seq, _ = x.shape
    n_layers = w_in.shape[0]
    assert seq % LRU_TIME_TILE == 0 and x.shape[2] == D_MODEL and bsz % GDN_BATCH_TILE == 0
    row = lambda v: v.reshape(1, D_MODEL)

    mod_all = _ada_mod(c, w_ada, b_ada).reshape(n_layers, bsz, N_MOD, D_MODEL)
    x2 = x.reshape(bsz * seq, D_MODEL)
    w_big, w_small_t = _split_w_in(w_in)
    w_branch_b = w_branch.astype(BF16)
    w_out_b = w_out.astype(BF16)
    w_mlp1_b = w_mlp1.astype(BF16)
    w_mlp2_b = w_mlp2.astype(BF16)
    for l in range(n_layers):
        mod = mod_all[l]
        big, small, small_t, h = _in_proj(x2, mod, row(g_pre_mix[l]), w_big, w_small_t, l, seq)
        oa = _gdn(big, small, small_t, conv_gdn[l], gdn_a_log[l], gdn_dt_bias[l], gdn_norm[l],
                  bsz, seq)
        ob = _lru(big, conv_lru[l], conv_lru_b[l], lru_w_a[l], lru_b_a[l], lru_w_i[l],
                  lru_b_i[l], lru_lambda[l], bsz, seq)
        oc = _gla(big, small, gla_w_gate[l], gla_b_gate[l], gla_norm[l], bsz, seq)
        merged = _merge(h, oa, ob, oc, w_big, w_branch_b, l, seq)
        x2 = _out_proj(merged, x2, mod, row(g_post_mix[l]), w_out_b, l, seq)
        x2 = _mlp(x2, mod, row(g_pre_mlp[l]), row(g_post_mlp[l]), w_mlp1_b, w_mlp2_b, l, seq)
    return x2.reshape(bsz, seq, D_MODEL)
```

```python
import functools

import jax
import jax.numpy as jnp
from jax import lax
from jax.experimental import pallas as pl
from jax.experimental.pallas import tpu as pltpu

F32 = jnp.float32
BF16 = jnp.bfloat16
HIGHEST = lax.Precision.HIGHEST

D_MODEL = 2048
MIX_WIDTH = D_MODEL // 2
N_BRANCH = 3
GDN_DK = 128
GDN_DV = 128
GDN_HEADS = MIX_WIDTH // GDN_DV
GDN_CHUNK = 64
LRU_WIDTH = MIX_WIDTH
LRU_BLOCKS = 8
LRU_BLOCK_W = LRU_WIDTH // LRU_BLOCKS
LRU_C = 8.0
GLA_HEADS = 4
GLA_DV = MIX_WIDTH // GLA_HEADS
GLA_DK = GLA_DV // 2
GLA_GATE_RANK = 16
GLA_GATE_NORM = 16.0
GLA_BASE_CHUNK = 16
CONV_K = 4
D_FF = 4 * D_MODEL
N_MOD = 6
EPS = 1e-6

GDN_QK_W = GDN_HEADS * GDN_DK
GDN_V_W = GDN_HEADS * GDN_DV
GLA_QK_W = GLA_HEADS * GLA_DK
GLA_V_W = GLA_HEADS * GLA_DV
IN_SPLITS = (GDN_QK_W, GDN_QK_W, GDN_V_W, GDN_V_W, GDN_HEADS, GDN_HEADS,
             LRU_WIDTH, LRU_WIDTH,
             GLA_QK_W, GLA_QK_W, GLA_V_W, GLA_V_W, GLA_GATE_RANK,
             N_BRANCH * D_MODEL)
IN_OFFS = tuple(sum(IN_SPLITS[:i]) for i in range(len(IN_SPLITS) + 1))

LANES = 128
SUBLANES = 8
V7X_VMEM_BYTES = 64 * 1024 * 1024
VMEM_LIMIT_BYTES = V7X_VMEM_BYTES - 8 * 1024 * 1024

BIG_GDN = 0
BIG_LRU = BIG_GDN + 4 * MIX_WIDTH
BIG_GLA = BIG_LRU + 2 * LRU_WIDTH
BIG_GATES = BIG_GLA + 2 * GLA_QK_W + 2 * GLA_V_W
BIG_W = BIG_GATES + N_BRANCH * D_MODEL
SMALL_BETA = 0
SMALL_ALPHA = GDN_HEADS
SMALL_GK = 2 * GDN_HEADS
SMALL_W = LANES

MOD_SH1, MOD_SC1, MOD_GT1, MOD_SH2, MOD_SC2, MOD_GT2 = range(N_MOD)

ADA_COL_TILE = 1024
CAST_BLOCK_BYTES = 8 * 1024 * 1024
PROJ_ROW_TILE = 1024
PROJ_COL_TILE = 1536
MERGE_ROW_TILE = 1024
MERGE_COL_TILE = 512
OUT_PROJ_ROW_TILE = 512
MLP_ROW_TILE = 1024
MLP_FF_TILE = 1024
TIME_TILE = 128
GDN_BATCH_TILE = 2
GLA_BATCH_TILE = 4
LRU_TIME_TILE = 512
HALO = SUBLANES

W_PREP_TILE = 512
W_PREP_RUN1_TILE = IN_OFFS[4] // W_PREP_TILE
W_PREP_RUN2_TILE = (IN_OFFS[4] + IN_OFFS[12] - IN_OFFS[6]) // W_PREP_TILE
W_PREP_SHIFT1 = IN_OFFS[6] - IN_OFFS[4]
W_PREP_SHIFT2 = W_PREP_SHIFT1 + IN_OFFS[13] - IN_OFFS[12]
W_PREP_NEXT = W_PREP_SHIFT2
W_PREP_NARROW1 = slice(IN_OFFS[4] - W_PREP_RUN1_TILE * W_PREP_TILE,
                       IN_OFFS[6] - W_PREP_RUN1_TILE * W_PREP_TILE)
W_PREP_NARROW2 = slice(IN_OFFS[12] - W_PREP_RUN2_TILE * W_PREP_TILE,
                       IN_OFFS[13] - W_PREP_RUN2_TILE * W_PREP_TILE)
assert (W_PREP_NARROW1.start, W_PREP_NARROW1.stop) == (SMALL_BETA, SMALL_GK)
assert (W_PREP_NARROW2.start, W_PREP_NARROW2.stop) == (SMALL_GK, SMALL_GK + GLA_GATE_RANK)
assert W_PREP_TILE % W_PREP_NEXT == 0 and IN_OFFS[-1] % W_PREP_NEXT == 0


def _sigmoid(x):
    return 1.0 / (1.0 + jnp.exp(-x))


def _silu(x):
    return x * _sigmoid(x)


def _softplus(x):
    return jnp.maximum(x, 0.0) + jnp.log(1.0 + jnp.exp(-jnp.abs(x)))


def _log_sigmoid(x):
    return -_softplus(-x)


def _gelu_tanh(x):
    c = 0.7978845608028654
    return 0.5 * x * (1.0 + jnp.tanh(c * (x + 0.044715 * (x * x * x))))


def _rms(x):
    return x * lax.rsqrt(jnp.mean(x * x, axis=-1, keepdims=True) + EPS)


def _dot(a, b, precision=None):
    return jnp.dot(a, b, preferred_element_type=F32, precision=precision)


def _dot_nt(a, b):
    return lax.dot_general(a, b, (((1,), (1,)), ((), ())), preferred_element_type=F32)


def _bdot(a, b):
    return _dot(a.astype(BF16), b.astype(BF16))


def _bdot_nt(a, b):
    return _dot_nt(a.astype(BF16), b.astype(BF16))


def _params(*sem):
    return pltpu.CompilerParams(dimension_semantics=sem, vmem_limit_bytes=VMEM_LIMIT_BYTES)


def _ada_kernel(c_ref, w_ref, b_ref, o_ref):
    sc = _silu(c_ref[...]).astype(BF16)
    o_ref[...] = _dot(sc, w_ref[...].astype(BF16)) + b_ref[...]


def _ada_mod(c, w_ada, b_ada):
    n_layers, _, n_out = w_ada.shape
    bsz = c.shape[0]
    tn = ADA_COL_TILE
    return pl.pallas_call(
        _ada_kernel,
        out_shape=jax.ShapeDtypeStruct((n_layers, bsz, n_out), F32),
        grid=(n_layers, n_out // tn),
        in_specs=[
            pl.BlockSpec((bsz, D_MODEL), lambda l, j: (0, 0)),
            pl.BlockSpec((None, D_MODEL, tn), lambda l, j: (l, 0, j)),
            pl.BlockSpec((None, 1, tn), lambda l, j: (l, 0, j)),
        ],
        out_specs=pl.BlockSpec((None, bsz, tn), lambda l, j: (l, 0, j)),
        compiler_params=_params("parallel", "parallel"),
        name="ada_mod",
    )(c, w_ada, b_ada.reshape(n_layers, 1, n_out))


def _w_prep_kernel(a_ref, b_ref, o_ref, wst_ref):
    j = pl.program_id(1)

    def emit(shift):
        src = a_ref[...] if shift == 0 else jnp.concatenate([a_ref[shift:, :], b_ref[:shift, :]],
                                                            axis=0)
        o_ref[...] = src.T.astype(BF16)

    @pl.when(j == 0)
    def _():
        wst_ref[...] = jnp.zeros(wst_ref.shape, BF16)

    @pl.when(j < W_PREP_RUN1_TILE)
    def _():
        emit(0)

    @pl.when((j >= W_PREP_RUN1_TILE) & (j < W_PREP_RUN2_TILE))
    def _():
        emit(W_PREP_SHIFT1)

    @pl.when(j >= W_PREP_RUN2_TILE)
    def _():
        emit(W_PREP_SHIFT2)

    @pl.when(j == W_PREP_RUN1_TILE)
    def _():
        wst_ref[W_PREP_NARROW1, :] = a_ref[W_PREP_NARROW1, :].astype(BF16)

    @pl.when(j == W_PREP_RUN2_TILE)
    def _():
        wst_ref[W_PREP_NARROW2, :] = a_ref[W_PREP_NARROW2, :].astype(BF16)


def _split_w_in(w_in):
    n_layers = w_in.shape[0]
    w_t = jnp.swapaxes(w_in, 1, 2)
    next_per_tile = W_PREP_TILE // W_PREP_NEXT
    w_big, w_small_t = pl.pallas_call(
        _w_prep_kernel,
        out_shape=(jax.ShapeDtypeStruct((n_layers, D_MODEL, BIG_W), BF16),
                   jax.ShapeDtypeStruct((n_layers, SMALL_W, D_MODEL), BF16)),
        grid=(n_layers, BIG_W // W_PREP_TILE),
        in_specs=[
            pl.BlockSpec((None, W_PREP_TILE, D_MODEL), lambda l, j: (l, j, 0)),
            pl.BlockSpec((None, W_PREP_NEXT, D_MODEL), lambda l, j: (l, (j + 1) * next_per_tile, 0)),
        ],
        out_specs=(pl.BlockSpec((None, D_MODEL, W_PREP_TILE), lambda l, j: (l, 0, j)),
                   pl.BlockSpec((None, SMALL_W, D_MODEL), lambda l, j: (l, 0, 0))),
        compiler_params=_params("parallel", "arbitrary"),
        name="w_prep",
    )(w_t, w_t)
    return w_big, w_small_t


def _cast_kernel(x_ref, o_ref):
    o_ref[...] = x_ref[...].astype(BF16)


def _to_bf16(w):
    cols = w.shape[-1]
    w2 = w.reshape(-1, cols)
    rows = w2.shape[0]
    tr = min(rows, max(2 * SUBLANES, CAST_BLOCK_BYTES // (4 * cols)))
    assert rows % tr == 0 and tr % (2 * SUBLANES) == 0
    out = pl.pallas_call(
        _cast_kernel,
        out_shape=jax.ShapeDtypeStruct((rows, cols), BF16),
        grid=(rows // tr,),
        in_specs=[pl.BlockSpec((tr, cols), lambda i: (i, 0))],
        out_specs=pl.BlockSpec((tr, cols), lambda i: (i, 0)),
        compiler_params=_params("parallel"),
        name="to_bf16",
    )(w2)
    return out.reshape(w.shape)


def _in_proj_kernel(xn_ref, modn_ref, g_ref, w_ref, wst_ref,
                    big_ref, small_ref, smallt_ref, h_ref, h_scr, *, chunks_per_tile):
    r = pl.program_id(0)
    j = pl.program_id(1)
    chunk = xn_ref.shape[0]

    def build_chunk():
        h = _rms(xn_ref[...]) * g_ref[...]
        h = h * (1.0 + modn_ref[MOD_SC1:MOD_SC1 + 1, :]) + modn_ref[MOD_SH1:MOD_SH1 + 1, :]
        row0 = pl.multiple_of(jnp.minimum(j, chunks_per_tile - 1) * chunk, chunk)
        h_scr[r % 2, pl.ds(row0, chunk), :] = h.astype(BF16)

    @pl.when(r == 0)
    def _():
        build_chunk()

    @pl.when(r > 0)
    def _():
        prev = (r + 1) % 2

        @pl.when(j == 0)
        def _():
            hb = h_scr[prev]
            h_ref[...] = hb
            st = _dot_nt(wst_ref[...], hb)
            smallt_ref[...] = st
            small_ref[...] = st.T

        big_ref[...] = _dot(h_scr[prev], w_ref[...])
        build_chunk()


def _in_proj(x2, mod, g, w_big, w_small_t, layer, seq):
    m = x2.shape[0]
    tm = min(PROJ_ROW_TILE, seq)
    tn = PROJ_COL_TILE
    n_i = m // tm
    n_j = BIG_GATES // tn
    chunks_per_tile = 1 << (min(n_j, SUBLANES).bit_length() - 1)
    chunk = tm // chunks_per_tile
    tiles_per_seq = seq // tm
    build = lambda r: jnp.minimum(r, n_i - 1)
    out = lambda r: jnp.maximum(r - 1, 0)
    return pl.pallas_call(
        functools.partial(_in_proj_kernel, chunks_per_tile=chunks_per_tile),
        out_shape=(jax.ShapeDtypeStruct((m, BIG_GATES), F32),
                   jax.ShapeDtypeStruct((m, SMALL_W), F32),
                   jax.ShapeDtypeStruct((SMALL_W, m), F32),
                   jax.ShapeDtypeStruct((m, D_MODEL), BF16)),
        grid=(n_i + 1, n_j),
        in_specs=[
            pl.BlockSpec((chunk, D_MODEL), lambda r, j: (
                build(r) * chunks_per_tile + jnp.minimum(j, chunks_per_tile - 1), 0)),
            pl.BlockSpec((None, N_MOD, D_MODEL), lambda r, j: (build(r) // tiles_per_seq, 0, 0)),
            pl.BlockSpec((1, D_MODEL), lambda r, j: (0, 0)),
            pl.BlockSpec((None, D_MODEL, tn), lambda r, j: (layer, 0, j)),
            pl.BlockSpec((None, SMALL_W, D_MODEL), lambda r, j: (layer, 0, 0)),
        ],
        out_specs=(pl.BlockSpec((tm, tn), lambda r, j: (out(r), jnp.where(r == 0, 0, j))),
                   pl.BlockSpec((tm, SMALL_W), lambda r, j: (out(r), 0)),
                   pl.BlockSpec((SMALL_W, tm), lambda r, j: (0, out(r))),
                   pl.BlockSpec((tm, D_MODEL), lambda r, j: (out(r), 0))),
        scratch_shapes=[pltpu.VMEM((2, tm, D_MODEL), BF16)],
        compiler_params=_params("arbitrary", "arbitrary"),
        name="in_proj",
    )(x2, mod, g, w_big, w_small_t)


def _causal_conv(xbuf, cw_ref, rows):
    acc = cw_ref[CONV_K - 1:CONV_K, :] * xbuf[HALO:HALO + rows, :]
    for j in range(CONV_K - 1):
        off = HALO - (CONV_K - 1) + j
        acc = acc + cw_ref[j:j + 1, :] * xbuf[off:off + rows, :]
    return acc


def _gdn_kernel(*refs):
    nb = GDN_BATCH_TILE
    q_ref, k_ref, v_ref, z_ref, sm_ref = refs[:5]
    smt_refs = refs[5:5 + nb]
    cw_ref, alr_ref, alc_ref, dtr_ref, dtc_ref, nw_ref, o_ref, xbuf, s_scr = refs[5 + nb:]
    rows = TIME_TILE

    @pl.when(pl.program_id(1) == 0)
    def _():
        xbuf[:, 0:HALO, :] = jnp.zeros((nb, HALO, xbuf.shape[2]), F32)
        s_scr[...] = jnp.zeros(s_scr.shape, F32)

    ri = lax.broadcasted_iota(jnp.int32, (rows, rows), 0)
    ci = lax.broadcasted_iota(jnp.int32, (rows, rows), 1)
    same = (ri // GDN_CHUNK) == (ci // GDN_CHUNK)
    causal = same & (ci <= ri)
    strict = same & (ci < ri)
    cum_c = causal.astype(F32)
    cum_r = (same & (ri <= ci)).astype(F32)
    eye = (ri == ci).astype(F32)
    level_masks = []
    s = 1
    while s < GDN_CHUNK:
        level_masks.append((ri // (2 * s) == ci // (2 * s)) & (ri % (2 * s) >= s) & (ci % (2 * s) < s))
        s *= 2
    zeros_half = jnp.zeros((GDN_CHUNK, GDN_DV), F32)
    col_row = lax.broadcasted_iota(jnp.int32, (1, rows), 1)
    n_chunks = rows // GDN_CHUNK

    units = [(bb, h) for bb in range(nb) for h in range(GDN_HEADS)]
    idx = range(len(units))
    qs, kts, gccs, gcrs, a_qks, a_kks, rhss = [], [], [], [], [], [], []
    for bb in range(nb):
        xb = xbuf.at[bb]
        xb[HALO:HALO + rows, 0:GDN_QK_W] = q_ref[bb]
        xb[HALO:HALO + rows, GDN_QK_W:2 * GDN_QK_W] = k_ref[bb]
        xb[HALO:HALO + rows, 2 * GDN_QK_W:2 * GDN_QK_W + GDN_V_W] = v_ref[bb]
        qkv = _silu(_causal_conv(xb, cw_ref, rows))
        xb[0:HALO, :] = xb[rows:rows + HALO, :]

        sm = sm_ref[bb]
        smt = smt_refs[bb][...]
        beta_c = _sigmoid(sm)
        g_c = -jnp.exp(alr_ref[...]) * _softplus(sm + dtr_ref[...])
        g_r = -jnp.exp(alc_ref[...]) * _softplus(smt + dtc_ref[...])
        gc_c = _dot(cum_c, g_c, precision=HIGHEST)
        gc_r = _dot(g_r, cum_r, precision=HIGHEST)

        for h in range(GDN_HEADS):
            qh = qkv[:, h * GDN_DK:(h + 1) * GDN_DK]
            kh = qkv[:, GDN_QK_W + h * GDN_DK:GDN_QK_W + (h + 1) * GDN_DK]
            vh = qkv[:, 2 * GDN_QK_W + h * GDN_DV:2 * GDN_QK_W + (h + 1) * GDN_DV]
            qh = qh * lax.rsqrt(jnp.sum(qh * qh, axis=-1, keepdims=True) + EPS) * (GDN_DK ** -0.5)
            kh = kh * lax.rsqrt(jnp.sum(kh * kh, axis=-1, keepdims=True) + EPS)
            bc = beta_c[:, SMALL_BETA + h:SMALL_BETA + h + 1]
            gcc = gc_c[:, SMALL_ALPHA + h:SMALL_ALPHA + h + 1]
            gcr = gc_r[SMALL_ALPHA + h:SMALL_ALPHA + h + 1, :]
            e = jnp.exp(jnp.where(causal, gcc - gcr, 0.0))
            kb = kh * bc
            kt = kh.T
            qk_kk = _bdot(jnp.concatenate([qh, kb], axis=0), kt)
            eg = jnp.exp(gcc)
            qs.append(qh * eg)
            kts.append(kt)
            gccs.append(gcc)
            gcrs.append(gcr)
            a_qks.append(jnp.where(causal, qk_kk[:rows] * e, 0.0))
            a_kks.append(jnp.where(strict, qk_kk[rows:] * e, 0.0))
            rhss.append(jnp.concatenate([vh * bc, kb * eg], axis=1))

    tinv = [eye - jnp.where(level_masks[0], a_kks[u], 0.0) for u in idx]
    for lm in level_masks[1:]:
        tb = [tinv[u].astype(BF16) for u in idx]
        et = [_dot(jnp.where(lm, a_kks[u], 0.0).astype(BF16), tb[u]) for u in idx]
        tinv = [tinv[u] - _dot(tb[u], et[u].astype(BF16)) for u in idx]
    uws = [_bdot(tinv[u], rhss[u]) for u in idx]

    states = [s_scr[bb, h] for bb, h in units]
    outs = [[] for _ in idx]
    for c in range(n_chunks):
        lo = c * GDN_CHUNK
        hi = lo + GDN_CHUNK
        in_chunk = (col_row >= lo) & (col_row < hi)
        rs = [_bdot(jnp.concatenate([uws[u][lo:hi, GDN_DV:], qs[u][lo:hi]], axis=0), states[u])
              for u in idx]
        v_pads = []
        for u in idx:
            pads = [zeros_half] * n_chunks
            pads[c] = uws[u][lo:hi, :GDN_DV] - rs[u][:GDN_CHUNK]
            v_pads.append(jnp.concatenate(pads, axis=0))
        new_states = []
        for u in idx:
            g_last = gccs[u][hi - 1:hi, :]
            tail = jnp.where(in_chunk, jnp.exp(jnp.where(in_chunk, g_last - gcrs[u], 0.0)), 0.0)
            new_states.append(states[u] * jnp.exp(g_last) + _bdot(kts[u] * tail, v_pads[u]))
        states = new_states
        for u in idx:
            outs[u].append(rs[u][GDN_CHUNK:] + _bdot(a_qks[u][lo:hi, :], v_pads[u]))
    for u, (bb, h) in enumerate(units):
        s_scr[bb, h] = states[u]
        o = _rms(jnp.concatenate(outs[u], axis=0)) * nw_ref[...]
        zg = _silu(z_ref[bb, :, h * GDN_DV:(h + 1) * GDN_DV])
        o_ref[bb, :, h * GDN_DV:(h + 1) * GDN_DV] = (o * zg).astype(BF16)


def _gdn(big, small, small_t, conv_w, a_log, dt_bias, norm_w, bsz, seq):
    nb = GDN_BATCH_TILE
    nt = seq // TIME_TILE
    qkv_w = 2 * GDN_QK_W + GDN_V_W
    big3 = big.reshape(bsz, seq, big.shape[1])
    small3 = small.reshape(bsz, seq, SMALL_W)

    def pad_row(v):
        return jnp.zeros((1, SMALL_W), F32).at[0, SMALL_ALPHA:SMALL_ALPHA + GDN_HEADS].set(v)

    alr = pad_row(a_log)
    dtr = pad_row(dt_bias)
    col = lambda cb: pl.BlockSpec((nb, TIME_TILE, MIX_WIDTH), lambda g, t: (g, t, cb))
    const = lambda shape: pl.BlockSpec(shape, lambda g, t: (0, 0))
    small_t_spec = lambda bb: pl.BlockSpec(
        (SMALL_W, TIME_TILE), lambda g, t: (0, (g * nb + bb) * nt + t))
    out = pl.pallas_call(
        _gdn_kernel,
        out_shape=jax.ShapeDtypeStruct((bsz, seq, MIX_WIDTH), BF16),
        grid=(bsz // nb, nt),
        in_specs=[
            col(0), col(1), col(2), col(3),
            pl.BlockSpec((nb, TIME_TILE, SMALL_W), lambda g, t: (g, t, 0)),
            *[small_t_spec(bb) for bb in range(nb)],
            const((CONV_K, qkv_w)),
            const((1, SMALL_W)), const((SMALL_W, 1)),
            const((1, SMALL_W)), const((SMALL_W, 1)),
            const((1, GDN_DV)),
        ],
        out_specs=pl.BlockSpec((nb, TIME_TILE, MIX_WIDTH), lambda g, t: (g, t, 0)),
        scratch_shapes=[pltpu.VMEM((nb, HALO + TIME_TILE, qkv_w), F32),
                        pltpu.VMEM((nb, GDN_HEADS, GDN_DK, GDN_DV), F32)],
        compiler_params=_params("parallel", "arbitrary"),
        name="gdn",
    )(big3, big3, big3, big3, small3, *([small_t] * nb), conv_w, alr, alr.T, dtr, dtr.T,
      norm_w.reshape(1, GDN_DV))
    return out.reshape(bsz * seq, MIX_WIDTH)


def _lru_kernel(x_ref, y_ref, cw_ref, cb_ref, wa_ref, ba_ref, wi_ref, bi_ref, lam_ref,
                o_ref, xbuf, h_scr):
    rows = LRU_TIME_TILE
    first = pl.program_id(1) == 0

    @pl.when(first)
    def _():
        xbuf[0:HALO, :] = jnp.zeros((HALO, LRU_WIDTH), F32)
        h_scr[...] = jnp.zeros(h_scr.shape, F32)

    xbuf[HALO:HALO + rows, :] = x_ref[...]
    xl = _causal_conv(xbuf, cw_ref, rows) + cb_ref[...]
    xbuf[0:HALO, :] = xbuf[rows:rows + HALO, :]

    xlb = xl.astype(BF16)
    r_parts = []
    i_parts = []
    for k in range(LRU_BLOCKS):
        blk = xlb[:, k * LRU_BLOCK_W:(k + 1) * LRU_BLOCK_W]
        r_parts.append(_dot(blk, wa_ref[k]))
        i_parts.append(_dot(blk, wi_ref[k]))
    r = _sigmoid(jnp.concatenate(r_parts, axis=1) + ba_ref[...])
    ig = _sigmoid(jnp.concatenate(i_parts, axis=1) + bi_ref[...])
    log_a = -LRU_C * r * _softplus(-lam_ref[...])
    a = jnp.exp(log_a)
    mult = jnp.sqrt(-jnp.tanh(log_a) * (a * a + 1.0))
    row = lax.broadcasted_iota(jnp.int32, (rows, 1), 0)
    mult = jnp.where(first & (row == 0), 1.0, mult)
    b = mult * ig * xl

    row8 = row[:SUBLANES]
    shift = 1
    while shift < SUBLANES:
        a_roll = pltpu.roll(a, shift, 0)
        b_roll = pltpu.roll(b, shift, 0)
        keep = row8 >= shift
        a_prev = jnp.concatenate([jnp.where(keep, a_roll[:SUBLANES], 1.0), a_roll[SUBLANES:]], axis=0)
        b_prev = jnp.concatenate([jnp.where(keep, b_roll[:SUBLANES], 0.0), b_roll[SUBLANES:]], axis=0)
        b = a * b_prev + b
        a = a * a_prev
        shift *= 2
    hg = a[:SUBLANES] * h_scr[0:1, :] + b[:SUBLANES]
    groups = [hg]
    for lo in range(SUBLANES, rows, SUBLANES):
        hg = a[lo:lo + SUBLANES] * hg + b[lo:lo + SUBLANES]
        groups.append(hg)
    hs = jnp.concatenate(groups, axis=0)
    h_scr[0:1, :] = hs[rows - 1:rows, :]
    o_ref[...] = (_gelu_tanh(y_ref[...]) * hs).astype(BF16)


def _lru(big, conv_w, conv_b, w_a, b_a, w_i, b_i, lam, bsz, seq):
    m = bsz * seq
    nt = seq // LRU_TIME_TILE
    base = BIG_LRU // LRU_WIDTH
    row1 = lambda v: v.reshape(1, LRU_WIDTH)
    const2 = lambda shape: pl.BlockSpec(shape, lambda b, t: (0, 0))
    const3 = lambda shape: pl.BlockSpec(shape, lambda b, t: (0, 0, 0))
    return pl.pallas_call(
        _lru_kernel,
        out_shape=jax.ShapeDtypeStruct((m, LRU_WIDTH), BF16),
        grid=(bsz, nt),
        in_specs=[
            pl.BlockSpec((LRU_TIME_TILE, LRU_WIDTH), lambda b, t: (b * nt + t, base)),
            pl.BlockSpec((LRU_TIME_TILE, LRU_WIDTH), lambda b, t: (b * nt + t, base + 1)),
            const2((CONV_K, LRU_WIDTH)), const2((1, LRU_WIDTH)),
            const3((LRU_BLOCKS, LRU_BLOCK_W, LRU_BLOCK_W)), const2((1, LRU_WIDTH)),
            const3((LRU_BLOCKS, LRU_BLOCK_W, LRU_BLOCK_W)), const2((1, LRU_WIDTH)),
            const2((1, LRU_WIDTH)),
        ],
        out_specs=pl.BlockSpec((LRU_TIME_TILE, LRU_WIDTH), lambda b, t: (b * nt + t, 0)),
        scratch_shapes=[pltpu.VMEM((HALO + LRU_TIME_TILE, LRU_WIDTH), F32),
                        pltpu.VMEM((SUBLANES, LRU_WIDTH), F32)],
        compiler_params=_params("parallel", "arbitrary"),
        name="rg_lru",
    )(big, big, conv_w, row1(conv_b), w_a.astype(BF16), row1(b_a), w_i.astype(BF16),
      row1(b_i), row1(lam))


def _gla_kernel(q_ref, k_ref, v_ref, z_ref, sm_ref, wg_ref, bg_ref, nw_ref, o_ref, st_scr):
    rows = TIME_TILE

    @pl.when(pl.program_id(1) == 0)
    def _():
        st_scr[...] = jnp.zeros(st_scr.shape, F32)

    nb = GLA_BATCH_TILE
    ri = lax.broadcasted_iota(jnp.int32, (rows, rows), 0)
    ci = lax.broadcasted_iota(jnp.int32, (rows, rows), 1)
    tri = (ci <= ri).astype(F32)
    bcums = []
    for bb in range(nb):
        log_a = _log_sigmoid(_dot(sm_ref[bb].astype(BF16), wg_ref[...]) + bg_ref[...]) / GLA_GATE_NORM
        bcums.append(_dot(tri, log_a, precision=HIGHEST))
    row = lax.broadcasted_iota(jnp.int32, (rows, 1), 0)

    def boundary_rows(bh, width, offset, first_zero):
        out = jnp.zeros_like(bh) if first_zero else None
        for g in range(rows // width):
            src = g * width + offset
            if src < 0:
                continue
            val = jnp.broadcast_to(bh[src:src + 1, :], bh.shape)
            out = val if out is None else jnp.where(row // width == g, val, out)
        return out

    units = [(bb, h) for bb in range(nb) for h in range(GLA_HEADS)]
    heads = range(len(units))
    bhs = [bcums[bb][:, h * GLA_DK:(h + 1) * GLA_DK] for bb, h in units]
    qh = [q_ref[bb, :, h * GLA_DK:(h + 1) * GLA_DK] * (GLA_DK ** -0.5) for bb, h in units]
    kh = [k_ref[bb, :, h * GLA_DK:(h + 1) * GLA_DK] for bb, h in units]
    vh = [v_ref[bb, :, h * GLA_DV:(h + 1) * GLA_DV] for bb, h in units]
    b_loc = [bhs[h] - boundary_rows(bhs[h], GLA_BASE_CHUNK, -1, True) for h in heads]
    base_mask = (ri // GLA_BASE_CHUNK == ci // GLA_BASE_CHUNK) & (ci <= ri)
    att = [jnp.where(base_mask,
                     _bdot_nt(qh[h] * jnp.exp(b_loc[h]), kh[h] * jnp.exp(-b_loc[h])), 0.0)
           for h in heads]
    s = GLA_BASE_CHUNK
    while s < rows:
        mask = ((ri // (2 * s) == ci // (2 * s)) & (ri % (2 * s) >= s) & (ci % (2 * s) < s))
        refs = [boundary_rows(bhs[h], 2 * s, s - 1, False) for h in heads]
        cross = [_bdot_nt(qh[h] * jnp.exp(jnp.minimum(bhs[h] - refs[h], 0.0)),
                          kh[h] * jnp.exp(jnp.minimum(refs[h] - bhs[h], 0.0))) for h in heads]
        att = [att[h] + jnp.where(mask, cross[h], 0.0) for h in heads]
        s *= 2
    sts = [st_scr[bb, h] for bb, h in units]
    o_inter = [_bdot_nt(qh[u] * jnp.exp(bhs[u]), sts[u]) for u in heads]
    o_intra = [_bdot(att[u], vh[u]) for u in heads]
    b_end = [bhs[u][rows - 1:rows, :] for u in heads]
    upd = [_bdot(vh[u].T, kh[u] * jnp.exp(b_end[u] - bhs[u])) for u in heads]
    for u, (bb, h) in enumerate(units):
        st_scr[bb, h] = sts[u] * jnp.exp(b_end[u]) + upd[u]
        o = _rms(o_intra[u] + o_inter[u]) * nw_ref[...]
        zg = _silu(z_ref[bb, :, h * GLA_DV:(h + 1) * GLA_DV])
        o_ref[bb, :, h * GLA_DV:(h + 1) * GLA_DV] = (o * zg).astype(BF16)


def _gla(big, small, w_gate, b_gate, norm_w, bsz, seq):
    nb = GLA_BATCH_TILE
    nt = seq // TIME_TILE
    big3 = big.reshape(bsz, seq, big.shape[1])
    small3 = small.reshape(bsz, seq, SMALL_W)
    wg = jnp.zeros((SMALL_W, GLA_QK_W), F32).at[SMALL_GK:SMALL_GK + GLA_GATE_RANK].set(w_gate)
    qk_base = BIG_GLA // GLA_QK_W
    v_base = (BIG_GLA + 2 * GLA_QK_W) // GLA_V_W
    const = lambda shape: pl.BlockSpec(shape, lambda g, t: (0, 0))
    tile = lambda width, cb: pl.BlockSpec((nb, TIME_TILE, width), lambda g, t: (g, t, cb))
    out = pl.pallas_call(
        _gla_kernel,
        out_shape=jax.ShapeDtypeStruct((bsz, seq, MIX_WIDTH), BF16),
        grid=(bsz // nb, nt),
        in_specs=[
            tile(GLA_QK_W, qk_base), tile(GLA_QK_W, qk_base + 1),
            tile(GLA_V_W, v_base), tile(GLA_V_W, v_base + 1),
            tile(SMALL_W, 0),
            const((SMALL_W, GLA_QK_W)), const((1, GLA_QK_W)), const((1, GLA_DV)),
        ],
        out_specs=tile(MIX_WIDTH, 0),
        scratch_shapes=[pltpu.VMEM((nb, GLA_HEADS, GLA_DV, GLA_DK), F32)],
        compiler_params=_params("parallel", "arbitrary"),
        name="gla",
    )(big3, big3, big3, big3, small3, wg.astype(BF16), b_gate.reshape(1, GLA_QK_W),
      norm_w.reshape(1, GLA_DV))
    return out.reshape(bsz * seq, MIX_WIDTH)


def _merge_kernel(h_ref, wga_ref, wgb_ref, wgc_ref, oa_ref, ob_ref, oc_ref, wb_ref, o_ref):
    h = h_ref[...]
    acc = _sigmoid(_dot(h, wga_ref[...])) * _dot(oa_ref[...], wb_ref[0])
    acc = acc + _sigmoid(_dot(h, wgb_ref[...])) * _dot(ob_ref[...], wb_ref[1])
    acc = acc + _sigmoid(_dot(h, wgc_ref[...])) * _dot(oc_ref[...], wb_ref[2])
    o_ref[...] = acc.astype(BF16)


def _merge(h, oa, ob, oc, w_big, w_branch, layer, seq):
    m = oa.shape[0]
    tm = min(MERGE_ROW_TILE, seq)
    tn = MERGE_COL_TILE
    gate_w = lambda n: pl.BlockSpec(
        (None, D_MODEL, tn), lambda i, j: (layer, 0, (BIG_GATES + n * D_MODEL) // tn + j))
    branch = pl.BlockSpec((tm, MIX_WIDTH), lambda i, j: (i, 0))
    return pl.pallas_call(
        _merge_kernel,
        out_shape=jax.ShapeDtypeStruct((m, D_MODEL), BF16),
        grid=(m // tm, D_MODEL // tn),
        in_specs=[pl.BlockSpec((tm, D_MODEL), lambda i, j: (i, 0)),
                  gate_w(0), gate_w(1), gate_w(2), branch, branch, branch,
                  pl.BlockSpec((None, N_BRANCH, MIX_WIDTH, tn), lambda i, j: (layer, 0, 0, j))],
        out_specs=pl.BlockSpec((tm, tn), lambda i, j: (i, j)),
        compiler_params=_params("parallel", "arbitrary"),
        name="merge",
    )(h, w_big, w_big, w_big, oa, ob, oc, w_branch)


def _out_proj_kernel(m_ref, x_ref, mod_ref, g_ref, w_ref, o_ref):
    y = _dot(m_ref[...], w_ref[...])
    o_ref[...] = x_ref[...] + mod_ref[MOD_GT1:MOD_GT1 + 1, :] * (_rms(y) * g_ref[...])


def _out_proj(merged, x2, mod, g, w_out, layer, seq):
    m = x2.shape[0]
    tm = min(OUT_PROJ_ROW_TILE, seq)
    tiles_per_seq = seq // tm
    return pl.pallas_call(
        _out_proj_kernel,
        out_shape=jax.ShapeDtypeStruct((m, D_MODEL), F32),
        grid=(m // tm,),
        in_specs=[
            pl.BlockSpec((tm, D_MODEL), lambda i: (i, 0)),
            pl.BlockSpec((tm, D_MODEL), lambda i: (i, 0)),
            pl.BlockSpec((None, N_MOD, D_MODEL), lambda i: (i // tiles_per_seq, 0, 0)),
            pl.BlockSpec((1, D_MODEL), lambda i: (0, 0)),
            pl.BlockSpec((None, D_MODEL, D_MODEL), lambda i: (layer, 0, 0)),
        ],
        out_specs=pl.BlockSpec((tm, D_MODEL), lambda i: (i, 0)),
        compiler_params=_params("parallel"),
        name="out_proj",
    )(merged, x2, mod, g, w_out)


def _mlp_kernel(xb_ref, xe_ref, modb_ref, mode_ref, gpre_ref, gpost_ref, w1_ref, w2_ref, o_ref,
                h_scr, acc_scr, *, n_tiles):
    r = pl.program_id(0)
    f = pl.program_id(1)
    chunk = xb_ref.shape[0]
    rows = pl.ds(pl.multiple_of(f * chunk, chunk), chunk)

    def build():
        h = _rms(xb_ref[...]) * gpre_ref[...]
        h = h * (1.0 + modb_ref[MOD_SC2:MOD_SC2 + 1, :]) + modb_ref[MOD_SH2:MOD_SH2 + 1, :]
        h_scr[r % 2, rows, :] = h.astype(BF16)

    def finish():
        y = _rms(acc_scr[r % 2, rows, :]) * gpost_ref[...]
        o_ref[...] = xe_ref[...] + mode_ref[MOD_GT2:MOD_GT2 + 1, :] * y
        acc_scr[r % 2, rows, :] = jnp.zeros((chunk, acc_scr.shape[2]), F32)

    @pl.when(r == 0)
    def _():
        @pl.when(f == 0)
        def _():
            acc_scr[...] = jnp.zeros(acc_scr.shape, F32)

        build()
        o_ref[...] = jnp.zeros(o_ref.shape, F32)

    @pl.when((r >= 1) & (r <= n_tiles))
    def _():
        prev = (r + 1) % 2
        finish()
        a = jnp.maximum(_dot(h_scr[prev], w1_ref[...]), 0.0)
        acc_scr[prev] += _dot((a * a).astype(BF16), w2_ref[...])
        build()

    @pl.when(r == n_tiles + 1)
    def _():
        finish()


def _mlp(x2, mod, g_pre, g_post, w1, w2, layer, seq):
    m = x2.shape[0]
    tm = min(MLP_ROW_TILE, seq)
    tf = MLP_FF_TILE
    n_f = D_FF // tf
    n_i = m // tm
    chunk = tm // n_f
    tiles_per_seq = seq // tm
    build = lambda r: jnp.minimum(r, n_i - 1)
    done = lambda r: jnp.clip(r - 2, 0, n_i - 1)
    wf = lambda r, f: jnp.where(r == 0, 0, jnp.where(r == n_i + 1, n_f - 1, f))
    return pl.pallas_call(
        functools.partial(_mlp_kernel, n_tiles=n_i),
        out_shape=jax.ShapeDtypeStruct((m, D_MODEL), F32),
        grid=(n_i + 2, n_f),
        in_specs=[
            pl.BlockSpec((chunk, D_MODEL), lambda r, f: (build(r) * n_f + f, 0)),
            pl.BlockSpec((chunk, D_MODEL), lambda r, f: (done(r) * n_f + f, 0)),
            pl.BlockSpec((None, N_MOD, D_MODEL), lambda r, f: (build(r) // tiles_per_seq, 0, 0)),
            pl.BlockSpec((None, N_MOD, D_MODEL), lambda r, f: (done(r) // tiles_per_seq, 0, 0)),
            pl.BlockSpec((1, D_MODEL), lambda r, f: (0, 0)),
            pl.BlockSpec((1, D_MODEL), lambda r, f: (0, 0)),
            pl.BlockSpec((None, D_MODEL, tf), lambda r, f: (layer, 0, wf(r, f))),
            pl.BlockSpec((None, tf, D_MODEL), lambda r, f: (layer, wf(r, f), 0)),
        ],
        out_specs=pl.BlockSpec((chunk, D_MODEL),
                               lambda r, f: (done(r) * n_f + jnp.where(r < 2, 0, f), 0)),
        scratch_shapes=[pltpu.VMEM((2, tm, D_MODEL), BF16), pltpu.VMEM((2, tm, D_MODEL), F32)],
        compiler_params=_params("arbitrary", "arbitrary"),
        name="mlp",
    )(x2, x2, mod, mod, g_pre, g_post, w1, w2)


def kernel(x, c, w_ada, b_ada, g_pre_mix, g_post_mix, g_pre_mlp, g_post_mlp, w_in, conv_gdn, gdn_a_log, gdn_dt_bias, gdn_norm, conv_lru, conv_lru_b, lru_w_a, lru_b_a, lru_w_i, lru_b_i, lru_lambda, gla_w_gate, gla_b_gate, gla_norm, w_branch, w_out, w_mlp1, w_mlp2):
    bsz, seq, _ = x.shape
    n_layers = w_in.shape[0]
    assert seq % LRU_TIME_TILE == 0 and x.shape[2] == D_MODEL and bsz % GDN_BATCH_TILE == 0
    row = lambda v: v.reshape(1, D_MODEL)

    mod_all = _ada_mod(c, w_ada, b_ada).reshape(n_layers, bsz, N_MOD, D_MODEL)
    x2 = x.reshape(bsz * seq, D_MODEL)
    w_big, w_small_t = _split_w_in(w_in)
    w_branch_b = _to_bf16(w_branch)
    w_out_b = _to_bf16(w_out)
    w_mlp1_b = _to_bf16(w_mlp1)
    w_mlp2_b = _to_bf16(w_mlp2)
    for l in range(n_layers):
        mod = mod_all[l]
        big, small, small_t, h = _in_proj(x2, mod, row(g_pre_mix[l]), w_big, w_small_t, l, seq)
        oa = _gdn(big, small, small_t, conv_gdn[l], gdn_a_log[l], gdn_dt_bias[l], gdn_norm[l],
                  bsz, seq)
        ob = _lru(big, conv_lru[l], conv_lru_b[l], lru_w_a[l], lru_b_a[l], lru_w_i[l],
                  lru_b_i[l], lru_lambda[l], bsz, seq)
        oc = _gla(big, small, gla_w_gate[l], gla_b_gate[l], gla_norm[l], bsz, seq)
        merged = _merge(h, oa, ob, oc, w_big, w_branch_b, l, seq)
        x2 = _out_proj(merged, x2, mod, row(g_post_mix[l]), w_out_b, l, seq)
        x2 = _mlp(x2, mod, row(g_pre_mlp[l]), row(g_post_mlp[l]), w_mlp1_b, w_mlp2_b, l, seq)
    return x2.reshape(bsz, seq, D_MODEL)
```
